```python
import math
import jax, jax.numpy as jnp
from jax import lax
import numpy as np

D_MODEL = 2048
BATCH = 16
SEQ = 2048
DEPTH = 4

CHUNK = 64
Q_BLOCK = 128
N_EVEN = (DEPTH + 1) // 2
N_ODD = DEPTH // 2
EPS = 1e-6
ROPE_THETA = 10000.0
MAX_OFFSET = 4096

MLA_HEADS = 8
MLA_NOPE = 128
MLA_ROPE = 64
MLA_VDIM = 128
Q_LORA = 512
KV_LORA = 256
MLA_OUT = MLA_HEADS * MLA_VDIM

SGU_BLOCK = 128
SGU_GROUPS = 8
SGU_CH = 128
SGU_WIDTH = SGU_GROUPS * SGU_CH

EVEN_IN_WIDTH = Q_LORA + KV_LORA + MLA_ROPE + 2 * SGU_WIDTH
EVEN_SPLITS = (Q_LORA, Q_LORA + KV_LORA, Q_LORA + KV_LORA + MLA_ROPE,
               Q_LORA + KV_LORA + MLA_ROPE + SGU_WIDTH)
EVEN_OUT_WIDTH = MLA_OUT + SGU_WIDTH

RET_HEADS = 8
RET_DK = D_MODEL // RET_HEADS
RET_DV = 2 * RET_DK
RET_QK_WIDTH = RET_HEADS * RET_DK
RET_V_WIDTH = RET_HEADS * RET_DV
RET_IN_WIDTH = 2 * RET_QK_WIDTH + 2 * RET_V_WIDTH
RET_SPLITS = (RET_QK_WIDTH, 2 * RET_QK_WIDTH, 2 * RET_QK_WIDTH + RET_V_WIDTH)

D_FF = 5632
CONV_WIDTH = 3

PLE_DIM = 256

kernel_name = "hybrid_mla_gmlp_retention_convffn"


def _rms_norm(x, g):
    xf = x.astype(jnp.float32)
    y = xf * lax.rsqrt(jnp.mean(xf * xf, axis=-1, keepdims=True) + EPS)
    return (y * g.astype(jnp.float32)).astype(x.dtype)


def _layer_norm(x, g, b):
    xf = x.astype(jnp.float32)
    mu = jnp.mean(xf, axis=-1, keepdims=True)
    xc = xf - mu
    y = xc * lax.rsqrt(jnp.mean(xc * xc, axis=-1, keepdims=True) + EPS)
    return (y * g.astype(jnp.float32) + b.astype(jnp.float32)).astype(x.dtype)


def _rope(x, pos):
    half = x.shape[-1] // 2
    inv_freq = ROPE_THETA ** (-jnp.arange(half, dtype=jnp.float32) / half)
    ang = pos.astype(jnp.float32)[..., None] * inv_freq
    cos = jnp.cos(ang)[:, :, None, :]
    sin = jnp.sin(ang)[:, :, None, :]
    xf = x.astype(jnp.float32)
    x1, x2 = xf[..., :half], xf[..., half:]
    return jnp.concatenate([x1 * cos - x2 * sin, x2 * cos + x1 * sin], axis=-1).astype(x.dtype)


def _mla(c_q, c_kv, k_pe, pos, q_norm_g, w_q_up, kv_norm_g, w_kv_up):
    B, S, _ = c_q.shape
    q = (_rms_norm(c_q, q_norm_g) @ w_q_up).reshape(B, S, MLA_HEADS, MLA_NOPE + MLA_ROPE)
    q = jnp.concatenate([q[..., :MLA_NOPE], _rope(q[..., MLA_NOPE:], pos)], axis=-1)
    kv = (_rms_norm(c_kv, kv_norm_g) @ w_kv_up).reshape(B, S, MLA_HEADS, MLA_NOPE + MLA_VDIM)
    k_rot = _rope(k_pe[:, :, None, :], pos)
    k = jnp.concatenate([kv[..., :MLA_NOPE],
                         jnp.broadcast_to(k_rot, (B, S, MLA_HEADS, MLA_ROPE))], axis=-1)
    v = kv[..., MLA_NOPE:]
    scale = (MLA_NOPE + MLA_ROPE) ** -0.5
    n_qb = S // Q_BLOCK
    q_blocks = q.reshape(B, n_qb, Q_BLOCK, MLA_HEADS, MLA_NOPE + MLA_ROPE).swapaxes(0, 1)
    key_chunk = jnp.arange(S) // CHUNK

    def attend(args):
        qb, qi = args
        s = jnp.einsum('bqhd,bkhd->bhqk', qb, k, preferred_element_type=jnp.float32) * scale
        q_chunk = (qi * Q_BLOCK + jnp.arange(Q_BLOCK)) // CHUNK
        mask = key_chunk[None, :] <= q_chunk[:, None]
        s = jnp.where(mask, s, -jnp.inf)
        pr = jax.nn.softmax(s, axis=-1).astype(v.dtype)
        return jnp.einsum('bhqk,bkhd->bqhd', pr, v)

    o = lax.map(attend, (q_blocks, jnp.arange(n_qb)))
    return o.swapaxes(0, 1).reshape(B, S, MLA_OUT)


def _sgu(u, v, ln_g, ln_b, w_s, b_s):
    B, S, _ = u.shape
    u = jax.nn.gelu(u)
    v = _layer_norm(jax.nn.gelu(v), ln_g, ln_b)
    vr = v.reshape(B, S // SGU_BLOCK, SGU_BLOCK, SGU_GROUPS, SGU_CH)
    pos_chunk = jnp.arange(SGU_BLOCK) // CHUNK
    w = jnp.where(pos_chunk[:, None] >= pos_chunk[None, :], w_s, 0.0)
    s = jnp.einsum('gpq,bnqgc->bnpgc', w, vr) + b_s.T[None, None, :, :, None]
    return u * s.reshape(B, S, SGU_WIDTH)


def _retention(q, k, v, pos):
    B, S, H, _ = q.shape
    n = S // CHUNK
    q = _rope(q, pos)
    k = _rope(k, pos) * (RET_DK ** -0.5)
    log_g = jnp.log1p(-(2.0 ** (-5.0 - jnp.arange(H, dtype=jnp.float32))))
    idx = jnp.arange(CHUNK, dtype=jnp.float32)
    intra_decay = jnp.exp(log_g[:, None, None] * jnp.abs(idx[:, None] - idx[None, :]))
    k_decay = jnp.exp(log_g[None, :] * (CHUNK - 1 - idx)[:, None])
    q_decay = jnp.exp(log_g[None, :] * (idx + 1.0)[:, None])
    chunk_decay = jnp.exp(log_g * CHUNK)
    qc = q.reshape(B, n, CHUNK, H, RET_DK)
    kc = k.reshape(B, n, CHUNK, H, RET_DK)
    vc = v.reshape(B, n, CHUNK, H, RET_DV)
    s = jnp.einsum('bnqhd,bnkhd->bnhqk', qc, kc, preferred_element_type=jnp.float32) * intra_decay
    intra = jnp.einsum('bnhqk,bnkhe->bnqhe', s, vc.astype(jnp.float32))

    def step(state, xs):
        q_i, k_i, v_i = xs
        cross = jnp.einsum('bqhd,bhde->bqhe', q_i.astype(jnp.float32), state) * q_decay[None, :, :, None]
        kz = k_i.astype(jnp.float32) * k_decay[None, :, :, None]
        state = state * chunk_decay[None, :, None, None] + jnp.einsum('bkhd,bkhe->bhde', kz, v_i.astype(jnp.float32))
        return state, cross

    state0 = jnp.zeros((B, H, RET_DK, RET_DV), jnp.float32)
    _, cross = lax.scan(step, state0, (qc.swapaxes(0, 1), kc.swapaxes(0, 1), vc.swapaxes(0, 1)))
    out = intra + cross.swapaxes(0, 1)
    return out.reshape(B, S, H, RET_DV)


def _conv_ffn(h, w_up, conv_w, conv_b, w_down):
    a = h @ w_up
    S = a.shape[1]
    ap = jnp.pad(a, ((0, 0), (CONV_WIDTH - 1, 0), (0, 0)))
    c = conv_b
    for j in range(CONV_WIDTH):
        c = c + ap[:, j:j + S] * conv_w[j]
    gate, val = jnp.split(c, 2, axis=-1)
    return (jax.nn.gelu(gate) * val) @ w_down


def setup_inputs(seed: int = 0) -> dict:
    key = jax.random.key(seed)
    ks = jax.random.split(key, 32)
    f32 = jnp.float32

    def nrm(k, shape, scale):
        return jax.random.normal(k, shape, f32) * scale

    def gain(k, shape):
        return 1.0 + 0.01 * jax.random.normal(k, shape, f32)

    x = nrm(ks[0], (BATCH, SEQ, D_MODEL), 1.0)
    p = nrm(ks[1], (DEPTH, BATCH, SEQ, PLE_DIM), 1.0)
    positions = (jax.random.randint(ks[2], (BATCH, 1), 0, MAX_OFFSET, dtype=jnp.int32)
                 + jnp.arange(SEQ, dtype=jnp.int32)[None, :])
    return {
        'x': x,
        'p': p,
        'positions': positions,
        'mix_norm_g': gain(ks[3], (DEPTH, D_MODEL)),
        'even_w_in': nrm(ks[4], (N_EVEN, D_MODEL, EVEN_IN_WIDTH), D_MODEL ** -0.5),
        'mla_q_norm_g': gain(ks[5], (N_EVEN, Q_LORA)),
        'mla_w_q_up': nrm(ks[6], (N_EVEN, Q_LORA, MLA_HEADS * (MLA_NOPE + MLA_ROPE)), Q_LORA ** -0.5),
        'mla_kv_norm_g': gain(ks[7], (N_EVEN, KV_LORA)),
        'mla_w_kv_up': nrm(ks[8], (N_EVEN, KV_LORA, MLA_HEADS * (MLA_NOPE + MLA_VDIM)), KV_LORA ** -0.5),
        'sgu_ln_g': gain(ks[9], (N_EVEN, SGU_WIDTH)),
        'sgu_ln_b': nrm(ks[10], (N_EVEN, SGU_WIDTH), 0.01),
        'sgu_w_s': nrm(ks[11], (N_EVEN, SGU_GROUPS, SGU_BLOCK, SGU_BLOCK), SGU_BLOCK ** -0.5),
        'sgu_b_s': gain(ks[12], (N_EVEN, SGU_GROUPS, SGU_BLOCK)),
        'even_w_out': nrm(ks[13], (N_EVEN, EVEN_OUT_WIDTH, D_MODEL), EVEN_OUT_WIDTH ** -0.5),
        'ret_w_in': nrm(ks[14], (N_ODD, D_MODEL, RET_IN_WIDTH), D_MODEL ** -0.5),
        'ret_gn_g': gain(ks[15], (N_ODD, RET_V_WIDTH)),
        'ret_gn_b': nrm(ks[16], (N_ODD, RET_V_WIDTH), 0.01),
        'ret_w_out': nrm(ks[17], (N_ODD, RET_V_WIDTH, D_MODEL), RET_V_WIDTH ** -0.5),
        'ffn_norm_g': gain(ks[18], (DEPTH, D_MODEL)),
        'ffn_w_up': nrm(ks[19], (DEPTH, D_MODEL, 2 * D_FF), D_MODEL ** -0.5),
        'ffn_conv_w': nrm(ks[20], (DEPTH, CONV_WIDTH, 2 * D_FF), CONV_WIDTH ** -0.5),
        'ffn_conv_b': nrm(ks[21], (DEPTH, 2 * D_FF), 0.01),
        'ffn_w_down': nrm(ks[22], (DEPTH, D_FF, D_MODEL), D_FF ** -0.5),
        'ple_norm_g': gain(ks[23], (DEPTH, D_MODEL)),
        'ple_w_gate': nrm(ks[24], (DEPTH, D_MODEL, D_MODEL), D_MODEL ** -0.5),
        'ple_w_up': nrm(ks[25], (DEPTH, PLE_DIM, D_MODEL), PLE_DIM ** -0.5),
        'final_norm_g': gain(ks[26], (D_MODEL,)),
    }


def reference(x, p, positions, mix_norm_g, even_w_in, mla_q_norm_g, mla_w_q_up,
              mla_kv_norm_g, mla_w_kv_up, sgu_ln_g, sgu_ln_b, sgu_w_s, sgu_b_s,
              even_w_out, ret_w_in, ret_gn_g, ret_gn_b, ret_w_out, ffn_norm_g,
              ffn_w_up, ffn_conv_w, ffn_conv_b, ffn_w_down, ple_norm_g, ple_w_gate,
              ple_w_up, final_norm_g):
    B, S, _ = x.shape
    h = x
    for i in range(DEPTH):
        hn = _rms_norm(h, mix_norm_g[i])
        if i % 2 == 0:
            j = i // 2
            z = hn @ even_w_in[j]
            c_q, c_kv, k_pe, u, v = jnp.split(z, EVEN_SPLITS, axis=-1)
            a_out = _mla(c_q, c_kv, k_pe, positions, mla_q_norm_g[j], mla_w_q_up[j],
                         mla_kv_norm_g[j], mla_w_kv_up[j])
            b_out = _sgu(u, v, sgu_ln_g[j], sgu_ln_b[j], sgu_w_s[j], sgu_b_s[j])
            h = h + jnp.concatenate([a_out, b_out], axis=-1) @ even_w_out[j]
        else:
            j = i // 2
            z = hn @ ret_w_in[j]
            q, k, v, g = jnp.split(z, RET_SPLITS, axis=-1)
            r = _retention(q.reshape(B, S, RET_HEADS, RET_DK),
                           k.reshape(B, S, RET_HEADS, RET_DK),
                           v.reshape(B, S, RET_HEADS, RET_DV), positions)
            r = _layer_norm(r, ret_gn_g[j].reshape(RET_HEADS, RET_DV),
                            ret_gn_b[j].reshape(RET_HEADS, RET_DV))
            r = r.reshape(B, S, RET_V_WIDTH).astype(h.dtype) * jax.nn.silu(g)
            h = h + r @ ret_w_out[j]
        h = h + _conv_ffn(_rms_norm(h, ffn_norm_g[i]), ffn_w_up[i], ffn_conv_w[i],
                          ffn_conv_b[i], ffn_w_down[i])
        gate = jax.nn.sigmoid(_rms_norm(h, ple_norm_g[i]) @ ple_w_gate[i])
        h = h + (p[i] @ ple_w_up[i]) * gate
    return _rms_norm(h, final_norm_g)
```

```python
import functools
import math

import jax
import jax.numpy as jnp
from jax import lax
from jax.experimental import pallas as pl
from jax.experimental.pallas import tpu as pltpu

F32 = jnp.float32
BF16 = jnp.bfloat16

D_MODEL = 2048
CHUNK = 64
EPS = 1e-6
ROPE_THETA = 10000.0
MLA_HEADS = 8
MLA_NOPE = 128
MLA_ROPE = 64
MLA_VDIM = 128
Q_LORA = 512
KV_LORA = 256
MLA_OUT = MLA_HEADS * MLA_VDIM
MLA_QK_PAD = 256
SGU_BLOCK = 128
SGU_GROUPS = 8
SGU_CH = 128
SGU_WIDTH = SGU_GROUPS * SGU_CH
EVEN_IN_PAD = 3072
RET_HEADS = 8
RET_DK = 256
RET_DV = 512
RET_QK_WIDTH = RET_HEADS * RET_DK
RET_V_WIDTH = RET_HEADS * RET_DV
RET_IN_WIDTH = 2 * RET_QK_WIDTH + 2 * RET_V_WIDTH
RET_SUPER = 256
D_FF = 5632
CONV_WIDTH = 3
PLE_DIM = 256

V7X_LANES = 128
V7X_SUBLANES = 8
V7X_VMEM_BUDGET = 58 * 1024 * 1024


def _nbytes(shape, dtype):
    return math.prod(shape) * jnp.dtype(dtype).itemsize


def _params(semantics, *buffers):
    need = sum(_nbytes(s, d) * n for s, d, n in buffers)
    return pltpu.CompilerParams(dimension_semantics=semantics,
                                vmem_limit_bytes=min(V7X_VMEM_BUDGET, need + (4 << 20)))


def _tile(n, pref, mult=V7X_SUBLANES):
    t = min(n, pref)
    while n % t or t % mult:
        t -= 1
    return t


def _const_spec(shape, index_map):
    return pl.BlockSpec(shape, index_map, pipeline_mode=pl.Buffered(1))


def _rmsnorm_rows(x, g):
    ms = jnp.mean(x * x, axis=-1, keepdims=True)
    return x * lax.rsqrt(ms + EPS) * g


def _rope_tables_kernel(pos_ref, fr_ref, fm_ref, cos_ref, sin_ref, m_ref):
    pos = pos_ref[...].astype(F32)
    ang = pos * fr_ref[...]
    cos_ref[...] = jnp.cos(ang)
    sin_ref[...] = jnp.sin(ang)
    angm = pos * fm_ref[...]
    lane = lax.broadcasted_iota(jnp.int32, angm.shape, 1)
    m_ref[...] = jnp.where(lane < MLA_ROPE, jnp.cos(angm), jnp.sin(angm))


def _rope_tables(positions):
    T = positions.size
    tm = _tile(T, 1024)
    half_r = RET_DK // 2
    half_m = MLA_ROPE // 2
    fr = (ROPE_THETA ** (-jnp.arange(half_r, dtype=F32) / half_r))[None, :]
    fm1 = ROPE_THETA ** (-jnp.arange(half_m, dtype=F32) / half_m)
    fm = jnp.tile(fm1, 4)[None, :]
    tab = jax.ShapeDtypeStruct((T, V7X_LANES), F32)
    row = pl.BlockSpec((tm, V7X_LANES), lambda i: (i, 0))
    frq = pl.BlockSpec((1, V7X_LANES), lambda i: (0, 0))
    return pl.pallas_call(
        _rope_tables_kernel,
        grid=(T // tm,),
        in_specs=[pl.BlockSpec((tm, 1), lambda i: (i, 0)), frq, frq],
        out_specs=[row, row, row],
        out_shape=[tab, tab, tab],
        compiler_params=_params(("parallel",), ((tm, V7X_LANES), F32, 16)),
        name="rope_tables",
    )(positions.reshape(T, 1), fr, fm)


def _init_norm_kernel(x_ref, g_ref, o_ref):
    o_ref[...] = _rmsnorm_rows(x_ref[...], g_ref[...]).astype(o_ref.dtype)


def _init_norm(x, g):
    T, D = x.shape
    tm = _tile(T, 512)
    return pl.pallas_call(
        _init_norm_kernel,
        grid=(T // tm,),
        in_specs=[pl.BlockSpec((tm, D), lambda i: (i, 0)), pl.BlockSpec((1, D), lambda i: (0, 0))],
        out_specs=pl.BlockSpec((tm, D), lambda i: (i, 0)),
        out_shape=jax.ShapeDtypeStruct((T, D), BF16),
        compiler_params=_params(("parallel",), ((tm, D), F32, 4), ((tm, D), BF16, 2)),
        name="init_norm",
    )(x, g.reshape(1, D))


def _mm_kernel(x_ref, w_ref, o_ref):
    o_ref[...] = jnp.dot(x_ref[...], w_ref[...], preferred_element_type=F32).astype(o_ref.dtype)


def _matmul(x, w, tn):
    T, K = x.shape
    N = w.shape[1]
    tm = _tile(T, 1024)
    return pl.pallas_call(
        _mm_kernel,
        grid=(T // tm, N // tn),
        in_specs=[pl.BlockSpec((tm, K), lambda i, j: (i, 0)), pl.BlockSpec((K, tn), lambda i, j: (0, j))],
        out_specs=pl.BlockSpec((tm, tn), lambda i, j: (i, j)),
        out_shape=jax.ShapeDtypeStruct((T, N), BF16),
        compiler_params=_params(("parallel", "arbitrary"), ((tm, K), BF16, 2), ((K, tn), BF16, 2),
                                ((tm, tn), BF16, 2), ((tm, tn), F32, 1)),
        name="proj",
    )(x, w)


def _mla_prep_kernel(cq_ref, ckv_ref, kpe_ref, m_ref, qg_ref, kvg_ref, wq_ref, wkv_ref,
                     q_ref, k_ref, v_ref):
    scale = (MLA_NOPE + MLA_ROPE) ** -0.5
    m = m_ref[...]
    cqn = _rmsnorm_rows(cq_ref[...].astype(F32), qg_ref[...]).astype(BF16)
    q = jnp.dot(cqn, wq_ref[...], preferred_element_type=F32)
    for h in range(MLA_HEADS):
        c0 = h * MLA_QK_PAD
        q_ref[:, c0:c0 + MLA_NOPE] = (q[:, c0:c0 + MLA_NOPE] * scale).astype(BF16)
        y = q[:, c0 + MLA_NOPE:c0 + MLA_QK_PAD] * m
        y = y + pltpu.roll(y, MLA_ROPE, axis=1)
        q_ref[:, c0 + MLA_NOPE:c0 + MLA_QK_PAD] = (y * scale).astype(BF16)
    ckvn = _rmsnorm_rows(ckv_ref[...].astype(F32), kvg_ref[...]).astype(BF16)
    kv = jnp.dot(ckvn, wkv_ref[...], preferred_element_type=F32)
    yk = kpe_ref[...].astype(F32) * m
    yk = yk + pltpu.roll(yk, MLA_ROPE, axis=1)
    lane = lax.broadcasted_iota(jnp.int32, yk.shape, 1)
    k_rot = jnp.where(lane < MLA_ROPE, yk, 0.0).astype(BF16)
    for h in range(MLA_HEADS):
        c0 = h * MLA_QK_PAD
        k_ref[:, c0:c0 + MLA_NOPE] = kv[:, h * MLA_NOPE:(h + 1) * MLA_NOPE].astype(BF16)
        k_ref[:, c0 + MLA_NOPE:c0 + MLA_QK_PAD] = k_rot
    v_ref[...] = kv[:, MLA_HEADS * MLA_NOPE:].astype(BF16)


def _mla_prep(z, m_tab, q_norm_g, kv_norm_g, wq, wkv):
    T = z.shape[0]
    tm = _tile(T, 512)
    qkw = MLA_HEADS * MLA_QK_PAD
    u_v = 2 * SGU_WIDTH
    return pl.pallas_call(
        _mla_prep_kernel,
        grid=(T // tm,),
        in_specs=[
            pl.BlockSpec((tm, Q_LORA), lambda i: (i, u_v // Q_LORA)),
            pl.BlockSpec((tm, KV_LORA), lambda i: (i, (u_v + Q_LORA) // KV_LORA)),
            pl.BlockSpec((tm, 2 * MLA_ROPE), lambda i: (i, (u_v + Q_LORA + KV_LORA) // (2 * MLA_ROPE))),
            pl.BlockSpec((tm, V7X_LANES), lambda i: (i, 0)),
            pl.BlockSpec((1, Q_LORA), lambda i: (0, 0)),
            pl.BlockSpec((1, KV_LORA), lambda i: (0, 0)),
            _const_spec((Q_LORA, qkw), lambda i: (0, 0)),
            _const_spec((KV_LORA, qkw), lambda i: (0, 0)),
        ],
        out_specs=[pl.BlockSpec((tm, qkw), lambda i: (i, 0)),
                   pl.BlockSpec((tm, qkw), lambda i: (i, 0)),
                   pl.BlockSpec((tm, MLA_OUT), lambda i: (i, 0))],
        out_shape=[jax.ShapeDtypeStruct((T, qkw), BF16), jax.ShapeDtypeStruct((T, qkw), BF16),
                   jax.ShapeDtypeStruct((T, MLA_OUT), BF16)],
        compiler_params=_params(("parallel",), ((tm, qkw), BF16, 8), ((tm, qkw), F32, 3),
                                ((Q_LORA + KV_LORA, qkw), BF16, 1)),
        name="mla_prep",
    )(z, z, z, m_tab, q_norm_g.reshape(1, -1), kv_norm_g.reshape(1, -1), wq, wkv)


def _mla_attn_kernel(q_ref, k_ref, v_ref, o_ref, *, tq):
    S = q_ref.shape[0]
    r = lax.broadcasted_iota(jnp.int32, (tq, tq), 0) // CHUNK
    c = lax.broadcasted_iota(jnp.int32, (tq, tq), 1) // CHUNK
    bias = jnp.where(c <= r, 0.0, -1e30).astype(F32)
    for qi in range(S // tq):
        past = qi * tq
        q = q_ref[past:past + tq, :]
        s = lax.dot_general(q, k_ref[0:past + tq, :], (((1,), (1,)), ((), ())),
                            preferred_element_type=F32)
        s_diag = s[:, past:] + bias
        s = s_diag if qi == 0 else jnp.concatenate([s[:, :past], s_diag], axis=1)
        mx = jnp.max(s, axis=-1, keepdims=True)
        p = jnp.exp(s - mx)
        l = jnp.sum(p, axis=-1, keepdims=True)
        o = jnp.dot(p.astype(BF16), v_ref[0:past + tq, :], preferred_element_type=F32)
        o_ref[past:past + tq, :] = (o / l).astype(o_ref.dtype)


def _mla_attn(q, k, v, B, S):
    tq = _tile(S, 256, CHUNK)
    q3 = q.reshape(B, S, -1)
    k3 = k.reshape(B, S, -1)
    v3 = v.reshape(B, S, -1)
    out = pl.pallas_call(
        functools.partial(_mla_attn_kernel, tq=tq),
        grid=(B, MLA_HEADS),
        in_specs=[pl.BlockSpec((None, S, MLA_QK_PAD), lambda b, h: (b, 0, h)),
                  pl.BlockSpec((None, S, MLA_QK_PAD), lambda b, h: (b, 0, h)),
                  pl.BlockSpec((None, S, MLA_VDIM), lambda b, h: (b, 0, h))],
        out_specs=pl.BlockSpec((None, S, MLA_VDIM), lambda b, h: (b, 0, h)),
        out_shape=jax.ShapeDtypeStruct((B, S, MLA_OUT), BF16),
        compiler_params=_params(("parallel", "parallel"), ((S, MLA_QK_PAD), BF16, 4),
                                ((S, MLA_VDIM), BF16, 4), ((tq, S), F32, 4)),
        name="mla_attn",
    )(q3, k3, v3)
    return out.reshape(B * S, MLA_OUT)


def _even_out_kernel(a_ref, u_ref, v_ref, h_ref, lng_ref, lnb_ref, ws_ref, bs_ref, wo_ref, g_ref,
                     o_ref, on_ref, bo_ref):
    tm = a_ref.shape[0]
    gu = jax.nn.gelu(u_ref[...].astype(F32))
    gv = jax.nn.gelu(v_ref[...].astype(F32))
    mu = jnp.mean(gv, axis=-1, keepdims=True)
    xc = gv - mu
    y = xc * lax.rsqrt(jnp.mean(xc * xc, axis=-1, keepdims=True) + EPS)
    vn = (y * lng_ref[...] + lnb_ref[...]).astype(BF16)
    pr = lax.broadcasted_iota(jnp.int32, (SGU_BLOCK, SGU_BLOCK), 0) // CHUNK
    pc = lax.broadcasted_iota(jnp.int32, (SGU_BLOCK, SGU_BLOCK), 1) // CHUNK
    for g in range(SGU_GROUPS):
        w = jnp.where(pr >= pc, ws_ref[g], 0.0).astype(BF16)
        bcol = bs_ref[:, g:g + 1]
        cols = slice(g * SGU_CH, (g + 1) * SGU_CH)
        for nb in range(tm // SGU_BLOCK):
            rows = slice(nb * SGU_BLOCK, (nb + 1) * SGU_BLOCK)
            s = jnp.dot(w, vn[rows, cols], preferred_element_type=F32) + bcol
            bo_ref[rows, cols] = (gu[rows, cols] * s).astype(BF16)
    acc = jnp.dot(a_ref[...], wo_ref[0:MLA_OUT, :], preferred_element_type=F32)
    acc = acc + jnp.dot(bo_ref[...], wo_ref[MLA_OUT:, :], preferred_element_type=F32)
    hnew = h_ref[...] + acc
    o_ref[...] = hnew
    on_ref[...] = _rmsnorm_rows(hnew, g_ref[...]).astype(on_ref.dtype)


def _even_out(a, z, h, ln_g, ln_b, w_s, b_s, wo, g_next):
    T, D = h.shape
    tm = _tile(T, 512, SGU_BLOCK)
    row = lambda i: (i, 0)
    fix = lambda i: (0, 0)
    return pl.pallas_call(
        _even_out_kernel,
        grid=(T // tm,),
        in_specs=[pl.BlockSpec((tm, MLA_OUT), row),
                  pl.BlockSpec((tm, SGU_WIDTH), lambda i: (i, 0)),
                  pl.BlockSpec((tm, SGU_WIDTH), lambda i: (i, 1)),
                  pl.BlockSpec((tm, D), row),
                  pl.BlockSpec((1, SGU_WIDTH), fix),
                  pl.BlockSpec((1, SGU_WIDTH), fix),
                  _const_spec((SGU_GROUPS, SGU_BLOCK, SGU_BLOCK), lambda i: (0, 0, 0)),
                  pl.BlockSpec((SGU_BLOCK, SGU_GROUPS), fix),
                  _const_spec((MLA_OUT + SGU_WIDTH, D), fix),
                  pl.BlockSpec((1, D), fix)],
        out_specs=[pl.BlockSpec((tm, D), row), pl.BlockSpec((tm, D), row)],
        out_shape=[jax.ShapeDtypeStruct((T, D), F32), jax.ShapeDtypeStruct((T, D), BF16)],
        scratch_shapes=[pltpu.VMEM((tm, SGU_WIDTH), BF16)],
        compiler_params=_params(("parallel",), ((tm, D), F32, 6), ((tm, D), BF16, 6),
                                ((MLA_OUT + SGU_WIDTH, D), BF16, 1), ((tm, SGU_WIDTH), F32, 4)),
        name="even_out",
    )(a, z, z, h, ln_g.reshape(1, -1), ln_b.reshape(1, -1), w_s, b_s.T, wo, g_next.reshape(1, D))


def _retention_kernel(q_ref, k_ref, v_ref, gate_ref, cos_ref, sin_ref, dm_ref, qd_ref, kd_ref, sd_ref,
                      gng_ref, gnb_ref, o_ref, state_ref, *, L):
    S = q_ref.shape[0]
    half = RET_DK // 2
    state_ref[...] = jnp.zeros_like(state_ref)
    dm = dm_ref[...]
    qd = qd_ref[...]
    kd = kd_ref[...]
    sd = sd_ref[...]
    gng = gng_ref[...]
    gnb = gnb_ref[...]

    def rope(x, cs, sn):
        x1, x2 = x[:, :half], x[:, half:]
        return jnp.concatenate([x1 * cs - x2 * sn, x2 * cs + x1 * sn], axis=1)

    def step(c, carry):
        rows = pl.ds(pl.multiple_of(c * L, L), L)
        cs = cos_ref[rows, :]
        sn = sin_ref[rows, :]
        qr = rope(q_ref[rows, :].astype(F32), cs, sn)
        kr = rope(k_ref[rows, :].astype(F32), cs, sn) * (RET_DK ** -0.5)
        qb = qr.astype(BF16)
        vb = v_ref[rows, :]
        sc = lax.dot_general(qb, kr.astype(BF16), (((1,), (1,)), ((), ())),
                             preferred_element_type=F32) * dm
        intra = jnp.dot(sc.astype(BF16), vb, preferred_element_type=F32)
        st = state_ref[...]
        cross = jnp.dot(qb, st.astype(BF16), preferred_element_type=F32) * qd
        kz = (kr * kd).astype(BF16)
        state_ref[...] = st * sd + lax.dot_general(kz, vb, (((0,), (0,)), ((), ())),
                                                   preferred_element_type=F32)
        out = intra + cross
        mu = jnp.mean(out, axis=-1, keepdims=True)
        xc = out - mu
        y = xc * lax.rsqrt(jnp.mean(xc * xc, axis=-1, keepdims=True) + EPS)
        y = y * gng + gnb
        o_ref[rows, :] = (y * jax.nn.silu(gate_ref[rows, :].astype(F32))).astype(o_ref.dtype)
        return carry

    lax.fori_loop(0, S // L, step, 0)


def _retention_tables(L):
    log_g = jnp.log1p(-(2.0 ** (-5.0 - jnp.arange(RET_HEADS, dtype=F32))))
    idx = jnp.arange(L, dtype=F32)
    chunk = jnp.arange(L) // CHUNK
    decay = jnp.exp(log_g[:, None, None] * jnp.abs(idx[:, None] - idx[None, :]))
    dm = jnp.where(chunk[None, :, None] >= chunk[None, None, :], decay, 0.0)
    qd = jnp.exp(log_g[:, None] * (idx + 1.0)[None, :])[:, :, None]
    kd = jnp.exp(log_g[:, None] * (L - 1 - idx)[None, :])[:, :, None]
    sd = jnp.broadcast_to(jnp.exp(log_g * L)[:, None, None], (RET_HEADS, 1, RET_DV))
    return dm, qd, kd, sd


def _retention(z, cos_t, sin_t, gn_g, gn_b, B, S):
    L = _tile(S, RET_SUPER, CHUNK)
    z3 = z.reshape(B, S, RET_IN_WIDTH)
    cos3 = cos_t.reshape(B, S, -1)
    sin3 = sin_t.reshape(B, S, -1)
    dm, qd, kd, sd = _retention_tables(L)
    qk_blocks = RET_QK_WIDTH // RET_DK
    v_first = 2 * RET_QK_WIDTH // RET_DV
    g_first = v_first + RET_V_WIDTH // RET_DV
    out = pl.pallas_call(
        functools.partial(_retention_kernel, L=L),
        grid=(B, RET_HEADS),
        in_specs=[pl.BlockSpec((None, S, RET_DK), lambda b, h: (b, 0, h)),
                  pl.BlockSpec((None, S, RET_DK), lambda b, h: (b, 0, qk_blocks + h)),
                  pl.BlockSpec((None, S, RET_DV), lambda b, h: (b, 0, v_first + h)),
                  pl.BlockSpec((None, S, RET_DV), lambda b, h: (b, 0, g_first + h)),
                  pl.BlockSpec((None, S, RET_DK // 2), lambda b, h: (b, 0, 0)),
                  pl.BlockSpec((None, S, RET_DK // 2), lambda b, h: (b, 0, 0)),
                  pl.BlockSpec((None, L, L), lambda b, h: (h, 0, 0)),
                  pl.BlockSpec((None, L, 1), lambda b, h: (h, 0, 0)),
                  pl.BlockSpec((None, L, 1), lambda b, h: (h, 0, 0)),
                  pl.BlockSpec((None, 1, RET_DV), lambda b, h: (h, 0, 0)),
                  pl.BlockSpec((1, RET_DV), lambda b, h: (0, h)),
                  pl.BlockSpec((1, RET_DV), lambda b, h: (0, h))],
        out_specs=pl.BlockSpec((None, S, RET_DV), lambda b, h: (b, 0, h)),
        out_shape=jax.ShapeDtypeStruct((B, S, RET_V_WIDTH), BF16),
        scratch_shapes=[pltpu.VMEM((RET_DK, RET_DV), F32)],
        compiler_params=_params(("parallel", "arbitrary"), ((S, RET_DK), BF16, 4), ((S, RET_DV), BF16, 6),
                                ((S, RET_DK // 2), F32, 4), ((L, L), F32, 4), ((L, V7X_LANES), F32, 4),
                                ((RET_DK, RET_DV), F32, 4), ((L, RET_DV), F32, 6)),
        name="retention",
    )(z3, z3, z3, z3, cos3, sin3, dm, qd, kd, sd, gn_g.reshape(1, -1), gn_b.reshape(1, -1))
    return out.reshape(B * S, RET_V_WIDTH)


def _res_mm_kernel(x_ref, w_ref, h_ref, g_ref, o_ref, on_ref, *, nk):
    k = pl.program_id(1)
    d = jnp.dot(x_ref[...], w_ref[...], preferred_element_type=F32)

    @pl.when(k == 0)
    def _():
        o_ref[...] = h_ref[...] + d

    @pl.when(k > 0)
    def _():
        o_ref[...] += d

    @pl.when(k == nk - 1)
    def _():
        on_ref[...] = _rmsnorm_rows(o_ref[...], g_ref[...]).astype(on_ref.dtype)


def _res_matmul(x, w, h, g_next, tk):
    T, K = x.shape
    D = h.shape[1]
    tm = _tile(T, 512)
    nk = K // tk
    return pl.pallas_call(
        functools.partial(_res_mm_kernel, nk=nk),
        grid=(T // tm, nk),
        in_specs=[pl.BlockSpec((tm, tk), lambda i, k: (i, k)),
                  pl.BlockSpec((tk, D), lambda i, k: (k, 0)),
                  pl.BlockSpec((tm, D), lambda i, k: (i, 0)),
                  pl.BlockSpec((1, D), lambda i, k: (0, 0))],
        out_specs=[pl.BlockSpec((tm, D), lambda i, k: (i, 0)), pl.BlockSpec((tm, D), lambda i, k: (i, 0))],
        out_shape=[jax.ShapeDtypeStruct((T, D), F32), jax.ShapeDtypeStruct((T, D), BF16)],
        compiler_params=_params(("parallel", "arbitrary"), ((tm, tk), BF16, 2), ((tk, D), BF16, 2),
                                ((tm, D), F32, 5), ((tm, D), BF16, 2)),
        name="res_proj",
    )(x, w, h, g_next.reshape(1, D))


def _ffn_up_kernel(x_ref, wg_ref, wv_ref, cwg_ref, cwv_ref, cbg_ref, cbv_ref, o_ref, eg_ref, ev_ref,
                   *, tiles_per_seq):
    tm = x_ref.shape[0]
    halo = V7X_SUBLANES
    first = (pl.program_id(1) % tiles_per_seq) == 0
    x = x_ref[...]

    def conv(w_ref, cw_ref, cb_ref, e_ref):
        @pl.when(first)
        def _():
            e_ref[0:halo, :] = jnp.zeros((halo, e_ref.shape[1]), F32)

        @pl.when(jnp.logical_not(first))
        def _():
            e_ref[0:halo, :] = e_ref[tm:tm + halo, :]

        e_ref[halo:halo + tm, :] = jnp.dot(x, w_ref[...], preferred_element_type=F32)
        c = cb_ref[...]
        for j in range(CONV_WIDTH):
            off = halo - (CONV_WIDTH - 1) + j
            c = c + e_ref[off:off + tm, :] * cw_ref[j:j + 1, :]
        return c

    gate = conv(wg_ref, cwg_ref, cbg_ref, eg_ref)
    val = conv(wv_ref, cwv_ref, cbv_ref, ev_ref)
    o_ref[...] = (jax.nn.gelu(gate) * val).astype(o_ref.dtype)


def _ffn_up(hn, w_up, conv_w, conv_b, S):
    T, D = hn.shape
    tn = 512
    tm = _tile(S, 1024)
    nj = D_FF // tn
    halo = V7X_SUBLANES
    return pl.pallas_call(
        functools.partial(_ffn_up_kernel, tiles_per_seq=S // tm),
        grid=(nj, T // tm),
        in_specs=[pl.BlockSpec((tm, D), lambda j, i: (i, 0)),
                  pl.BlockSpec((D, tn), lambda j, i: (0, j)),
                  pl.BlockSpec((D, tn), lambda j, i: (0, nj + j)),
                  pl.BlockSpec((CONV_WIDTH, tn), lambda j, i: (0, j)),
                  pl.BlockSpec((CONV_WIDTH, tn), lambda j, i: (0, nj + j)),
                  pl.BlockSpec((1, tn), lambda j, i: (0, j)),
                  pl.BlockSpec((1, tn), lambda j, i: (0, nj + j))],
        out_specs=pl.BlockSpec((tm, tn), lambda j, i: (i, j)),
        out_shape=jax.ShapeDtypeStruct((T, D_FF), BF16),
        scratch_shapes=[pltpu.VMEM((tm + halo, tn), F32), pltpu.VMEM((tm + halo, tn), F32)],
        compiler_params=_params(("parallel", "arbitrary"), ((tm, D), BF16, 2), ((D, tn), BF16, 4),
                                ((tm + halo, tn), F32, 2), ((tm, tn), F32, 4), ((tm, tn), BF16, 2)),
        name="ffn_up",
    )(hn, w_up, w_up, conv_w, conv_w, conv_b.reshape(1, -1), conv_b.reshape(1, -1))


def _ple_kernel(hn_ref, wg_ref, p_ref, wu_ref, h_ref, g_ref, o_ref, on_ref):
    gate = jax.nn.sigmoid(jnp.dot(hn_ref[...], wg_ref[...], preferred_element_type=F32))
    up = jnp.dot(p_ref[...].astype(BF16), wu_ref[...], preferred_element_type=F32)
    hnew = h_ref[...] + up * gate
    o_ref[...] = hnew
    on_ref[...] = _rmsnorm_rows(hnew, g_ref[...]).astype(on_ref.dtype)


def _ple(hn, wg, p_all, layer, wu, h, g_next, out_dtype):
    T, D = h.shape
    tm = _tile(T, 512)
    row = lambda i: (i, 0)
    fix = lambda i: (0, 0)
    return pl.pallas_call(
        _ple_kernel,
        grid=(T // tm,),
        in_specs=[pl.BlockSpec((tm, D), row),
                  _const_spec((D, D), fix),
                  pl.BlockSpec((None, tm, PLE_DIM), lambda i: (layer, i, 0)),
                  _const_spec((PLE_DIM, D), fix),
                  pl.BlockSpec((tm, D), row),
                  pl.BlockSpec((1, D), fix)],
        out_specs=[pl.BlockSpec((tm, D), row), pl.BlockSpec((tm, D), row)],
        out_shape=[jax.ShapeDtypeStruct((T, D), F32), jax.ShapeDtypeStruct((T, D), out_dtype)],
        compiler_params=_params(("parallel",), ((tm, D), BF16, 2), ((D, D), BF16, 1), ((PLE_DIM, D), BF16, 1),
                                ((tm, PLE_DIM), F32, 2), ((tm, D), F32, 8)),
        name="ple",
    )(hn, wg, p_all, wu, h, g_next.reshape(1, D))


def _rot_cols(w):
    half = MLA_ROPE // 2
    return jnp.concatenate([-w[..., half:], w[..., :half]], axis=-1)


def _even_in_weight(w):
    cq = w[:, :Q_LORA]
    ckv = w[:, Q_LORA:Q_LORA + KV_LORA]
    kpe = w[:, Q_LORA + KV_LORA:Q_LORA + KV_LORA + MLA_ROPE]
    uv = w[:, Q_LORA + KV_LORA + MLA_ROPE:]
    used = 2 * SGU_WIDTH + Q_LORA + KV_LORA + 2 * MLA_ROPE
    pad = jnp.zeros((w.shape[0], EVEN_IN_PAD - used), w.dtype)
    return jnp.concatenate([uv, cq, ckv, kpe, _rot_cols(kpe), pad], axis=1).astype(BF16)


def _q_up_weight(w):
    w = w.reshape(Q_LORA, MLA_HEADS, MLA_NOPE + MLA_ROPE)
    rope = w[..., MLA_NOPE:]
    w = jnp.concatenate([w[..., :MLA_NOPE], rope, _rot_cols(rope)], axis=-1)
    return w.reshape(Q_LORA, MLA_HEADS * MLA_QK_PAD).astype(BF16)


def _kv_up_weight(w):
    w = w.reshape(KV_LORA, MLA_HEADS, MLA_NOPE + MLA_VDIM)
    k = w[..., :MLA_NOPE].reshape(KV_LORA, -1)
    v = w[..., MLA_NOPE:].reshape(KV_LORA, -1)
    return jnp.concatenate([k, v], axis=1).astype(BF16)


def kernel(x, p, positions, mix_norm_g, even_w_in, mla_q_norm_g, mla_w_q_up, mla_kv_norm_g, mla_w_kv_up,
           sgu_ln_g, sgu_ln_b, sgu_w_s, sgu_b_s, even_w_out, ret_w_in, ret_gn_g, ret_gn_b, ret_w_out,
           ffn_norm_g, ffn_w_up, ffn_conv_w, ffn_conv_b, ffn_w_down, ple_norm_g, ple_w_gate, ple_w_up,
           final_norm_g):
    B, S, D = x.shape
    depth = p.shape[0]
    T = B * S
    assert D == D_MODEL and S % SGU_BLOCK == 0
    cos_t, sin_t, m_tab = _rope_tables(positions)
    h = x.reshape(T, D)
    p_all = p.reshape(depth, T, PLE_DIM)
    hn = _init_norm(h, mix_norm_g[0])
    for i in range(depth):
        j = i // 2
        if i % 2 == 0:
            z = _matmul(hn, _even_in_weight(even_w_in[j]), 1024)
            q, k, v = _mla_prep(z, m_tab, mla_q_norm_g[j], mla_kv_norm_g[j],
                                _q_up_weight(mla_w_q_up[j]), _kv_up_weight(mla_w_kv_up[j]))
            a = _mla_attn(q, k, v, B, S)
            h, hn = _even_out(a, z, h, sgu_ln_g[j], sgu_ln_b[j], sgu_w_s[j], sgu_b_s[j],
                              even_w_out[j].astype(BF16), ffn_norm_g[i])
        else:
            z = _matmul(hn, ret_w_in[j].astype(BF16), 1024)
            r = _retention(z, cos_t, sin_t, ret_gn_g[j], ret_gn_b[j], B, S)
            h, hn = _res_matmul(r, ret_w_out[j].astype(BF16), h, ffn_norm_g[i], 1024)
        act = _ffn_up(hn, ffn_w_up[i].astype(BF16), ffn_conv_w[i], ffn_conv_b[i], S)
        h, hn = _res_matmul(act, ffn_w_down[i].astype(BF16), h, ple_norm_g[i], D_FF // 4)
        last = i == depth - 1
        g_next = final_norm_g if last else mix_norm_g[i + 1]
        h, hn = _ple(hn, ple_w_gate[i].astype(BF16), p_all, i, ple_w_up[i].astype(BF16), h, g_next,
                     F32 if last else BF16)
    return hn.reshape(B, S, D)
```

```python
import functools
import math

import jax
import jax.numpy as jnp
from jax import lax
from jax.experimental import pallas as pl
from jax.experimental.pallas import tpu as pltpu

F32 = jnp.float32
BF16 = jnp.bfloat16

D_MODEL = 2048
CHUNK = 64
EPS = 1e-6
ROPE_THETA = 10000.0
MLA_HEADS = 8
MLA_NOPE = 128
MLA_ROPE = 64
MLA_VDIM = 128
Q_LORA = 512
KV_LORA = 256
MLA_OUT = MLA_HEADS * MLA_VDIM
MLA_QK_PAD = 256
SGU_BLOCK = 128
SGU_GROUPS = 8
SGU_CH = 128
SGU_WIDTH = SGU_GROUPS * SGU_CH
EVEN_IN_PAD = 3072
RET_HEADS = 8
RET_DK = 256
RET_DV = 512
RET_QK_WIDTH = RET_HEADS * RET_DK
RET_V_WIDTH = RET_HEADS * RET_DV
RET_IN_WIDTH = 2 * RET_QK_WIDTH + 2 * RET_V_WIDTH
RET_SUPER = 256
D_FF = 5632
CONV_WIDTH = 3
PLE_DIM = 256

V7X_LANES = 128
V7X_SUBLANES = 8
V7X_VMEM_BUDGET = 58 * 1024 * 1024


def _nbytes(shape, dtype):
    return math.prod(shape) * jnp.dtype(dtype).itemsize


def _params(semantics, *buffers):
    need = sum(_nbytes(s, d) * n for s, d, n in buffers)
    return pltpu.CompilerParams(dimension_semantics=semantics,
                                vmem_limit_bytes=min(V7X_VMEM_BUDGET, need + (4 << 20)))


def _tile(n, pref, mult=V7X_SUBLANES):
    t = min(n, pref)
    while n % t or t % mult:
        t -= 1
    return t


def _const_spec(shape, index_map):
    return pl.BlockSpec(shape, index_map, pipeline_mode=pl.Buffered(1))


def _rmsnorm_rows(x, g):
    ms = jnp.mean(x * x, axis=-1, keepdims=True)
    return x * lax.rsqrt(ms + EPS) * g


def _rope_tables_kernel(pos_ref, fr_ref, fm_ref, cos_ref, sin_ref, m_ref):
    pos = pos_ref[...].astype(F32)
    ang = pos * fr_ref[...]
    cos_ref[...] = jnp.cos(ang)
    sin_ref[...] = jnp.sin(ang)
    angm = pos * fm_ref[...]
    lane = lax.broadcasted_iota(jnp.int32, angm.shape, 1)
    m_ref[...] = jnp.where(lane < MLA_ROPE, jnp.cos(angm), jnp.sin(angm))


def _rope_tables(positions):
    T = positions.size
    tm = _tile(T, 1024)
    half_r = RET_DK // 2
    half_m = MLA_ROPE // 2
    fr = (ROPE_THETA ** (-jnp.arange(half_r, dtype=F32) / half_r))[None, :]
    fm1 = ROPE_THETA ** (-jnp.arange(half_m, dtype=F32) / half_m)
    fm = jnp.tile(fm1, 4)[None, :]
    tab = jax.ShapeDtypeStruct((T, V7X_LANES), F32)
    row = pl.BlockSpec((tm, V7X_LANES), lambda i: (i, 0))
    frq = pl.BlockSpec((1, V7X_LANES), lambda i: (0, 0))
    return pl.pallas_call(
        _rope_tables_kernel,
        grid=(T // tm,),
        in_specs=[pl.BlockSpec((tm, 1), lambda i: (i, 0)), frq, frq],
        out_specs=[row, row, row],
        out_shape=[tab, tab, tab],
        compiler_params=_params(("parallel",), ((tm, V7X_LANES), F32, 16)),
        name="rope_tables",
    )(positions.reshape(T, 1), fr, fm)


def _init_norm_kernel(x_ref, g_ref, o_ref):
    o_ref[...] = _rmsnorm_rows(x_ref[...], g_ref[...]).astype(o_ref.dtype)


def _init_norm(x, g):
    T, D = x.shape
    tm = _tile(T, 512)
    return pl.pallas_call(
        _init_norm_kernel,
        grid=(T // tm,),
        in_specs=[pl.BlockSpec((tm, D), lambda i: (i, 0)), pl.BlockSpec((1, D), lambda i: (0, 0))],
        out_specs=pl.BlockSpec((tm, D), lambda i: (i, 0)),
        out_shape=jax.ShapeDtypeStruct((T, D), BF16),
        compiler_params=_params(("parallel",), ((tm, D), F32, 4), ((tm, D), BF16, 2)),
        name="init_norm",
    )(x, g.reshape(1, D))


def _mm_kernel(x_ref, w_ref, o_ref):
    o_ref[...] = jnp.dot(x_ref[...], w_ref[...], preferred_element_type=F32).astype(o_ref.dtype)


def _matmul(x, w, tn):
    T, K = x.shape
    N = w.shape[1]
    tm = _tile(T, 1024)
    return pl.pallas_call(
        _mm_kernel,
        grid=(T // tm, N // tn),
        in_specs=[pl.BlockSpec((tm, K), lambda i, j: (i, 0)), pl.BlockSpec((K, tn), lambda i, j: (0, j))],
        out_specs=pl.BlockSpec((tm, tn), lambda i, j: (i, j)),
        out_shape=jax.ShapeDtypeStruct((T, N), BF16),
        compiler_params=_params(("parallel", "arbitrary"), ((tm, K), BF16, 2), ((K, tn), BF16, 2),
                                ((tm, tn), BF16, 2), ((tm, tn), F32, 1)),
        name="proj",
    )(x, w)


def _mla_prep_kernel(cq_ref, ckv_ref, kpe_ref, m_ref, qg_ref, kvg_ref, wq_ref, wkv_ref,
                     q_ref, k_ref, v_ref):
    scale = (MLA_NOPE + MLA_ROPE) ** -0.5
    m = m_ref[...]
    cqn = _rmsnorm_rows(cq_ref[...].astype(F32), qg_ref[...]).astype(BF16)
    q = jnp.dot(cqn, wq_ref[...], preferred_element_type=F32)
    for h in range(MLA_HEADS):
        c0 = h * MLA_QK_PAD
        q_ref[:, c0:c0 + MLA_NOPE] = (q[:, c0:c0 + MLA_NOPE] * scale).astype(BF16)
        y = q[:, c0 + MLA_NOPE:c0 + MLA_QK_PAD] * m
        y = y + pltpu.roll(y, MLA_ROPE, axis=1)
        q_ref[:, c0 + MLA_NOPE:c0 + MLA_QK_PAD] = (y * scale).astype(BF16)
    ckvn = _rmsnorm_rows(ckv_ref[...].astype(F32), kvg_ref[...]).astype(BF16)
    kv = jnp.dot(ckvn, wkv_ref[...], preferred_element_type=F32)
    yk = kpe_ref[...].astype(F32) * m
    yk = yk + pltpu.roll(yk, MLA_ROPE, axis=1)
    lane = lax.broadcasted_iota(jnp.int32, yk.shape, 1)
    k_rot = jnp.where(lane < MLA_ROPE, yk, 0.0).astype(BF16)
    for h in range(MLA_HEADS):
        c0 = h * MLA_QK_PAD
        k_ref[:, c0:c0 + MLA_NOPE] = kv[:, h * MLA_NOPE:(h + 1) * MLA_NOPE].astype(BF16)
        k_ref[:, c0 + MLA_NOPE:c0 + MLA_QK_PAD] = k_rot
    v_ref[...] = kv[:, MLA_HEADS * MLA_NOPE:].astype(BF16)


def _mla_prep(z, m_tab, q_norm_g, kv_norm_g, wq, wkv):
    T = z.shape[0]
    tm = _tile(T, 512)
    qkw = MLA_HEADS * MLA_QK_PAD
    u_v = 2 * SGU_WIDTH
    return pl.pallas_call(
        _mla_prep_kernel,
        grid=(T // tm,),
        in_specs=[
            pl.BlockSpec((tm, Q_LORA), lambda i: (i, u_v // Q_LORA)),
            pl.BlockSpec((tm, KV_LORA), lambda i: (i, (u_v + Q_LORA) // KV_LORA)),
            pl.BlockSpec((tm, 2 * MLA_ROPE), lambda i: (i, (u_v + Q_LORA + KV_LORA) // (2 * MLA_ROPE))),
            pl.BlockSpec((tm, V7X_LANES), lambda i: (i, 0)),
            pl.BlockSpec((1, Q_LORA), lambda i: (0, 0)),
            pl.BlockSpec((1, KV_LORA), lambda i: (0, 0)),
            _const_spec((Q_LORA, qkw), lambda i: (0, 0)),
            _const_spec((KV_LORA, qkw), lambda i: (0, 0)),
        ],
        out_specs=[pl.BlockSpec((tm, qkw), lambda i: (i, 0)),
                   pl.BlockSpec((tm, qkw), lambda i: (i, 0)),
                   pl.BlockSpec((tm, MLA_OUT), lambda i: (i, 0))],
        out_shape=[jax.ShapeDtypeStruct((T, qkw), BF16), jax.ShapeDtypeStruct((T, qkw), BF16),
                   jax.ShapeDtypeStruct((T, MLA_OUT), BF16)],
        compiler_params=_params(("parallel",), ((tm, qkw), BF16, 8), ((tm, qkw), F32, 3),
                                ((Q_LORA + KV_LORA, qkw), BF16, 1)),
        name="mla_prep",
    )(z, z, z, m_tab, q_norm_g.reshape(1, -1), kv_norm_g.reshape(1, -1), wq, wkv)


def _mla_attn_kernel(q_ref, k_ref, v_ref, o_ref, *, tq):
    S = q_ref.shape[0]
    r = lax.broadcasted_iota(jnp.int32, (tq, tq), 0) // CHUNK
    c = lax.broadcasted_iota(jnp.int32, (tq, tq), 1) // CHUNK
    bias = jnp.where(c <= r, 0.0, -1e30).astype(F32)
    for qi in range(S // tq):
        past = qi * tq
        q = q_ref[past:past + tq, :]
        s = lax.dot_general(q, k_ref[0:past + tq, :], (((1,), (1,)), ((), ())),
                            preferred_element_type=F32)
        s_diag = s[:, past:] + bias
        s = s_diag if qi == 0 else jnp.concatenate([s[:, :past], s_diag], axis=1)
        mx = jnp.max(s, axis=-1, keepdims=True)
        p = jnp.exp(s - mx)
        l = jnp.sum(p, axis=-1, keepdims=True)
        o = jnp.dot(p.astype(BF16), v_ref[0:past + tq, :], preferred_element_type=F32)
        o_ref[past:past + tq, :] = (o / l).astype(o_ref.dtype)


def _mla_attn(q, k, v, B, S):
    tq = _tile(S, 256, CHUNK)
    q3 = q.reshape(B, S, -1)
    k3 = k.reshape(B, S, -1)
    v3 = v.reshape(B, S, -1)
    out = pl.pallas_call(
        functools.partial(_mla_attn_kernel, tq=tq),
        grid=(B, MLA_HEADS),
        in_specs=[pl.BlockSpec((None, S, MLA_QK_PAD), lambda b, h: (b, 0, h)),
                  pl.BlockSpec((None, S, MLA_QK_PAD), lambda b, h: (b, 0, h)),
                  pl.BlockSpec((None, S, MLA_VDIM), lambda b, h: (b, 0, h))],
        out_specs=pl.BlockSpec((None, S, MLA_VDIM), lambda b, h: (b, 0, h)),
        out_shape=jax.ShapeDtypeStruct((B, S, MLA_OUT), BF16),
        compiler_params=_params(("parallel", "parallel"), ((S, MLA_QK_PAD), BF16, 4),
                                ((S, MLA_VDIM), BF16, 4), ((tq, S), F32, 4)),
        name="mla_attn",
    )(q3, k3, v3)
    return out.reshape(B * S, MLA_OUT)


def _even_out_kernel(a_ref, u_ref, v_ref, h_ref, lng_ref, lnb_ref, ws_ref, bs_ref, wo_ref, g_ref,
                     o_ref, on_ref, wm_ref, bo_ref, *, sub):
    tm = a_ref.shape[0]
    pr = lax.broadcasted_iota(jnp.int32, (SGU_BLOCK, SGU_BLOCK), 0) // CHUNK
    pc = lax.broadcasted_iota(jnp.int32, (SGU_BLOCK, SGU_BLOCK), 1) // CHUNK
    for g in range(SGU_GROUPS):
        wm_ref[g] = jnp.where(pr >= pc, ws_ref[g], 0.0).astype(BF16)
    for r0 in range(0, tm, sub):
        rows = slice(r0, r0 + sub)
        gu = jax.nn.gelu(u_ref[rows, :].astype(F32))
        gv = jax.nn.gelu(v_ref[rows, :].astype(F32))
        mu = jnp.mean(gv, axis=-1, keepdims=True)
        xc = gv - mu
        y = xc * lax.rsqrt(jnp.mean(xc * xc, axis=-1, keepdims=True) + EPS)
        vn = (y * lng_ref[...] + lnb_ref[...]).astype(BF16)
        for g in range(SGU_GROUPS):
            bcol = bs_ref[:, g:g + 1]
            cols = slice(g * SGU_CH, (g + 1) * SGU_CH)
            for nb in range(sub // SGU_BLOCK):
                rr = slice(nb * SGU_BLOCK, (nb + 1) * SGU_BLOCK)
                s = jnp.dot(wm_ref[g], vn[rr, cols], preferred_element_type=F32) + bcol
                bo_ref[r0 + nb * SGU_BLOCK:r0 + (nb + 1) * SGU_BLOCK, cols] = (gu[rr, cols] * s).astype(BF16)
        acc = jnp.dot(a_ref[rows, :], wo_ref[0:MLA_OUT, :], preferred_element_type=F32)
        acc = acc + jnp.dot(bo_ref[rows, :], wo_ref[MLA_OUT:, :], preferred_element_type=F32)
        hnew = h_ref[rows, :] + acc
        o_ref[rows, :] = hnew
        on_ref[rows, :] = _rmsnorm_rows(hnew, g_ref[...]).astype(on_ref.dtype)


def _even_out(a, z, h, ln_g, ln_b, w_s, b_s, wo, g_next):
    T, D = h.shape
    tm = _tile(T, 512, SGU_BLOCK)
    sub = _tile(tm, 256, SGU_BLOCK)
    row = lambda i: (i, 0)
    fix = lambda i: (0, 0)
    return pl.pallas_call(
        functools.partial(_even_out_kernel, sub=sub),
        grid=(T // tm,),
        in_specs=[pl.BlockSpec((tm, MLA_OUT), row),
                  pl.BlockSpec((tm, SGU_WIDTH), lambda i: (i, 0)),
                  pl.BlockSpec((tm, SGU_WIDTH), lambda i: (i, 1)),
                  pl.BlockSpec((tm, D), row),
                  pl.BlockSpec((1, SGU_WIDTH), fix),
                  pl.BlockSpec((1, SGU_WIDTH), fix),
                  _const_spec((SGU_GROUPS, SGU_BLOCK, SGU_BLOCK), lambda i: (0, 0, 0)),
                  pl.BlockSpec((SGU_BLOCK, SGU_GROUPS), fix),
                  _const_spec((MLA_OUT + SGU_WIDTH, D), fix),
                  pl.BlockSpec((1, D), fix)],
        out_specs=[pl.BlockSpec((tm, D), row), pl.BlockSpec((tm, D), row)],
        out_shape=[jax.ShapeDtypeStruct((T, D), F32), jax.ShapeDtypeStruct((T, D), BF16)],
        scratch_shapes=[pltpu.VMEM((SGU_GROUPS, SGU_BLOCK, SGU_BLOCK), BF16), pltpu.VMEM((tm, SGU_WIDTH), BF16)],
        compiler_params=_params(("parallel",), ((tm, D), F32, 6), ((tm, D), BF16, 6),
                                ((MLA_OUT + SGU_WIDTH, D), BF16, 1), ((tm, SGU_WIDTH), F32, 4)),
        name="even_out",
    )(a, z, z, h, ln_g.reshape(1, -1), ln_b.reshape(1, -1), w_s, b_s.T, wo, g_next.reshape(1, D))


def _retention_kernel(q_ref, k_ref, v_ref, gate_ref, cos_ref, sin_ref, dm_ref, qd_ref, kd_ref, sd_ref,
                      gng_ref, gnb_ref, o_ref, state_ref, *, L):
    S = q_ref.shape[0]
    half = RET_DK // 2
    state_ref[...] = jnp.zeros_like(state_ref)
    dm = dm_ref[...]
    qd = qd_ref[...]
    kd = kd_ref[...]
    sd = sd_ref[...]
    gng = gng_ref[...]
    gnb = gnb_ref[...]

    def rope(x, cs, sn):
        x1, x2 = x[:, :half], x[:, half:]
        return jnp.concatenate([x1 * cs - x2 * sn, x2 * cs + x1 * sn], axis=1)

    def step(c, carry):
        rows = pl.ds(pl.multiple_of(c * L, L), L)
        cs = cos_ref[rows, :]
        sn = sin_ref[rows, :]
        qr = rope(q_ref[rows, :].astype(F32), cs, sn)
        kr = rope(k_ref[rows, :].astype(F32), cs, sn) * (RET_DK ** -0.5)
        qb = qr.astype(BF16)
        vb = v_ref[rows, :]
        sc = lax.dot_general(qb, kr.astype(BF16), (((1,), (1,)), ((), ())),
                             preferred_element_type=F32) * dm
        intra = jnp.dot(sc.astype(BF16), vb, preferred_element_type=F32)
        st = state_ref[...]
        cross = jnp.dot(qb, st.astype(BF16), preferred_element_type=F32) * qd
        kz = (kr * kd).astype(BF16)
        state_ref[...] = st * sd + lax.dot_general(kz, vb, (((0,), (0,)), ((), ())),
                                                   preferred_element_type=F32)
        out = intra + cross
        mu = jnp.mean(out, axis=-1, keepdims=True)
        xc = out - mu
        y = xc * lax.rsqrt(jnp.mean(xc * xc, axis=-1, keepdims=True) + EPS)
        y = y * gng + gnb
        o_ref[rows, :] = (y * jax.nn.silu(gate_ref[rows, :].astype(F32))).astype(o_ref.dtype)
        return carry

    lax.fori_loop(0, S // L, step, 0)


def _retention_tables(L):
    log_g = jnp.log1p(-(2.0 ** (-5.0 - jnp.arange(RET_HEADS, dtype=F32))))
    idx = jnp.arange(L, dtype=F32)
    chunk = jnp.arange(L) // CHUNK
    decay = jnp.exp(log_g[:, None, None] * jnp.abs(idx[:, None] - idx[None, :]))
    dm = jnp.where(chunk[None, :, None] >= chunk[None, None, :], decay, 0.0)
    qd = jnp.exp(log_g[:, None] * (idx + 1.0)[None, :])[:, :, None]
    kd = jnp.exp(log_g[:, None] * (L - 1 - idx)[None, :])[:, :, None]
    sd = jnp.broadcast_to(jnp.exp(log_g * L)[:, None, None], (RET_HEADS, 1, RET_DV))
    return dm, qd, kd, sd


def _retention(z, cos_t, sin_t, gn_g, gn_b, B, S):
    L = _tile(S, RET_SUPER, CHUNK)
    z3 = z.reshape(B, S, RET_IN_WIDTH)
    cos3 = cos_t.reshape(B, S, -1)
    sin3 = sin_t.reshape(B, S, -1)
    dm, qd, kd, sd = _retention_tables(L)
    qk_blocks = RET_QK_WIDTH // RET_DK
    v_first = 2 * RET_QK_WIDTH // RET_DV
    g_first = v_first + RET_V_WIDTH // RET_DV
    out = pl.pallas_call(
        functools.partial(_retention_kernel, L=L),
        grid=(B, RET_HEADS),
        in_specs=[pl.BlockSpec((None, S, RET_DK), lambda b, h: (b, 0, h)),
                  pl.BlockSpec((None, S, RET_DK), lambda b, h: (b, 0, qk_blocks + h)),
                  pl.BlockSpec((None, S, RET_DV), lambda b, h: (b, 0, v_first + h)),
                  pl.BlockSpec((None, S, RET_DV), lambda b, h: (b, 0, g_first + h)),
                  pl.BlockSpec((None, S, RET_DK // 2), lambda b, h: (b, 0, 0)),
                  pl.BlockSpec((None, S, RET_DK // 2), lambda b, h: (b, 0, 0)),
                  pl.BlockSpec((None, L, L), lambda b, h: (h, 0, 0)),
                  pl.BlockSpec((None, L, 1), lambda b, h: (h, 0, 0)),
                  pl.BlockSpec((None, L, 1), lambda b, h: (h, 0, 0)),
                  pl.BlockSpec((None, 1, RET_DV), lambda b, h: (h, 0, 0)),
                  pl.BlockSpec((1, RET_DV), lambda b, h: (0, h)),
                  pl.BlockSpec((1, RET_DV), lambda b, h: (0, h))],
        out_specs=pl.BlockSpec((None, S, RET_DV), lambda b, h: (b, 0, h)),
        out_shape=jax.ShapeDtypeStruct((B, S, RET_V_WIDTH), BF16),
        scratch_shapes=[pltpu.VMEM((RET_DK, RET_DV), F32)],
        compiler_params=_params(("parallel", "arbitrary"), ((S, RET_DK), BF16, 4), ((S, RET_DV), BF16, 6),
                                ((S, RET_DK // 2), F32, 4), ((L, L), F32, 4), ((L, V7X_LANES), F32, 4),
                                ((RET_DK, RET_DV), F32, 4), ((L, RET_DV), F32, 6)),
        name="retention",
    )(z3, z3, z3, z3, cos3, sin3, dm, qd, kd, sd, gn_g.reshape(1, -1), gn_b.reshape(1, -1))
    return out.reshape(B * S, RET_V_WIDTH)


def _res_mm_kernel(x_ref, w_ref, h_ref, g_ref, o_ref, on_ref, *, sub):
    for r0 in range(0, x_ref.shape[0], sub):
        rows = slice(r0, r0 + sub)
        hnew = h_ref[rows, :] + jnp.dot(x_ref[rows, :], w_ref[...], preferred_element_type=F32)
        o_ref[rows, :] = hnew
        on_ref[rows, :] = _rmsnorm_rows(hnew, g_ref[...]).astype(on_ref.dtype)


def _res_matmul(x, w, h, g_next, tm_pref):
    T, K = x.shape
    D = h.shape[1]
    tm = _tile(T, tm_pref)
    row = lambda i: (i, 0)
    fix = lambda i: (0, 0)
    return pl.pallas_call(
        functools.partial(_res_mm_kernel, sub=_tile(tm, 256)),
        grid=(T // tm,),
        in_specs=[pl.BlockSpec((tm, K), row), _const_spec((K, D), fix),
                  pl.BlockSpec((tm, D), row), pl.BlockSpec((1, D), fix)],
        out_specs=[pl.BlockSpec((tm, D), row), pl.BlockSpec((tm, D), row)],
        out_shape=[jax.ShapeDtypeStruct((T, D), F32), jax.ShapeDtypeStruct((T, D), BF16)],
        compiler_params=_params(("parallel",), ((tm, K), BF16, 2), ((K, D), BF16, 1),
                                ((tm, D), F32, 6), ((tm, D), BF16, 2)),
        name="res_proj",
    )(x, w, h, g_next.reshape(1, D))


def _ffn_up_kernel(x_ref, w_ref, cw_ref, cb_ref, o_ref, e_ref, *, tiles_per_seq, sub):
    tm = x_ref.shape[0]
    tn = o_ref.shape[1]
    halo = V7X_SUBLANES
    lanes = V7X_LANES
    first = (pl.program_id(1) % tiles_per_seq) == 0

    @pl.when(first)
    def _():
        e_ref[:, 0:halo, :] = jnp.zeros((e_ref.shape[0], halo, lanes), F32)

    @pl.when(jnp.logical_not(first))
    def _():
        e_ref[:, 0:halo, :] = e_ref[:, tm:tm + halo, :]

    def conv(k, r0):
        cols = slice(k * lanes, (k + 1) * lanes)
        c = cb_ref[:, cols]
        for j in range(CONV_WIDTH):
            off = halo + r0 - (CONV_WIDTH - 1) + j
            c = c + e_ref[k, off:off + sub, :] * cw_ref[j:j + 1, cols]
        return c

    for r0 in range(0, tm, sub):
        a = jnp.dot(x_ref[r0:r0 + sub, :], w_ref[...], preferred_element_type=F32)
        for k in range(2 * tn // lanes):
            e_ref[k, halo + r0:halo + r0 + sub, :] = a[:, k * lanes:(k + 1) * lanes]
        for k in range(tn // lanes):
            gate = conv(k, r0)
            val = conv(k + tn // lanes, r0)
            o_ref[r0:r0 + sub, k * lanes:(k + 1) * lanes] = (jax.nn.gelu(gate) * val).astype(o_ref.dtype)


def _pair_cols(a, tn):
    lead = a.shape[:-1]
    a = a.reshape(*lead, 2, D_FF // tn, tn)
    return jnp.swapaxes(a, -3, -2).reshape(*lead, 2 * D_FF)


def _ffn_up(hn, w_up, conv_w, conv_b, S):
    T, D = hn.shape
    tn = 512
    tm = _tile(S, 2048)
    sub = _tile(tm, 256)
    halo = V7X_SUBLANES
    return pl.pallas_call(
        functools.partial(_ffn_up_kernel, tiles_per_seq=S // tm, sub=sub),
        grid=(D_FF // tn, T // tm),
        in_specs=[pl.BlockSpec((tm, D), lambda j, i: (i, 0)),
                  pl.BlockSpec((D, 2 * tn), lambda j, i: (0, j)),
                  pl.BlockSpec((CONV_WIDTH, 2 * tn), lambda j, i: (0, j)),
                  pl.BlockSpec((1, 2 * tn), lambda j, i: (0, j))],
        out_specs=pl.BlockSpec((tm, tn), lambda j, i: (i, j)),
        out_shape=jax.ShapeDtypeStruct((T, D_FF), BF16),
        scratch_shapes=[pltpu.VMEM((2 * tn // V7X_LANES, tm + halo, V7X_LANES), F32)],
        compiler_params=_params(("parallel", "arbitrary"), ((tm, D), BF16, 2), ((D, 2 * tn), BF16, 2),
                                ((tm + halo, 2 * tn), F32, 1), ((sub, 2 * tn), F32, 4), ((tm, tn), BF16, 2)),
        name="ffn_up",
    )(hn, _pair_cols(w_up, tn).astype(BF16), _pair_cols(conv_w, tn), _pair_cols(conv_b.reshape(1, -1), tn))


def _ple_kernel(hn_ref, wg_ref, p_ref, wu_ref, h_ref, g_ref, o_ref, on_ref, *, sub):
    for r0 in range(0, hn_ref.shape[0], sub):
        rows = slice(r0, r0 + sub)
        gate = jax.nn.sigmoid(jnp.dot(hn_ref[rows, :], wg_ref[...], preferred_element_type=F32))
        up = jnp.dot(p_ref[rows, :].astype(BF16), wu_ref[...], preferred_element_type=F32)
        hnew = h_ref[rows, :] + up * gate
        o_ref[rows, :] = hnew
        on_ref[rows, :] = _rmsnorm_rows(hnew, g_ref[...]).astype(on_ref.dtype)


def _ple(hn, wg, p_all, layer, wu, h, g_next, out_dtype):
    T, D = h.shape
    tm = _tile(T, 512)
    row = lambda i: (i, 0)
    fix = lambda i: (0, 0)
    return pl.pallas_call(
        functools.partial(_ple_kernel, sub=_tile(tm, 256)),
        grid=(T // tm,),
        in_specs=[pl.BlockSpec((tm, D), row),
                  _const_spec((D, D), fix),
                  pl.BlockSpec((None, tm, PLE_DIM), lambda i: (layer, i, 0)),
                  _const_spec((PLE_DIM, D), fix),
                  pl.BlockSpec((tm, D), row),
                  pl.BlockSpec((1, D), fix)],
        out_specs=[pl.BlockSpec((tm, D), row), pl.BlockSpec((tm, D), row)],
        out_shape=[jax.ShapeDtypeStruct((T, D), F32), jax.ShapeDtypeStruct((T, D), out_dtype)],
        compiler_params=_params(("parallel",), ((tm, D), BF16, 2), ((D, D), BF16, 1), ((PLE_DIM, D), BF16, 1),
                                ((tm, PLE_DIM), F32, 2), ((tm, D), F32, 8)),
        name="ple",
    )(hn, wg, p_all, wu, h, g_next.reshape(1, D))


def _rot_cols(w):
    half = MLA_ROPE // 2
    return jnp.concatenate([-w[..., half:], w[..., :half]], axis=-1)


def _even_in_weight(w):
    cq = w[:, :Q_LORA]
    ckv = w[:, Q_LORA:Q_LORA + KV_LORA]
    kpe = w[:, Q_LORA + KV_LORA:Q_LORA + KV_LORA + MLA_ROPE]
    uv = w[:, Q_LORA + KV_LORA + MLA_ROPE:]
    used = 2 * SGU_WIDTH + Q_LORA + KV_LORA + 2 * MLA_ROPE
    pad = jnp.zeros((w.shape[0], EVEN_IN_PAD - used), w.dtype)
    return jnp.concatenate([uv, cq, ckv, kpe, _rot_cols(kpe), pad], axis=1).astype(BF16)


def _q_up_weight(w):
    w = w.reshape(Q_LORA, MLA_HEADS, MLA_NOPE + MLA_ROPE)
    rope = w[..., MLA_NOPE:]
    w = jnp.concatenate([w[..., :MLA_NOPE], rope, _rot_cols(rope)], axis=-1)
    return w.reshape(Q_LORA, MLA_HEADS * MLA_QK_PAD).astype(BF16)


def _kv_up_weight(w):
    w = w.reshape(KV_LORA, MLA_HEADS, MLA_NOPE + MLA_VDIM)
    k = w[..., :MLA_NOPE].reshape(KV_LORA, -1)
    v = w[..., MLA_NOPE:].reshape(KV_LORA, -1)
    return jnp.concatenate([k, v], axis=1).astype(BF16)


def kernel(x, p, positions, mix_norm_g, even_w_in, mla_q_norm_g, mla_w_q_up, mla_kv_norm_g, mla_w_kv_up,
           sgu_ln_g, sgu_ln_b, sgu_w_s, sgu_b_s, even_w_out, ret_w_in, ret_gn_g, ret_gn_b, ret_w_out,
           ffn_norm_g, ffn_w_up, ffn_conv_w, ffn_conv_b, ffn_w_down, ple_norm_g, ple_w_gate, ple_w_up,
           final_norm_g):
    B, S, D = x.shape
    depth = p.shape[0]
    T = B * S
    assert D == D_MODEL and S % SGU_BLOCK == 0
    cos_t, sin_t, m_tab = _rope_tables(positions)
    h = x.reshape(T, D)
    p_all = p.reshape(depth, T, PLE_DIM)
    hn = _init_norm(h, mix_norm_g[0])
    for i in range(depth):
        j = i // 2
        if i % 2 == 0:
            z = _matmul(hn, _even_in_weight(even_w_in[j]), 1024)
            q, k, v = _mla_prep(z, m_tab, mla_q_norm_g[j], mla_kv_norm_g[j],
                                _q_up_weight(mla_w_q_up[j]), _kv_up_weight(mla_w_kv_up[j]))
            a = _mla_attn(q, k, v, B, S)
            h, hn = _even_out(a, z, h, sgu_ln_g[j], sgu_ln_b[j], sgu_w_s[j], sgu_b_s[j],
                              even_w_out[j].astype(BF16), ffn_norm_g[i])
        else:
            z = _matmul(hn, ret_w_in[j].astype(BF16), 2048)
            r = _retention(z, cos_t, sin_t, ret_gn_g[j], ret_gn_b[j], B, S)
            h, hn = _res_matmul(r, ret_w_out[j].astype(BF16), h, ffn_norm_g[i], 512)
        act = _ffn_up(hn, ffn_w_up[i], ffn_conv_w[i], ffn_conv_b[i], S)
        h, hn = _res_matmul(act, ffn_w_down[i].astype(BF16), h, ple_norm_g[i], 256)
        last = i == depth - 1
        g_next = final_norm_g if last else mix_norm_g[i + 1]
        h, hn = _ple(hn, ple_w_gate[i].astype(BF16), p_all, i, ple_w_up[i].astype(BF16), h, g_next,
                     F32 if last else BF16)
    return hn.reshape(B, S, D)
```

```python
import functools
import math

import jax
import jax.numpy as jnp
from jax import lax
from jax.experimental import pallas as pl
from jax.experimental.pallas import tpu as pltpu

F32 = jnp.float32
BF16 = jnp.bfloat16

D_MODEL = 2048
CHUNK = 64
EPS = 1e-6
ROPE_THETA = 10000.0
MLA_HEADS = 8
MLA_NOPE = 128
MLA_ROPE = 64
MLA_VDIM = 128
Q_LORA = 512
KV_LORA = 256
MLA_OUT = MLA_HEADS * MLA_VDIM
MLA_QK_PAD = 256
SGU_BLOCK = 128
SGU_GROUPS = 8
SGU_CH = 128
SGU_WIDTH = SGU_GROUPS * SGU_CH
EVEN_IN_PAD = 3072
RET_HEADS = 8
RET_DK = 256
RET_DV = 512
RET_QK_WIDTH = RET_HEADS * RET_DK
RET_V_WIDTH = RET_HEADS * RET_DV
RET_IN_WIDTH = 2 * RET_QK_WIDTH + 2 * RET_V_WIDTH
RET_SUPER = 256
D_FF = 5632
CONV_WIDTH = 3
PLE_DIM = 256

V7X_LANES = 128
V7X_SUBLANES = 8
V7X_VMEM_BUDGET = 58 * 1024 * 1024


def _nbytes(shape, dtype):
    return math.prod(shape) * jnp.dtype(dtype).itemsize


def _params(semantics, *buffers):
    need = sum(_nbytes(s, d) * n for s, d, n in buffers)
    return pltpu.CompilerParams(dimension_semantics=semantics,
                                vmem_limit_bytes=min(V7X_VMEM_BUDGET, need + (4 << 20)))


def _tile(n, pref, mult=V7X_SUBLANES):
    t = min(n, pref)
    while n % t or t % mult:
        t -= 1
    return t


def _const_spec(shape, index_map):
    return pl.BlockSpec(shape, index_map, pipeline_mode=pl.Buffered(1))


def _rmsnorm_rows(x, g):
    ms = jnp.mean(x * x, axis=-1, keepdims=True)
    return x * lax.rsqrt(ms + EPS) * g


def _rope_tables_kernel(pos_ref, fr_ref, fm_ref, cos_ref, sin_ref, m_ref):
    pos = pos_ref[...].astype(F32)
    ang = pos * fr_ref[...]
    cos_ref[...] = jnp.cos(ang)
    sin_ref[...] = jnp.sin(ang)
    angm = pos * fm_ref[...]
    lane = lax.broadcasted_iota(jnp.int32, angm.shape, 1)
    m_ref[...] = jnp.where(lane < MLA_ROPE, jnp.cos(angm), jnp.sin(angm))


def _rope_tables(positions):
    T = positions.size
    tm = _tile(T, 1024)
    half_r = RET_DK // 2
    half_m = MLA_ROPE // 2
    fr = (ROPE_THETA ** (-jnp.arange(half_r, dtype=F32) / half_r))[None, :]
    fm1 = ROPE_THETA ** (-jnp.arange(half_m, dtype=F32) / half_m)
    fm = jnp.tile(fm1, 4)[None, :]
    tab = jax.ShapeDtypeStruct((T, V7X_LANES), F32)
    row = pl.BlockSpec((tm, V7X_LANES), lambda i: (i, 0))
    frq = pl.BlockSpec((1, V7X_LANES), lambda i: (0, 0))
    return pl.pallas_call(
        _rope_tables_kernel,
        grid=(T // tm,),
        in_specs=[pl.BlockSpec((tm, 1), lambda i: (i, 0)), frq, frq],
        out_specs=[row, row, row],
        out_shape=[tab, tab, tab],
        compiler_params=_params(("parallel",), ((tm, V7X_LANES), F32, 16)),
        name="rope_tables",
    )(positions.reshape(T, 1), fr, fm)


def _init_norm_kernel(x_ref, g_ref, o_ref):
    o_ref[...] = _rmsnorm_rows(x_ref[...], g_ref[...]).astype(o_ref.dtype)


def _init_norm(x, g):
    T, D = x.shape
    tm = _tile(T, 512)
    return pl.pallas_call(
        _init_norm_kernel,
        grid=(T // tm,),
        in_specs=[pl.BlockSpec((tm, D), lambda i: (i, 0)), pl.BlockSpec((1, D), lambda i: (0, 0))],
        out_specs=pl.BlockSpec((tm, D), lambda i: (i, 0)),
        out_shape=jax.ShapeDtypeStruct((T, D), BF16),
        compiler_params=_params(("parallel",), ((tm, D), F32, 4), ((tm, D), BF16, 2)),
        name="init_norm",
    )(x, g.reshape(1, D))


def _mm_kernel(x_ref, w_ref, o_ref):
    o_ref[...] = jnp.dot(x_ref[...], w_ref[...], preferred_element_type=F32).astype(o_ref.dtype)


def _matmul(x, w_all, layer, tn):
    T, K = x.shape
    N = w_all.shape[2]
    tm = _tile(T, 1024)
    return pl.pallas_call(
        _mm_kernel,
        grid=(T // tm, N // tn),
        in_specs=[pl.BlockSpec((tm, K), lambda i, j: (i, 0)),
                  pl.BlockSpec((None, K, tn), lambda i, j: (layer, 0, j))],
        out_specs=pl.BlockSpec((tm, tn), lambda i, j: (i, j)),
        out_shape=jax.ShapeDtypeStruct((T, N), BF16),
        compiler_params=_params(("parallel", "arbitrary"), ((tm, K), BF16, 2), ((K, tn), BF16, 2),
                                ((tm, tn), BF16, 2), ((tm, tn), F32, 1)),
        name="proj",
    )(x, w_all)


def _mla_prep_kernel(cq_ref, ckv_ref, kpe_ref, m_ref, qg_ref, kvg_ref, wq_ref, wkv_ref,
                     q_ref, k_ref, v_ref):
    scale = (MLA_NOPE + MLA_ROPE) ** -0.5
    m = m_ref[...]
    cqn = _rmsnorm_rows(cq_ref[...].astype(F32), qg_ref[...]).astype(BF16)
    q = jnp.dot(cqn, wq_ref[...], preferred_element_type=F32)
    for h in range(MLA_HEADS):
        c0 = h * MLA_QK_PAD
        q_ref[:, c0:c0 + MLA_NOPE] = (q[:, c0:c0 + MLA_NOPE] * scale).astype(BF16)
        y = q[:, c0 + MLA_NOPE:c0 + MLA_QK_PAD] * m
        y = y + pltpu.roll(y, MLA_ROPE, axis=1)
        q_ref[:, c0 + MLA_NOPE:c0 + MLA_QK_PAD] = (y * scale).astype(BF16)
    ckvn = _rmsnorm_rows(ckv_ref[...].astype(F32), kvg_ref[...]).astype(BF16)
    kv = jnp.dot(ckvn, wkv_ref[...], preferred_element_type=F32)
    yk = kpe_ref[...].astype(F32) * m
    yk = yk + pltpu.roll(yk, MLA_ROPE, axis=1)
    lane = lax.broadcasted_iota(jnp.int32, yk.shape, 1)
    k_rot = jnp.where(lane < MLA_ROPE, yk, 0.0).astype(BF16)
    for h in range(MLA_HEADS):
        c0 = h * MLA_QK_PAD
        k_ref[:, c0:c0 + MLA_NOPE] = kv[:, h * MLA_NOPE:(h + 1) * MLA_NOPE].astype(BF16)
        k_ref[:, c0 + MLA_NOPE:c0 + MLA_QK_PAD] = k_rot
    v_ref[...] = kv[:, MLA_HEADS * MLA_NOPE:].astype(BF16)


def _mla_prep(z, m_tab, q_norm_g, kv_norm_g, wq, wkv):
    T = z.shape[0]
    tm = _tile(T, 512)
    qkw = MLA_HEADS * MLA_QK_PAD
    u_v = 2 * SGU_WIDTH
    return pl.pallas_call(
        _mla_prep_kernel,
        grid=(T // tm,),
        in_specs=[
            pl.BlockSpec((tm, Q_LORA), lambda i: (i, u_v // Q_LORA)),
            pl.BlockSpec((tm, KV_LORA), lambda i: (i, (u_v + Q_LORA) // KV_LORA)),
            pl.BlockSpec((tm, 2 * MLA_ROPE), lambda i: (i, (u_v + Q_LORA + KV_LORA) // (2 * MLA_ROPE))),
            pl.BlockSpec((tm, V7X_LANES), lambda i: (i, 0)),
            pl.BlockSpec((1, Q_LORA), lambda i: (0, 0)),
            pl.BlockSpec((1, KV_LORA), lambda i: (0, 0)),
            _const_spec((Q_LORA, qkw), lambda i: (0, 0)),
            _const_spec((KV_LORA, qkw), lambda i: (0, 0)),
        ],
        out_specs=[pl.BlockSpec((tm, qkw), lambda i: (i, 0)),
                   pl.BlockSpec((tm, qkw), lambda i: (i, 0)),
                   pl.BlockSpec((tm, MLA_OUT), lambda i: (i, 0))],
        out_shape=[jax.ShapeDtypeStruct((T, qkw), BF16), jax.ShapeDtypeStruct((T, qkw), BF16),
                   jax.ShapeDtypeStruct((T, MLA_OUT), BF16)],
        compiler_params=_params(("parallel",), ((tm, qkw), BF16, 8), ((tm, qkw), F32, 3),
                                ((Q_LORA + KV_LORA, qkw), BF16, 1)),
        name="mla_prep",
    )(z, z, z, m_tab, q_norm_g.reshape(1, -1), kv_norm_g.reshape(1, -1), wq, wkv)


def _mla_attn_kernel(q_ref, k_ref, v_ref, o_ref, *, tq):
    S = q_ref.shape[0]
    r = lax.broadcasted_iota(jnp.int32, (tq, tq), 0) // CHUNK
    c = lax.broadcasted_iota(jnp.int32, (tq, tq), 1) // CHUNK
    bias = jnp.where(c <= r, 0.0, -1e30).astype(F32)
    for qi in range(S // tq):
        past = qi * tq
        q = q_ref[past:past + tq, :]
        s = lax.dot_general(q, k_ref[0:past + tq, :], (((1,), (1,)), ((), ())),
                            preferred_element_type=F32)
        s_diag = s[:, past:] + bias
        s = s_diag if qi == 0 else jnp.concatenate([s[:, :past], s_diag], axis=1)
        mx = jnp.max(s, axis=-1, keepdims=True)
        p = jnp.exp(s - mx)
        l = jnp.sum(p, axis=-1, keepdims=True)
        o = jnp.dot(p.astype(BF16), v_ref[0:past + tq, :], preferred_element_type=F32)
        o_ref[past:past + tq, :] = (o / l).astype(o_ref.dtype)


def _mla_attn(q, k, v, B, S):
    tq = _tile(S, 256, CHUNK)
    q3 = q.reshape(B, S, -1)
    k3 = k.reshape(B, S, -1)
    v3 = v.reshape(B, S, -1)
    out = pl.pallas_call(
        functools.partial(_mla_attn_kernel, tq=tq),
        grid=(B, MLA_HEADS),
        in_specs=[pl.BlockSpec((None, S, MLA_QK_PAD), lambda b, h: (b, 0, h)),
                  pl.BlockSpec((None, S, MLA_QK_PAD), lambda b, h: (b, 0, h)),
                  pl.BlockSpec((None, S, MLA_VDIM), lambda b, h: (b, 0, h))],
        out_specs=pl.BlockSpec((None, S, MLA_VDIM), lambda b, h: (b, 0, h)),
        out_shape=jax.ShapeDtypeStruct((B, S, MLA_OUT), BF16),
        compiler_params=_params(("parallel", "parallel"), ((S, MLA_QK_PAD), BF16, 4),
                                ((S, MLA_VDIM), BF16, 4), ((tq, S), F32, 4)),
        name="mla_attn",
    )(q3, k3, v3)
    return out.reshape(B * S, MLA_OUT)


def _even_out_kernel(a_ref, u_ref, v_ref, h_ref, lng_ref, lnb_ref, ws_ref, bs_ref, wo_ref, g_ref,
                     o_ref, on_ref, wm_ref, bo_ref, *, sub):
    tm = a_ref.shape[0]
    pr = lax.broadcasted_iota(jnp.int32, (SGU_BLOCK, SGU_BLOCK), 0) // CHUNK
    pc = lax.broadcasted_iota(jnp.int32, (SGU_BLOCK, SGU_BLOCK), 1) // CHUNK
    for g in range(SGU_GROUPS):
        wm_ref[g] = jnp.where(pr >= pc, ws_ref[g], 0.0).astype(BF16)
    for r0 in range(0, tm, sub):
        rows = slice(r0, r0 + sub)
        gu = jax.nn.gelu(u_ref[rows, :].astype(F32))
        gv = jax.nn.gelu(v_ref[rows, :].astype(F32))
        mu = jnp.mean(gv, axis=-1, keepdims=True)
        xc = gv - mu
        y = xc * lax.rsqrt(jnp.mean(xc * xc, axis=-1, keepdims=True) + EPS)
        vn = (y * lng_ref[...] + lnb_ref[...]).astype(BF16)
        for g in range(SGU_GROUPS):
            bcol = bs_ref[:, g:g + 1]
            cols = slice(g * SGU_CH, (g + 1) * SGU_CH)
            for nb in range(sub // SGU_BLOCK):
                rr = slice(nb * SGU_BLOCK, (nb + 1) * SGU_BLOCK)
                s = jnp.dot(wm_ref[g], vn[rr, cols], preferred_element_type=F32) + bcol
                bo_ref[r0 + nb * SGU_BLOCK:r0 + (nb + 1) * SGU_BLOCK, cols] = (gu[rr, cols] * s).astype(BF16)
        acc = jnp.dot(a_ref[rows, :], wo_ref[0:MLA_OUT, :], preferred_element_type=F32)
        acc = acc + jnp.dot(bo_ref[rows, :], wo_ref[MLA_OUT:, :], preferred_element_type=F32)
        hnew = h_ref[rows, :] + acc
        o_ref[rows, :] = hnew
        on_ref[rows, :] = _rmsnorm_rows(hnew, g_ref[...]).astype(on_ref.dtype)


def _even_out(a, z, h, ln_g, ln_b, w_s, b_s, wo_all, layer, g_next):
    T, D = h.shape
    tm = _tile(T, 512, SGU_BLOCK)
    sub = _tile(tm, 256, SGU_BLOCK)
    row = lambda i: (i, 0)
    fix = lambda i: (0, 0)
    return pl.pallas_call(
        functools.partial(_even_out_kernel, sub=sub),
        grid=(T // tm,),
        in_specs=[pl.BlockSpec((tm, MLA_OUT), row),
                  pl.BlockSpec((tm, SGU_WIDTH), lambda i: (i, 0)),
                  pl.BlockSpec((tm, SGU_WIDTH), lambda i: (i, 1)),
                  pl.BlockSpec((tm, D), row),
                  pl.BlockSpec((1, SGU_WIDTH), fix),
                  pl.BlockSpec((1, SGU_WIDTH), fix),
                  _const_spec((SGU_GROUPS, SGU_BLOCK, SGU_BLOCK), lambda i: (0, 0, 0)),
                  pl.BlockSpec((SGU_BLOCK, SGU_GROUPS), fix),
                  _const_spec((None, MLA_OUT + SGU_WIDTH, D), lambda i: (layer, 0, 0)),
                  pl.BlockSpec((1, D), fix)],
        out_specs=[pl.BlockSpec((tm, D), row), pl.BlockSpec((tm, D), row)],
        out_shape=[jax.ShapeDtypeStruct((T, D), F32), jax.ShapeDtypeStruct((T, D), BF16)],
        scratch_shapes=[pltpu.VMEM((SGU_GROUPS, SGU_BLOCK, SGU_BLOCK), BF16), pltpu.VMEM((tm, SGU_WIDTH), BF16)],
        compiler_params=_params(("parallel",), ((tm, D), F32, 6), ((tm, D), BF16, 6),
                                ((MLA_OUT + SGU_WIDTH, D), BF16, 1), ((tm, SGU_WIDTH), F32, 4)),
        name="even_out",
    )(a, z, z, h, ln_g.reshape(1, -1), ln_b.reshape(1, -1), w_s, b_s.T, wo_all, g_next.reshape(1, D))


def _retention_kernel(q_ref, k_ref, v_ref, gate_ref, cos_ref, sin_ref, dm_ref, qd_ref, kd_ref, sd_ref,
                      gng_ref, gnb_ref, o_ref, state_ref, *, L):
    S = q_ref.shape[0]
    half = RET_DK // 2
    state_ref[...] = jnp.zeros_like(state_ref)
    dm = dm_ref[...]
    qd = qd_ref[...]
    kd = kd_ref[...]
    sd = sd_ref[...]
    gng = gng_ref[...]
    gnb = gnb_ref[...]

    def rope(x, cs, sn):
        x1, x2 = x[:, :half], x[:, half:]
        return jnp.concatenate([x1 * cs - x2 * sn, x2 * cs + x1 * sn], axis=1)

    def step(c, carry):
        rows = pl.ds(pl.multiple_of(c * L, L), L)
        cs = cos_ref[rows, :]
        sn = sin_ref[rows, :]
        qr = rope(q_ref[rows, :].astype(F32), cs, sn)
        kr = rope(k_ref[rows, :].astype(F32), cs, sn) * (RET_DK ** -0.5)
        qb = qr.astype(BF16)
        vb = v_ref[rows, :]
        sc = lax.dot_general(qb, kr.astype(BF16), (((1,), (1,)), ((), ())),
                             preferred_element_type=F32) * dm
        intra = jnp.dot(sc.astype(BF16), vb, preferred_element_type=F32)
        st = state_ref[...]
        cross = jnp.dot(qb, st.astype(BF16), preferred_element_type=F32) * qd
        kz = (kr * kd).astype(BF16)
        state_ref[...] = st * sd + lax.dot_general(kz, vb, (((0,), (0,)), ((), ())),
                                                   preferred_element_type=F32)
        out = intra + cross
        mu = jnp.mean(out, axis=-1, keepdims=True)
        xc = out - mu
        y = xc * lax.rsqrt(jnp.mean(xc * xc, axis=-1, keepdims=True) + EPS)
        y = y * gng + gnb
        o_ref[rows, :] = (y * jax.nn.silu(gate_ref[rows, :].astype(F32))).astype(o_ref.dtype)
        return carry

    lax.fori_loop(0, S // L, step, 0)


def _retention_tables(L):
    log_g = jnp.log1p(-(2.0 ** (-5.0 - jnp.arange(RET_HEADS, dtype=F32))))
    idx = jnp.arange(L, dtype=F32)
    chunk = jnp.arange(L) // CHUNK
    decay = jnp.exp(log_g[:, None, None] * jnp.abs(idx[:, None] - idx[None, :]))
    dm = jnp.where(chunk[None, :, None] >= chunk[None, None, :], decay, 0.0)
    qd = jnp.exp(log_g[:, None] * (idx + 1.0)[None, :])[:, :, None]
    kd = jnp.exp(log_g[:, None] * (L - 1 - idx)[None, :])[:, :, None]
    sd = jnp.broadcast_to(jnp.exp(log_g * L)[:, None, None], (RET_HEADS, 1, RET_DV))
    return dm, qd, kd, sd


def _retention(z, cos_t, sin_t, gn_g, gn_b, B, S):
    L = _tile(S, RET_SUPER, CHUNK)
    z3 = z.reshape(B, S, RET_IN_WIDTH)
    cos3 = cos_t.reshape(B, S, -1)
    sin3 = sin_t.reshape(B, S, -1)
    dm, qd, kd, sd = _retention_tables(L)
    qk_blocks = RET_QK_WIDTH // RET_DK
    v_first = 2 * RET_QK_WIDTH // RET_DV
    g_first = v_first + RET_V_WIDTH // RET_DV
    out = pl.pallas_call(
        functools.partial(_retention_kernel, L=L),
        grid=(B, RET_HEADS),
        in_specs=[pl.BlockSpec((None, S, RET_DK), lambda b, h: (b, 0, h)),
                  pl.BlockSpec((None, S, RET_DK), lambda b, h: (b, 0, qk_blocks + h)),
                  pl.BlockSpec((None, S, RET_DV), lambda b, h: (b, 0, v_first + h)),
                  pl.BlockSpec((None, S, RET_DV), lambda b, h: (b, 0, g_first + h)),
                  pl.BlockSpec((None, S, RET_DK // 2), lambda b, h: (b, 0, 0)),
                  pl.BlockSpec((None, S, RET_DK // 2), lambda b, h: (b, 0, 0)),
                  pl.BlockSpec((None, L, L), lambda b, h: (h, 0, 0)),
                  pl.BlockSpec((None, L, 1), lambda b, h: (h, 0, 0)),
                  pl.BlockSpec((None, L, 1), lambda b, h: (h, 0, 0)),
                  pl.BlockSpec((None, 1, RET_DV), lambda b, h: (h, 0, 0)),
                  pl.BlockSpec((1, RET_DV), lambda b, h: (0, h)),
                  pl.BlockSpec((1, RET_DV), lambda b, h: (0, h))],
        out_specs=pl.BlockSpec((None, S, RET_DV), lambda b, h: (b, 0, h)),
        out_shape=jax.ShapeDtypeStruct((B, S, RET_V_WIDTH), BF16),
        scratch_shapes=[pltpu.VMEM((RET_DK, RET_DV), F32)],
        compiler_params=_params(("parallel", "arbitrary"), ((S, RET_DK), BF16, 4), ((S, RET_DV), BF16, 6),
                                ((S, RET_DK // 2), F32, 4), ((L, L), F32, 4), ((L, V7X_LANES), F32, 4),
                                ((RET_DK, RET_DV), F32, 4), ((L, RET_DV), F32, 6)),
        name="retention",
    )(z3, z3, z3, z3, cos3, sin3, dm, qd, kd, sd, gn_g.reshape(1, -1), gn_b.reshape(1, -1))
    return out.reshape(B * S, RET_V_WIDTH)


def _res_mm_kernel(x_ref, w_ref, h_ref, g_ref, o_ref, on_ref, *, sub):
    for r0 in range(0, x_ref.shape[0], sub):
        rows = slice(r0, r0 + sub)
        hnew = h_ref[rows, :] + jnp.dot(x_ref[rows, :], w_ref[...], preferred_element_type=F32)
        o_ref[rows, :] = hnew
        on_ref[rows, :] = _rmsnorm_rows(hnew, g_ref[...]).astype(on_ref.dtype)


def _res_matmul(x, w_all, layer, h, g_next, tm_pref):
    T, K = x.shape
    D = h.shape[1]
    tm = _tile(T, tm_pref)
    row = lambda i: (i, 0)
    fix = lambda i: (0, 0)
    return pl.pallas_call(
        functools.partial(_res_mm_kernel, sub=_tile(tm, 256)),
        grid=(T // tm,),
        in_specs=[pl.BlockSpec((tm, K), row), _const_spec((None, K, D), lambda i: (layer, 0, 0)),
                  pl.BlockSpec((tm, D), row), pl.BlockSpec((1, D), fix)],
        out_specs=[pl.BlockSpec((tm, D), row), pl.BlockSpec((tm, D), row)],
        out_shape=[jax.ShapeDtypeStruct((T, D), F32), jax.ShapeDtypeStruct((T, D), BF16)],
        compiler_params=_params(("parallel",), ((tm, K), BF16, 2), ((K, D), BF16, 1),
                                ((tm, D), F32, 6), ((tm, D), BF16, 2)),
        name="res_proj",
    )(x, w_all, h, g_next.reshape(1, D))


def _ffn_up_kernel(x_ref, wg_ref, wv_ref, cwg_ref, cwv_ref, cbg_ref, cbv_ref, o_ref, wb_ref, e_ref,
                   *, tiles_per_seq, sub):
    tm = x_ref.shape[0]
    tn = o_ref.shape[1]
    halo = V7X_SUBLANES
    lanes = V7X_LANES
    nslab = tn // lanes
    i = pl.program_id(1)
    first = (i % tiles_per_seq) == 0

    @pl.when(i == 0)
    def _():
        wb_ref[:, 0:tn] = wg_ref[...].astype(BF16)
        wb_ref[:, tn:2 * tn] = wv_ref[...].astype(BF16)

    @pl.when(first)
    def _():
        e_ref[:, 0:halo, :] = jnp.zeros((e_ref.shape[0], halo, lanes), F32)

    @pl.when(jnp.logical_not(first))
    def _():
        e_ref[:, 0:halo, :] = e_ref[:, tm:tm + halo, :]

    def conv(k, r0, cw_ref, cb_ref, kc):
        cols = slice(kc * lanes, (kc + 1) * lanes)
        c = cb_ref[:, cols]
        for j in range(CONV_WIDTH):
            off = halo + r0 - (CONV_WIDTH - 1) + j
            c = c + e_ref[k, off:off + sub, :] * cw_ref[j:j + 1, cols]
        return c

    for r0 in range(0, tm, sub):
        a = jnp.dot(x_ref[r0:r0 + sub, :], wb_ref[...], preferred_element_type=F32)
        for k in range(2 * nslab):
            e_ref[k, halo + r0:halo + r0 + sub, :] = a[:, k * lanes:(k + 1) * lanes]
        for k in range(nslab):
            gate = conv(k, r0, cwg_ref, cbg_ref, k)
            val = conv(k + nslab, r0, cwv_ref, cbv_ref, k)
            o_ref[r0:r0 + sub, k * lanes:(k + 1) * lanes] = (jax.nn.gelu(gate) * val).astype(o_ref.dtype)


def _ffn_up(hn, w_up_all, conv_w_all, conv_b_all, layer, S):
    T, D = hn.shape
    tn = 512
    tm = _tile(S, 2048)
    sub = _tile(tm, 256)
    nj = D_FF // tn
    halo = V7X_SUBLANES
    conv_b3 = conv_b_all.reshape(conv_b_all.shape[0], 1, 2 * D_FF)
    gate_col = lambda j, i: (layer, 0, j)
    val_col = lambda j, i: (layer, 0, nj + j)
    return pl.pallas_call(
        functools.partial(_ffn_up_kernel, tiles_per_seq=S // tm, sub=sub),
        grid=(nj, T // tm),
        in_specs=[pl.BlockSpec((tm, D), lambda j, i: (i, 0)),
                  pl.BlockSpec((None, D, tn), gate_col),
                  pl.BlockSpec((None, D, tn), val_col),
                  pl.BlockSpec((None, CONV_WIDTH, tn), gate_col),
                  pl.BlockSpec((None, CONV_WIDTH, tn), val_col),
                  pl.BlockSpec((None, 1, tn), gate_col),
                  pl.BlockSpec((None, 1, tn), val_col)],
        out_specs=pl.BlockSpec((tm, tn), lambda j, i: (i, j)),
        out_shape=jax.ShapeDtypeStruct((T, D_FF), BF16),
        scratch_shapes=[pltpu.VMEM((D, 2 * tn), BF16),
                        pltpu.VMEM((2 * tn // V7X_LANES, tm + halo, V7X_LANES), F32)],
        compiler_params=_params(("arbitrary", "arbitrary"), ((tm, D), BF16, 2), ((D, tn), F32, 4),
                                ((D, 2 * tn), BF16, 1), ((tm + halo, 2 * tn), F32, 1),
                                ((sub, 2 * tn), F32, 4), ((tm, tn), BF16, 2)),
        name="ffn_up",
    )(hn, w_up_all, w_up_all, conv_w_all, conv_w_all, conv_b3, conv_b3)


def _ple_kernel(hn_ref, wg_ref, p_ref, wu_ref, h_ref, g_ref, *out_refs, sub):
    on_ref = out_refs[-1]
    for r0 in range(0, hn_ref.shape[0], sub):
        rows = slice(r0, r0 + sub)
        gate = jax.nn.sigmoid(jnp.dot(hn_ref[rows, :], wg_ref[...], preferred_element_type=F32))
        up = jnp.dot(p_ref[rows, :].astype(BF16), wu_ref[...], preferred_element_type=F32)
        hnew = h_ref[rows, :] + up * gate
        if len(out_refs) == 2:
            out_refs[0][rows, :] = hnew
        on_ref[rows, :] = _rmsnorm_rows(hnew, g_ref[...]).astype(on_ref.dtype)


def _ple(hn, wg_all, p_all, layer, wu_all, h, g_next, last):
    T, D = h.shape
    tm = _tile(T, 512)
    row = lambda i: (i, 0)
    fix = lambda i: (0, 0)
    out_specs = [pl.BlockSpec((tm, D), row), pl.BlockSpec((tm, D), row)]
    out_shape = [jax.ShapeDtypeStruct((T, D), F32), jax.ShapeDtypeStruct((T, D), BF16)]
    if last:
        out_specs, out_shape = out_specs[:1], out_shape[:1]
    return pl.pallas_call(
        functools.partial(_ple_kernel, sub=_tile(tm, 256)),
        grid=(T // tm,),
        in_specs=[pl.BlockSpec((tm, D), row),
                  _const_spec((None, D, D), lambda i: (layer, 0, 0)),
                  pl.BlockSpec((None, tm, PLE_DIM), lambda i: (layer, i, 0)),
                  _const_spec((None, PLE_DIM, D), lambda i: (layer, 0, 0)),
                  pl.BlockSpec((tm, D), row),
                  pl.BlockSpec((1, D), fix)],
        out_specs=out_specs,
        out_shape=out_shape,
        compiler_params=_params(("parallel",), ((tm, D), BF16, 2), ((D, D), BF16, 1), ((PLE_DIM, D), BF16, 1),
                                ((tm, PLE_DIM), F32, 2), ((tm, D), F32, 8)),
        name="ple",
    )(hn, wg_all, p_all, wu_all, h, g_next.reshape(1, D))


def _rot_cols(w):
    half = MLA_ROPE // 2
    return jnp.concatenate([-w[..., half:], w[..., :half]], axis=-1)


def _even_in_weight(w):
    cq = w[..., :Q_LORA]
    ckv = w[..., Q_LORA:Q_LORA + KV_LORA]
    kpe = w[..., Q_LORA + KV_LORA:Q_LORA + KV_LORA + MLA_ROPE]
    uv = w[..., Q_LORA + KV_LORA + MLA_ROPE:]
    used = 2 * SGU_WIDTH + Q_LORA + KV_LORA + 2 * MLA_ROPE
    pad = jnp.zeros(w.shape[:-1] + (EVEN_IN_PAD - used,), w.dtype)
    return jnp.concatenate([uv, cq, ckv, kpe, _rot_cols(kpe), pad], axis=-1).astype(BF16)


def _q_up_weight(w):
    w = w.reshape(Q_LORA, MLA_HEADS, MLA_NOPE + MLA_ROPE)
    rope = w[..., MLA_NOPE:]
    w = jnp.concatenate([w[..., :MLA_NOPE], rope, _rot_cols(rope)], axis=-1)
    return w.reshape(Q_LORA, MLA_HEADS * MLA_QK_PAD).astype(BF16)


def _kv_up_weight(w):
    w = w.reshape(KV_LORA, MLA_HEADS, MLA_NOPE + MLA_VDIM)
    k = w[..., :MLA_NOPE].reshape(KV_LORA, -1)
    v = w[..., MLA_NOPE:].reshape(KV_LORA, -1)
    return jnp.concatenate([k, v], axis=1).astype(BF16)


def kernel(x, p, positions, mix_norm_g, even_w_in, mla_q_norm_g, mla_w_q_up, mla_kv_norm_g, mla_w_kv_up,
           sgu_ln_g, sgu_ln_b, sgu_w_s, sgu_b_s, even_w_out, ret_w_in, ret_gn_g, ret_gn_b, ret_w_out,
           ffn_norm_g, ffn_w_up, ffn_conv_w, ffn_conv_b, ffn_w_down, ple_norm_g, ple_w_gate, ple_w_up,
           final_norm_g):
    B, S, D = x.shape
    depth = p.shape[0]
    T = B * S
    assert D == D_MODEL and S % SGU_BLOCK == 0
    cos_t, sin_t, m_tab = _rope_tables(positions)
    h = x.reshape(T, D)
    p_all = p.reshape(depth, T, PLE_DIM)
    hn = _init_norm(h, mix_norm_g[0])
    even_in_b = _even_in_weight(even_w_in)
    even_out_b = even_w_out.astype(BF16)
    ret_in_b = ret_w_in.astype(BF16)
    ret_out_b = ret_w_out.astype(BF16)
    ffn_down_b = ffn_w_down.astype(BF16)
    ple_gate_b = ple_w_gate.astype(BF16)
    ple_up_b = ple_w_up.astype(BF16)
    for i in range(depth):
        j = i // 2
        if i % 2 == 0:
            z = _matmul(hn, even_in_b, j, 1024)
            q, k, v = _mla_prep(z, m_tab, mla_q_norm_g[j], mla_kv_norm_g[j],
                                _q_up_weight(mla_w_q_up[j]), _kv_up_weight(mla_w_kv_up[j]))
            a = _mla_attn(q, k, v, B, S)
            h, hn = _even_out(a, z, h, sgu_ln_g[j], sgu_ln_b[j], sgu_w_s[j], sgu_b_s[j],
                              even_out_b, j, ffn_norm_g[i])
        else:
            z = _matmul(hn, ret_in_b, j, 2048)
            r = _retention(z, cos_t, sin_t, ret_gn_g[j], ret_gn_b[j], B, S)
            h, hn = _res_matmul(r, ret_out_b, j, h, ffn_norm_g[i], 512)
        act = _ffn_up(hn, ffn_w_up, ffn_conv_w, ffn_conv_b, i, S)
        h, hn = _res_matmul(act, ffn_down_b, i, h, ple_norm_g[i], 256)
        if i < depth - 1:
            h, hn = _ple(hn, ple_gate_b, p_all, i, ple_up_b, h, mix_norm_g[i + 1], False)
        else:
            out, = _ple(hn, ple_gate_b, p_all, i, ple_up_b, h, final_norm_g, True)
    return out.reshape(B, S, D)
```

```python
import functools
import math

import jax
import jax.numpy as jnp
from jax import lax
from jax.experimental import pallas as pl
from jax.experimental.pallas import tpu as pltpu

F32 = jnp.float32
BF16 = jnp.bfloat16

D_MODEL = 2048
CHUNK = 64
EPS = 1e-6
ROPE_THETA = 10000.0
MLA_HEADS = 8
MLA_NOPE = 128
MLA_ROPE = 64
MLA_VDIM = 128
Q_LORA = 512
KV_LORA = 256
MLA_OUT = MLA_HEADS * MLA_VDIM
MLA_QK_PAD = 256
SGU_BLOCK = 128
SGU_GROUPS = 8
SGU_CH = 128
SGU_WIDTH = SGU_GROUPS * SGU_CH
EVEN_IN_PAD = 3072
RET_HEADS = 8
RET_DK = 256
RET_DV = 512
RET_QK_WIDTH = RET_HEADS * RET_DK
RET_V_WIDTH = RET_HEADS * RET_DV
RET_IN_WIDTH = 2 * RET_QK_WIDTH + 2 * RET_V_WIDTH
RET_SUPER = 256
D_FF = 5632
CONV_WIDTH = 3
PLE_DIM = 256

V7X_LANES = 128
V7X_SUBLANES = 8
V7X_VMEM_BUDGET = 58 * 1024 * 1024


def _nbytes(shape, dtype):
    return math.prod(shape) * jnp.dtype(dtype).itemsize


def _params(semantics, *buffers):
    need = sum(_nbytes(s, d) * n for s, d, n in buffers)
    return pltpu.CompilerParams(dimension_semantics=semantics,
                                vmem_limit_bytes=min(V7X_VMEM_BUDGET, need + (4 << 20)))


def _tile(n, pref, mult=V7X_SUBLANES):
    t = min(n, pref)
    while n % t or t % mult:
        t -= 1
    return t


def _const_spec(shape, index_map):
    return pl.BlockSpec(shape, index_map, pipeline_mode=pl.Buffered(1))


def _rmsnorm_rows(x, g):
    ms = jnp.mean(x * x, axis=-1, keepdims=True)
    return x * lax.rsqrt(ms + EPS) * g


def _rope_tables_kernel(pos_ref, fr_ref, fm_ref, cos_ref, sin_ref, m_ref):
    pos = pos_ref[...].astype(F32)
    ang = pos * fr_ref[...]
    cos_ref[...] = jnp.cos(ang)
    sin_ref[...] = jnp.sin(ang)
    angm = pos * fm_ref[...]
    lane = lax.broadcasted_iota(jnp.int32, angm.shape, 1)
    m_ref[...] = jnp.where(lane < MLA_ROPE, jnp.cos(angm), jnp.sin(angm))


def _rope_tables(positions):
    T = positions.size
    tm = _tile(T, 1024)
    half_r = RET_DK // 2
    half_m = MLA_ROPE // 2
    fr = (ROPE_THETA ** (-jnp.arange(half_r, dtype=F32) / half_r))[None, :]
    fm1 = ROPE_THETA ** (-jnp.arange(half_m, dtype=F32) / half_m)
    fm = jnp.tile(fm1, 4)[None, :]
    tab = jax.ShapeDtypeStruct((T, V7X_LANES), F32)
    row = pl.BlockSpec((tm, V7X_LANES), lambda i: (i, 0))
    frq = pl.BlockSpec((1, V7X_LANES), lambda i: (0, 0))
    return pl.pallas_call(
        _rope_tables_kernel,
        grid=(T // tm,),
        in_specs=[pl.BlockSpec((tm, 1), lambda i: (i, 0)), frq, frq],
        out_specs=[row, row, row],
        out_shape=[tab, tab, tab],
        compiler_params=_params(("parallel",), ((tm, V7X_LANES), F32, 16)),
        name="rope_tables",
    )(positions.reshape(T, 1), fr, fm)


def _init_norm_kernel(x_ref, g_ref, o_ref):
    o_ref[...] = _rmsnorm_rows(x_ref[...], g_ref[...]).astype(o_ref.dtype)


def _init_norm(x, g):
    T, D = x.shape
    tm = _tile(T, 512)
    return pl.pallas_call(
        _init_norm_kernel,
        grid=(T // tm,),
        in_specs=[pl.BlockSpec((tm, D), lambda i: (i, 0)), pl.BlockSpec((1, D), lambda i: (0, 0))],
        out_specs=pl.BlockSpec((tm, D), lambda i: (i, 0)),
        out_shape=jax.ShapeDtypeStruct((T, D), BF16),
        compiler_params=_params(("parallel",), ((tm, D), F32, 4), ((tm, D), BF16, 2)),
        name="init_norm",
    )(x, g.reshape(1, D))


def _mm_kernel(x_ref, w_ref, o_ref):
    o_ref[...] = jnp.dot(x_ref[...], w_ref[...], preferred_element_type=F32).astype(o_ref.dtype)


def _matmul(x, w_all, layer, tn):
    T, K = x.shape
    N = w_all.shape[2]
    tm = _tile(T, 1024)
    return pl.pallas_call(
        _mm_kernel,
        grid=(T // tm, N // tn),
        in_specs=[pl.BlockSpec((tm, K), lambda i, j: (i, 0)),
                  pl.BlockSpec((None, K, tn), lambda i, j: (layer, 0, j))],
        out_specs=pl.BlockSpec((tm, tn), lambda i, j: (i, j)),
        out_shape=jax.ShapeDtypeStruct((T, N), BF16),
        compiler_params=_params(("parallel", "arbitrary"), ((tm, K), BF16, 2), ((K, tn), BF16, 2),
                                ((tm, tn), BF16, 2), ((tm, tn), F32, 1)),
        name="proj",
    )(x, w_all)


def _ret_in_kernel(x_ref, w_ref, cos_ref, sin_ref, o_ref, *, sub):
    j = pl.program_id(1)
    tm = x_ref.shape[0]
    half = RET_DK // 2
    v_tile = 2
    gate_tile = 4

    def blocks(finish):
        for r0 in range(0, tm, sub):
            rows = slice(r0, r0 + sub)
            finish(jnp.dot(x_ref[rows, :], w_ref[...], preferred_element_type=F32), rows)

    def rotary(scale):
        def finish(a, rows):
            cs = cos_ref[rows, :]
            sn = sin_ref[rows, :]
            if scale != 1.0:
                cs, sn = cs * scale, sn * scale
            for h in range(RET_HEADS):
                c0 = h * RET_DK
                x1, x2 = a[:, c0:c0 + half], a[:, c0 + half:c0 + RET_DK]
                o_ref[rows, c0:c0 + half] = (x1 * cs - x2 * sn).astype(o_ref.dtype)
                o_ref[rows, c0 + half:c0 + RET_DK] = (x2 * cs + x1 * sn).astype(o_ref.dtype)
        return finish

    def plain(a, rows):
        o_ref[rows, :] = a.astype(o_ref.dtype)

    def gate(a, rows):
        o_ref[rows, :] = jax.nn.silu(a).astype(o_ref.dtype)

    @pl.when(j == 0)
    def _():
        blocks(rotary(1.0))

    @pl.when(j == 1)
    def _():
        blocks(rotary(RET_DK ** -0.5))

    @pl.when(jnp.logical_and(j >= v_tile, j < gate_tile))
    def _():
        blocks(plain)

    @pl.when(j >= gate_tile)
    def _():
        blocks(gate)


def _ret_in_proj(x, w_all, layer, cos_t, sin_t):
    T, K = x.shape
    tn = RET_QK_WIDTH
    tm = _tile(T, 1024)
    half = RET_DK // 2
    return pl.pallas_call(
        functools.partial(_ret_in_kernel, sub=_tile(tm, 256)),
        grid=(T // tm, RET_IN_WIDTH // tn),
        in_specs=[pl.BlockSpec((tm, K), lambda i, j: (i, 0)),
                  pl.BlockSpec((None, K, tn), lambda i, j: (layer, 0, j)),
                  pl.BlockSpec((tm, half), lambda i, j: (i, 0)),
                  pl.BlockSpec((tm, half), lambda i, j: (i, 0))],
        out_specs=pl.BlockSpec((tm, tn), lambda i, j: (i, j)),
        out_shape=jax.ShapeDtypeStruct((T, RET_IN_WIDTH), BF16),
        compiler_params=_params(("parallel", "arbitrary"), ((tm, K), BF16, 2), ((K, tn), BF16, 2),
                                ((tm, tn), BF16, 2), ((tm, half), F32, 4), ((tm // 4, tn), F32, 4)),
        name="ret_in_proj",
    )(x, w_all, cos_t, sin_t)


def _mla_prep_kernel(cq_ref, ckv_ref, kpe_ref, m_ref, qg_ref, kvg_ref, wq_ref, wkv_ref,
                     q_ref, k_ref, v_ref):
    scale = (MLA_NOPE + MLA_ROPE) ** -0.5
    m = m_ref[...]
    cqn = _rmsnorm_rows(cq_ref[...].astype(F32), qg_ref[...]).astype(BF16)
    q = jnp.dot(cqn, wq_ref[...], preferred_element_type=F32)
    for h in range(MLA_HEADS):
        c0 = h * MLA_QK_PAD
        q_ref[:, c0:c0 + MLA_NOPE] = (q[:, c0:c0 + MLA_NOPE] * scale).astype(BF16)
        y = q[:, c0 + MLA_NOPE:c0 + MLA_QK_PAD] * m
        y = y + pltpu.roll(y, MLA_ROPE, axis=1)
        q_ref[:, c0 + MLA_NOPE:c0 + MLA_QK_PAD] = (y * scale).astype(BF16)
    ckvn = _rmsnorm_rows(ckv_ref[...].astype(F32), kvg_ref[...]).astype(BF16)
    kv = jnp.dot(ckvn, wkv_ref[...], preferred_element_type=F32)
    yk = kpe_ref[...].astype(F32) * m
    yk = yk + pltpu.roll(yk, MLA_ROPE, axis=1)
    lane = lax.broadcasted_iota(jnp.int32, yk.shape, 1)
    k_rot = jnp.where(lane < MLA_ROPE, yk, 0.0).astype(BF16)
    for h in range(MLA_HEADS):
        c0 = h * MLA_QK_PAD
        k_ref[:, c0:c0 + MLA_NOPE] = kv[:, h * MLA_NOPE:(h + 1) * MLA_NOPE].astype(BF16)
        k_ref[:, c0 + MLA_NOPE:c0 + MLA_QK_PAD] = k_rot
    v_ref[...] = kv[:, MLA_HEADS * MLA_NOPE:].astype(BF16)


def _mla_prep(z, m_tab, q_norm_g, kv_norm_g, wq, wkv):
    T = z.shape[0]
    tm = _tile(T, 512)
    qkw = MLA_HEADS * MLA_QK_PAD
    u_v = 2 * SGU_WIDTH
    return pl.pallas_call(
        _mla_prep_kernel,
        grid=(T // tm,),
        in_specs=[
            pl.BlockSpec((tm, Q_LORA), lambda i: (i, u_v // Q_LORA)),
            pl.BlockSpec((tm, KV_LORA), lambda i: (i, (u_v + Q_LORA) // KV_LORA)),
            pl.BlockSpec((tm, 2 * MLA_ROPE), lambda i: (i, (u_v + Q_LORA + KV_LORA) // (2 * MLA_ROPE))),
            pl.BlockSpec((tm, V7X_LANES), lambda i: (i, 0)),
            pl.BlockSpec((1, Q_LORA), lambda i: (0, 0)),
            pl.BlockSpec((1, KV_LORA), lambda i: (0, 0)),
            _const_spec((Q_LORA, qkw), lambda i: (0, 0)),
            _const_spec((KV_LORA, qkw), lambda i: (0, 0)),
        ],
        out_specs=[pl.BlockSpec((tm, qkw), lambda i: (i, 0)),
                   pl.BlockSpec((tm, qkw), lambda i: (i, 0)),
                   pl.BlockSpec((tm, MLA_OUT), lambda i: (i, 0))],
        out_shape=[jax.ShapeDtypeStruct((T, qkw), BF16), jax.ShapeDtypeStruct((T, qkw), BF16),
                   jax.ShapeDtypeStruct((T, MLA_OUT), BF16)],
        compiler_params=_params(("parallel",), ((tm, qkw), BF16, 8), ((tm, qkw), F32, 3),
                                ((Q_LORA + KV_LORA, qkw), BF16, 1)),
        name="mla_prep",
    )(z, z, z, m_tab, q_norm_g.reshape(1, -1), kv_norm_g.reshape(1, -1), wq, wkv)


def _mla_attn_kernel(q_ref, k_ref, v_ref, o_ref, *, tq):
    S = q_ref.shape[0]
    r = lax.broadcasted_iota(jnp.int32, (tq, tq), 0) // CHUNK
    c = lax.broadcasted_iota(jnp.int32, (tq, tq), 1) // CHUNK
    bias = jnp.where(c <= r, 0.0, -1e30).astype(F32)
    for qi in range(S // tq):
        past = qi * tq
        q = q_ref[past:past + tq, :]
        s = lax.dot_general(q, k_ref[0:past + tq, :], (((1,), (1,)), ((), ())),
                            preferred_element_type=F32)
        s_diag = s[:, past:] + bias
        s = s_diag if qi == 0 else jnp.concatenate([s[:, :past], s_diag], axis=1)
        mx = jnp.max(s, axis=-1, keepdims=True)
        p = jnp.exp(s - mx)
        l = jnp.sum(p, axis=-1, keepdims=True)
        o = jnp.dot(p.astype(BF16), v_ref[0:past + tq, :], preferred_element_type=F32)
        o_ref[past:past + tq, :] = (o / l).astype(o_ref.dtype)


def _mla_attn(q, k, v, B, S):
    tq = _tile(S, 256, CHUNK)
    q3 = q.reshape(B, S, -1)
    k3 = k.reshape(B, S, -1)
    v3 = v.reshape(B, S, -1)
    out = pl.pallas_call(
        functools.partial(_mla_attn_kernel, tq=tq),
        grid=(B, MLA_HEADS),
        in_specs=[pl.BlockSpec((None, S, MLA_QK_PAD), lambda b, h: (b, 0, h)),
                  pl.BlockSpec((None, S, MLA_QK_PAD), lambda b, h: (b, 0, h)),
                  pl.BlockSpec((None, S, MLA_VDIM), lambda b, h: (b, 0, h))],
        out_specs=pl.BlockSpec((None, S, MLA_VDIM), lambda b, h: (b, 0, h)),
        out_shape=jax.ShapeDtypeStruct((B, S, MLA_OUT), BF16),
        compiler_params=_params(("parallel", "parallel"), ((S, MLA_QK_PAD), BF16, 4),
                                ((S, MLA_VDIM), BF16, 4), ((tq, S), F32, 4)),
        name="mla_attn",
    )(q3, k3, v3)
    return out.reshape(B * S, MLA_OUT)


def _even_out_kernel(a_ref, u_ref, v_ref, h_ref, lng_ref, lnb_ref, ws_ref, bs_ref, wo_ref, g_ref,
                     o_ref, on_ref, wm_ref, bo_ref, acc_ref, *, sub):
    tm = a_ref.shape[0]
    pr = lax.broadcasted_iota(jnp.int32, (SGU_BLOCK, SGU_BLOCK), 0) // CHUNK
    pc = lax.broadcasted_iota(jnp.int32, (SGU_BLOCK, SGU_BLOCK), 1) // CHUNK
    for g in range(SGU_GROUPS):
        wm_ref[g] = jnp.where(pr >= pc, ws_ref[g], 0.0).astype(BF16)
    for r0 in range(0, tm, sub):
        rows = slice(r0, r0 + sub)
        gu = jax.nn.gelu(u_ref[rows, :].astype(F32))
        gv = jax.nn.gelu(v_ref[rows, :].astype(F32))
        mu = jnp.mean(gv, axis=-1, keepdims=True)
        xc = gv - mu
        y = xc * lax.rsqrt(jnp.mean(xc * xc, axis=-1, keepdims=True) + EPS)
        vn = (y * lng_ref[...] + lnb_ref[...]).astype(BF16)
        acc_ref[rows, :] = jnp.dot(a_ref[rows, :], wo_ref[0:MLA_OUT, :], preferred_element_type=F32)
        for g in range(SGU_GROUPS):
            bcol = bs_ref[:, g:g + 1]
            cols = slice(g * SGU_CH, (g + 1) * SGU_CH)
            for nb in range(sub // SGU_BLOCK):
                rr = slice(nb * SGU_BLOCK, (nb + 1) * SGU_BLOCK)
                s = jnp.dot(wm_ref[g], vn[rr, cols], preferred_element_type=F32) + bcol
                bo_ref[r0 + nb * SGU_BLOCK:r0 + (nb + 1) * SGU_BLOCK, cols] = (gu[rr, cols] * s).astype(BF16)
        acc = acc_ref[rows, :] + jnp.dot(bo_ref[rows, :], wo_ref[MLA_OUT:, :], preferred_element_type=F32)
        hnew = h_ref[rows, :] + acc
        o_ref[rows, :] = hnew
        on_ref[rows, :] = _rmsnorm_rows(hnew, g_ref[...]).astype(on_ref.dtype)


def _even_out(a, z, h, ln_g, ln_b, w_s, b_s, wo_all, layer, g_next):
    T, D = h.shape
    tm = _tile(T, 512, SGU_BLOCK)
    sub = _tile(tm, 256, SGU_BLOCK)
    row = lambda i: (i, 0)
    fix = lambda i: (0, 0)
    return pl.pallas_call(
        functools.partial(_even_out_kernel, sub=sub),
        grid=(T // tm,),
        in_specs=[pl.BlockSpec((tm, MLA_OUT), row),
                  pl.BlockSpec((tm, SGU_WIDTH), lambda i: (i, 0)),
                  pl.BlockSpec((tm, SGU_WIDTH), lambda i: (i, 1)),
                  pl.BlockSpec((tm, D), row),
                  pl.BlockSpec((1, SGU_WIDTH), fix),
                  pl.BlockSpec((1, SGU_WIDTH), fix),
                  _const_spec((SGU_GROUPS, SGU_BLOCK, SGU_BLOCK), lambda i: (0, 0, 0)),
                  pl.BlockSpec((SGU_BLOCK, SGU_GROUPS), fix),
                  _const_spec((None, MLA_OUT + SGU_WIDTH, D), lambda i: (layer, 0, 0)),
                  pl.BlockSpec((1, D), fix)],
        out_specs=[pl.BlockSpec((tm, D), row), pl.BlockSpec((tm, D), row)],
        out_shape=[jax.ShapeDtypeStruct((T, D), F32), jax.ShapeDtypeStruct((T, D), BF16)],
        scratch_shapes=[pltpu.VMEM((SGU_GROUPS, SGU_BLOCK, SGU_BLOCK), BF16), pltpu.VMEM((tm, SGU_WIDTH), BF16),
                        pltpu.VMEM((tm, D), F32)],
        compiler_params=_params(("parallel",), ((tm, D), F32, 7), ((tm, D), BF16, 6),
                                ((MLA_OUT + SGU_WIDTH, D), BF16, 1), ((tm, SGU_WIDTH), F32, 4)),
        name="even_out",
    )(a, z, z, h, ln_g.reshape(1, -1), ln_b.reshape(1, -1), w_s, b_s.T, wo_all, g_next.reshape(1, D))


def _retention_kernel(q_ref, k_ref, v_ref, gate_ref, dm_ref, qd_ref, kd_ref, sd_ref,
                      gng_ref, gnb_ref, o_ref, state_ref, *, L):
    S = q_ref.shape[0]
    state_ref[...] = jnp.zeros_like(state_ref)
    dm = dm_ref[...]
    qd = qd_ref[...]
    kd = kd_ref[...]
    sd = sd_ref[...]
    gng = gng_ref[...]
    gnb = gnb_ref[...]

    def step(c):
        rows = slice(c * L, (c + 1) * L)
        qb = q_ref[rows, :]
        kb = k_ref[rows, :]
        vb = v_ref[rows, :]
        sc = lax.dot_general(qb, kb, (((1,), (1,)), ((), ())), preferred_element_type=F32) * dm
        intra = jnp.dot(sc.astype(BF16), vb, preferred_element_type=F32)
        st = state_ref[...]
        cross = jnp.dot(qb, st.astype(BF16), preferred_element_type=F32) * qd
        kz = (kb.astype(F32) * kd).astype(BF16)
        state_ref[...] = st * sd + lax.dot_general(kz, vb, (((0,), (0,)), ((), ())),
                                                   preferred_element_type=F32)
        out = intra + cross
        mu = jnp.mean(out, axis=-1, keepdims=True)
        xc = out - mu
        y = xc * lax.rsqrt(jnp.mean(xc * xc, axis=-1, keepdims=True) + EPS)
        y = y * gng + gnb
        o_ref[rows, :] = (y * gate_ref[rows, :].astype(F32)).astype(o_ref.dtype)

    for c in range(S // L):
        step(c)


def _retention_tables(L):
    log_g = jnp.log1p(-(2.0 ** (-5.0 - jnp.arange(RET_HEADS, dtype=F32))))
    idx = jnp.arange(L, dtype=F32)
    chunk = jnp.arange(L) // CHUNK
    decay = jnp.exp(log_g[:, None, None] * jnp.abs(idx[:, None] - idx[None, :]))
    dm = jnp.where(chunk[None, :, None] >= chunk[None, None, :], decay, 0.0)
    qd = jnp.exp(log_g[:, None] * (idx + 1.0)[None, :])[:, :, None]
    kd = jnp.exp(log_g[:, None] * (L - 1 - idx)[None, :])[:, :, None]
    sd = jnp.broadcast_to(jnp.exp(log_g * L)[:, None, None], (RET_HEADS, 1, RET_DV))
    return dm, qd, kd, sd


def _retention(z, gn_g, gn_b, B, S):
    L = _tile(S, RET_SUPER, CHUNK)
    z3 = z.reshape(B, S, RET_IN_WIDTH)
    dm, qd, kd, sd = _retention_tables(L)
    qk_blocks = RET_QK_WIDTH // RET_DK
    v_first = 2 * RET_QK_WIDTH // RET_DV
    g_first = v_first + RET_V_WIDTH // RET_DV
    out = pl.pallas_call(
        functools.partial(_retention_kernel, L=L),
        grid=(B, RET_HEADS),
        in_specs=[pl.BlockSpec((None, S, RET_DK), lambda b, h: (b, 0, h)),
                  pl.BlockSpec((None, S, RET_DK), lambda b, h: (b, 0, qk_blocks + h)),
                  pl.BlockSpec((None, S, RET_DV), lambda b, h: (b, 0, v_first + h)),
                  pl.BlockSpec((None, S, RET_DV), lambda b, h: (b, 0, g_first + h)),
                  pl.BlockSpec((None, L, L), lambda b, h: (h, 0, 0)),
                  pl.BlockSpec((None, L, 1), lambda b, h: (h, 0, 0)),
                  pl.BlockSpec((None, L, 1), lambda b, h: (h, 0, 0)),
                  pl.BlockSpec((None, 1, RET_DV), lambda b, h: (h, 0, 0)),
                  pl.BlockSpec((1, RET_DV), lambda b, h: (0, h)),
                  pl.BlockSpec((1, RET_DV), lambda b, h: (0, h))],
        out_specs=pl.BlockSpec((None, S, RET_DV), lambda b, h: (b, 0, h)),
        out_shape=jax.ShapeDtypeStruct((B, S, RET_V_WIDTH), BF16),
        scratch_shapes=[pltpu.VMEM((RET_DK, RET_DV), F32)],
        compiler_params=_params(("parallel", "arbitrary"), ((S, RET_DK), BF16, 4), ((S, RET_DV), BF16, 6),
                                ((L, L), F32, 4), ((L, V7X_LANES), F32, 4),
                                ((RET_DK, RET_DV), F32, 4), ((L, RET_DV), F32, 8)),
        name="retention",
    )(z3, z3, z3, z3, dm, qd, kd, sd, gn_g.reshape(1, -1), gn_b.reshape(1, -1))
    return out.reshape(B * S, RET_V_WIDTH)


def _res_mm_kernel(x_ref, w_ref, h_ref, g_ref, o_ref, on_ref, *, sub):
    for r0 in range(0, x_ref.shape[0], sub):
        rows = slice(r0, r0 + sub)
        hnew = h_ref[rows, :] + jnp.dot(x_ref[rows, :], w_ref[...], preferred_element_type=F32)
        o_ref[rows, :] = hnew
        on_ref[rows, :] = _rmsnorm_rows(hnew, g_ref[...]).astype(on_ref.dtype)


def _res_matmul(x, w_all, layer, h, g_next, tm_pref):
    T, K = x.shape
    D = h.shape[1]
    tm = _tile(T, tm_pref)
    row = lambda i: (i, 0)
    fix = lambda i: (0, 0)
    return pl.pallas_call(
        functools.partial(_res_mm_kernel, sub=_tile(tm, 256)),
        grid=(T // tm,),
        in_specs=[pl.BlockSpec((tm, K), row), _const_spec((None, K, D), lambda i: (layer, 0, 0)),
                  pl.BlockSpec((tm, D), row), pl.BlockSpec((1, D), fix)],
        out_specs=[pl.BlockSpec((tm, D), row), pl.BlockSpec((tm, D), row)],
        out_shape=[jax.ShapeDtypeStruct((T, D), F32), jax.ShapeDtypeStruct((T, D), BF16)],
        compiler_params=_params(("parallel",), ((tm, K), BF16, 2), ((K, D), BF16, 1),
                                ((tm, D), F32, 6), ((tm, D), BF16, 2)),
        name="res_proj",
    )(x, w_all, h, g_next.reshape(1, D))


def _ffn_up_kernel(x_ref, wg_ref, wv_ref, cwg_ref, cwv_ref, cbg_ref, cbv_ref, o_ref, wb_ref, e_ref,
                   *, tiles_per_seq, sub):
    tm = x_ref.shape[0]
    tn = o_ref.shape[1]
    halo = V7X_SUBLANES
    lanes = V7X_LANES
    nslab = tn // lanes
    i = pl.program_id(1)
    first = (i % tiles_per_seq) == 0

    @pl.when(i == 0)
    def _():
        wb_ref[:, 0:tn] = wg_ref[...].astype(BF16)
        wb_ref[:, tn:2 * tn] = wv_ref[...].astype(BF16)

    @pl.when(first)
    def _():
        e_ref[:, 0:halo, :] = jnp.zeros((e_ref.shape[0], halo, lanes), F32)

    @pl.when(jnp.logical_not(first))
    def _():
        e_ref[:, 0:halo, :] = e_ref[:, tm:tm + halo, :]

    def conv(k, r0, n, cw_ref, cb_ref, kc):
        cols = slice(kc * lanes, (kc + 1) * lanes)
        c = cb_ref[:, cols]
        for j in range(CONV_WIDTH):
            off = halo + r0 - (CONV_WIDTH - 1) + j
            c = c + e_ref[k, off:off + n, :] * cw_ref[j:j + 1, cols]
        return c

    n = sub
    for r0 in range(0, tm, n):
        a = jnp.dot(x_ref[r0:r0 + n, :], wb_ref[...], preferred_element_type=F32)
        for k in range(2 * nslab):
            e_ref[k, halo + r0:halo + r0 + n, :] = a[:, k * lanes:(k + 1) * lanes]
        for k in range(nslab):
            gate = conv(k, r0, n, cwg_ref, cbg_ref, k)
            val = conv(k + nslab, r0, n, cwv_ref, cbv_ref, k)
            o_ref[r0:r0 + n, k * lanes:(k + 1) * lanes] = (jax.nn.gelu(gate) * val).astype(o_ref.dtype)


def _ffn_up(hn, w_up_all, conv_w_all, conv_b_all, layer, S):
    T, D = hn.shape
    tn = 512
    tm = _tile(S, 2048)
    sub = _tile(tm, 512)
    nj = D_FF // tn
    halo = V7X_SUBLANES
    conv_b3 = conv_b_all.reshape(conv_b_all.shape[0], 1, 2 * D_FF)
    gate_col = lambda j, i: (layer, 0, j)
    val_col = lambda j, i: (layer, 0, nj + j)
    return pl.pallas_call(
        functools.partial(_ffn_up_kernel, tiles_per_seq=S // tm, sub=sub),
        grid=(nj, T // tm),
        in_specs=[pl.BlockSpec((tm, D), lambda j, i: (i, 0)),
                  pl.BlockSpec((None, D, tn), gate_col),
                  pl.BlockSpec((None, D, tn), val_col),
                  pl.BlockSpec((None, CONV_WIDTH, tn), gate_col),
                  pl.BlockSpec((None, CONV_WIDTH, tn), val_col),
                  pl.BlockSpec((None, 1, tn), gate_col),
                  pl.BlockSpec((None, 1, tn), val_col)],
        out_specs=pl.BlockSpec((tm, tn), lambda j, i: (i, j)),
        out_shape=jax.ShapeDtypeStruct((T, D_FF), BF16),
        scratch_shapes=[pltpu.VMEM((D, 2 * tn), BF16),
                        pltpu.VMEM((2 * tn // V7X_LANES, tm + halo, V7X_LANES), F32)],
        compiler_params=_params(("arbitrary", "arbitrary"), ((tm, D), BF16, 2), ((D, tn), F32, 4),
                                ((D, 2 * tn), BF16, 1), ((tm + halo, 2 * tn), F32, 1),
                                ((sub, 2 * tn), F32, 4), ((tm, tn), BF16, 2)),
        name="ffn_up",
    )(hn, w_up_all, w_up_all, conv_w_all, conv_w_all, conv_b3, conv_b3)


def _ple_kernel(hn_ref, wg_ref, p_ref, wu_ref, h_ref, g_ref, *out_refs, sub):
    on_ref = out_refs[-1]
    for r0 in range(0, hn_ref.shape[0], sub):
        rows = slice(r0, r0 + sub)
        gate = jax.nn.sigmoid(jnp.dot(hn_ref[rows, :], wg_ref[...], preferred_element_type=F32))
        up = jnp.dot(p_ref[rows, :].astype(BF16), wu_ref[...], preferred_element_type=F32)
        hnew = h_ref[rows, :] + up * gate
        if len(out_refs) == 2:
            out_refs[0][rows, :] = hnew
        on_ref[rows, :] = _rmsnorm_rows(hnew, g_ref[...]).astype(on_ref.dtype)


def _ple(hn, wg_all, p_all, layer, wu_all, h, g_next, last):
    T, D = h.shape
    tm = _tile(T, 512)
    row = lambda i: (i, 0)
    fix = lambda i: (0, 0)
    out_specs = [pl.BlockSpec((tm, D), row), pl.BlockSpec((tm, D), row)]
    out_shape = [jax.ShapeDtypeStruct((T, D), F32), jax.ShapeDtypeStruct((T, D), BF16)]
    if last:
        out_specs, out_shape = out_specs[:1], out_shape[:1]
    return pl.pallas_call(
        functools.partial(_ple_kernel, sub=_tile(tm, 256)),
        grid=(T // tm,),
        in_specs=[pl.BlockSpec((tm, D), row),
                  _const_spec((None, D, D), lambda i: (layer, 0, 0)),
                  pl.BlockSpec((None, tm, PLE_DIM), lambda i: (layer, i, 0)),
                  _const_spec((None, PLE_DIM, D), lambda i: (layer, 0, 0)),
                  pl.BlockSpec((tm, D), row),
                  pl.BlockSpec((1, D), fix)],
        out_specs=out_specs,
        out_shape=out_shape,
        compiler_params=_params(("parallel",), ((tm, D), BF16, 2), ((D, D), BF16, 1), ((PLE_DIM, D), BF16, 1),
                                ((tm, PLE_DIM), F32, 2), ((tm, D), F32, 8)),
        name="ple",
    )(hn, wg_all, p_all, wu_all, h, g_next.reshape(1, D))


def _rot_cols(w):
    half = MLA_ROPE // 2
    return jnp.concatenate([-w[..., half:], w[..., :half]], axis=-1)


def _even_in_weight(w):
    cq = w[..., :Q_LORA]
    ckv = w[..., Q_LORA:Q_LORA + KV_LORA]
    kpe = w[..., Q_LORA + KV_LORA:Q_LORA + KV_LORA + MLA_ROPE]
    uv = w[..., Q_LORA + KV_LORA + MLA_ROPE:]
    used = 2 * SGU_WIDTH + Q_LORA + KV_LORA + 2 * MLA_ROPE
    pad = jnp.zeros(w.shape[:-1] + (EVEN_IN_PAD - used,), w.dtype)
    return jnp.concatenate([uv, cq, ckv, kpe, _rot_cols(kpe), pad], axis=-1).astype(BF16)


def _q_up_weight(w):
    w = w.reshape(Q_LORA, MLA_HEADS, MLA_NOPE + MLA_ROPE)
    rope = w[..., MLA_NOPE:]
    w = jnp.concatenate([w[..., :MLA_NOPE], rope, _rot_cols(rope)], axis=-1)
    return w.reshape(Q_LORA, MLA_HEADS * MLA_QK_PAD).astype(BF16)


def _kv_up_weight(w):
    w = w.reshape(KV_LORA, MLA_HEADS, MLA_NOPE + MLA_VDIM)
    k = w[..., :MLA_NOPE].reshape(KV_LORA, -1)
    v = w[..., MLA_NOPE:].reshape(KV_LORA, -1)
    return jnp.concatenate([k, v], axis=1).astype(BF16)


def kernel(x, p, positions, mix_norm_g, even_w_in, mla_q_norm_g, mla_w_q_up, mla_kv_norm_g, mla_w_kv_up,
           sgu_ln_g, sgu_ln_b, sgu_w_s, sgu_b_s, even_w_out, ret_w_in, ret_gn_g, ret_gn_b, ret_w_out,
           ffn_norm_g, ffn_w_up, ffn_conv_w, ffn_conv_b, ffn_w_down, ple_norm_g, ple_w_gate, ple_w_up,
           final_norm_g):
    B, S, D = x.shape
    depth = p.shape[0]
    T = B * S
    assert D == D_MODEL and S % SGU_BLOCK == 0
    cos_t, sin_t, m_tab = _rope_tables(positions)
    h = x.reshape(T, D)
    p_all = p.reshape(depth, T, PLE_DIM)
    hn = _init_norm(h, mix_norm_g[0])
    even_in_b = _even_in_weight(even_w_in)
    even_out_b = even_w_out.astype(BF16)
    ret_in_b = ret_w_in.astype(BF16)
    ret_out_b = ret_w_out.astype(BF16)
    ffn_down_b = ffn_w_down.astype(BF16)
    ple_gate_b = ple_w_gate.astype(BF16)
    ple_up_b = ple_w_up.astype(BF16)
    for i in range(depth):
        j = i // 2
        if i % 2 == 0:
            z = _matmul(hn, even_in_b, j, 1024)
            q, k, v = _mla_prep(z, m_tab, mla_q_norm_g[j], mla_kv_norm_g[j],
                                _q_up_weight(mla_w_q_up[j]), _kv_up_weight(mla_w_kv_up[j]))
            a = _mla_attn(q, k, v, B, S)
            h, hn = _even_out(a, z, h, sgu_ln_g[j], sgu_ln_b[j], sgu_w_s[j], sgu_b_s[j],
                              even_out_b, j, ffn_norm_g[i])
        else:
            z = _ret_in_proj(hn, ret_in_b, j, cos_t, sin_t)
            r = _retention(z, ret_gn_g[j], ret_gn_b[j], B, S)
            h, hn = _res_matmul(r, ret_out_b, j, h, ffn_norm_g[i], 512)
        act = _ffn_up(hn, ffn_w_up, ffn_conv_w, ffn_conv_b, i, S)
        h, hn = _res_matmul(act, ffn_down_b, i, h, ple_norm_g[i], 256)
        if i < depth - 1:
            h, hn = _ple(hn, ple_gate_b, p_all, i, ple_up_b, h, mix_norm_g[i + 1], False)
        else:
            out, = _ple(hn, ple_gate_b, p_all, i, ple_up_b, h, final_norm_g, True)
    return out.reshape(B, S, D)
```

```python
import functools
import math

import jax
import jax.numpy as jnp
from jax import lax
from jax.experimental import pallas as pl
from jax.experimental.pallas import tpu as pltpu

F32 = jnp.float32
BF16 = jnp.bfloat16

D_MODEL = 2048
CHUNK = 64
EPS = 1e-6
ROPE_THETA = 10000.0
MLA_HEADS = 8
MLA_NOPE = 128
MLA_ROPE = 64
MLA_VDIM = 128
Q_LORA = 512
KV_LORA = 256
MLA_OUT = MLA_HEADS * MLA_VDIM
MLA_QK_PAD = 256
SGU_BLOCK = 128
SGU_GROUPS = 8
SGU_CH = 128
SGU_WIDTH = SGU_GROUPS * SGU_CH
EVEN_IN_PAD = 3072
RET_HEADS = 8
RET_DK = 256
RET_DV = 512
RET_QK_WIDTH = RET_HEADS * RET_DK
RET_V_WIDTH = RET_HEADS * RET_DV
RET_IN_WIDTH = 2 * RET_QK_WIDTH + 2 * RET_V_WIDTH
RET_SUPER = 256
D_FF = 5632
CONV_WIDTH = 3
PLE_DIM = 256

V7X_LANES = 128
V7X_SUBLANES = 8
V7X_VMEM_BUDGET = 58 * 1024 * 1024


def _nbytes(shape, dtype):
    return math.prod(shape) * jnp.dtype(dtype).itemsize


def _params(semantics, *buffers):
    need = sum(_nbytes(s, d) * n for s, d, n in buffers)
    return pltpu.CompilerParams(dimension_semantics=semantics,
                                vmem_limit_bytes=min(V7X_VMEM_BUDGET, need + (4 << 20)))


def _tile(n, pref, mult=V7X_SUBLANES):
    t = min(n, pref)
    while n % t or t % mult:
        t -= 1
    return t


def _const_spec(shape, index_map):
    return pl.BlockSpec(shape, index_map, pipeline_mode=pl.Buffered(1))


def _rmsnorm_rows(x, g):
    ms = jnp.mean(x * x, axis=-1, keepdims=True)
    return x * lax.rsqrt(ms + EPS) * g


def _rope_tables_kernel(pos_ref, fr_ref, fm_ref, cos_ref, sin_ref, m_ref):
    pos = pos_ref[...].astype(F32)
    ang = pos * fr_ref[...]
    cos_ref[...] = jnp.cos(ang)
    sin_ref[...] = jnp.sin(ang)
    angm = pos * fm_ref[...]
    lane = lax.broadcasted_iota(jnp.int32, angm.shape, 1)
    m_ref[...] = jnp.where(lane < MLA_ROPE, jnp.cos(angm), jnp.sin(angm))


def _rope_tables(positions):
    T = positions.size
    tm = _tile(T, 1024)
    half_r = RET_DK // 2
    half_m = MLA_ROPE // 2
    fr = (ROPE_THETA ** (-jnp.arange(half_r, dtype=F32) / half_r))[None, :]
    fm1 = ROPE_THETA ** (-jnp.arange(half_m, dtype=F32) / half_m)
    fm = jnp.tile(fm1, 4)[None, :]
    tab = jax.ShapeDtypeStruct((T, V7X_LANES), F32)
    row = pl.BlockSpec((tm, V7X_LANES), lambda i: (i, 0))
    frq = pl.BlockSpec((1, V7X_LANES), lambda i: (0, 0))
    return pl.pallas_call(
        _rope_tables_kernel,
        grid=(T // tm,),
        in_specs=[pl.BlockSpec((tm, 1), lambda i: (i, 0)), frq, frq],
        out_specs=[row, row, row],
        out_shape=[tab, tab, tab],
        compiler_params=_params(("parallel",), ((tm, V7X_LANES), F32, 16)),
        name="rope_tables",
    )(positions.reshape(T, 1), fr, fm)


def _init_norm_kernel(x_ref, g_ref, o_ref):
    o_ref[...] = _rmsnorm_rows(x_ref[...], g_ref[...]).astype(o_ref.dtype)


def _init_norm(x, g):
    T, D = x.shape
    tm = _tile(T, 512)
    return pl.pallas_call(
        _init_norm_kernel,
        grid=(T // tm,),
        in_specs=[pl.BlockSpec((tm, D), lambda i: (i, 0)), pl.BlockSpec((1, D), lambda i: (0, 0))],
        out_specs=pl.BlockSpec((tm, D), lambda i: (i, 0)),
        out_shape=jax.ShapeDtypeStruct((T, D), BF16),
        compiler_params=_params(("parallel",), ((tm, D), F32, 4), ((tm, D), BF16, 2)),
        name="init_norm",
    )(x, g.reshape(1, D))


def _mm_kernel(x_ref, w_ref, o_ref):
    o_ref[...] = jnp.dot(x_ref[...], w_ref[...], preferred_element_type=F32).astype(o_ref.dtype)


def _matmul(x, w_all, layer, tn):
    T, K = x.shape
    N = w_all.shape[2]
    tm = _tile(T, 1024)
    return pl.pallas_call(
        _mm_kernel,
        grid=(T // tm, N // tn),
        in_specs=[pl.BlockSpec((tm, K), lambda i, j: (i, 0)),
                  pl.BlockSpec((None, K, tn), lambda i, j: (layer, 0, j))],
        out_specs=pl.BlockSpec((tm, tn), lambda i, j: (i, j)),
        out_shape=jax.ShapeDtypeStruct((T, N), BF16),
        compiler_params=_params(("parallel", "arbitrary"), ((tm, K), BF16, 2), ((K, tn), BF16, 2),
                                ((tm, tn), BF16, 2), ((tm, tn), F32, 1)),
        name="proj",
    )(x, w_all)


def _ret_in_kernel(x_ref, w_ref, cos_ref, sin_ref, o_ref, *, sub):
    j = pl.program_id(1)
    tm = x_ref.shape[0]
    half = RET_DK // 2
    v_tile = 2
    gate_tile = 4

    def blocks(finish):
        for r0 in range(0, tm, sub):
            rows = slice(r0, r0 + sub)
            finish(jnp.dot(x_ref[rows, :], w_ref[...], preferred_element_type=F32), rows)

    def rotary(scale):
        def finish(a, rows):
            cs = cos_ref[rows, :]
            sn = sin_ref[rows, :]
            if scale != 1.0:
                cs, sn = cs * scale, sn * scale
            for h in range(RET_HEADS):
                c0 = h * RET_DK
                x1, x2 = a[:, c0:c0 + half], a[:, c0 + half:c0 + RET_DK]
                o_ref[rows, c0:c0 + half] = (x1 * cs - x2 * sn).astype(o_ref.dtype)
                o_ref[rows, c0 + half:c0 + RET_DK] = (x2 * cs + x1 * sn).astype(o_ref.dtype)
        return finish

    def plain(a, rows):
        o_ref[rows, :] = a.astype(o_ref.dtype)

    def gate(a, rows):
        o_ref[rows, :] = jax.nn.silu(a).astype(o_ref.dtype)

    @pl.when(j == 0)
    def _():
        blocks(rotary(1.0))

    @pl.when(j == 1)
    def _():
        blocks(rotary(RET_DK ** -0.5))

    @pl.when(jnp.logical_and(j >= v_tile, j < gate_tile))
    def _():
        blocks(plain)

    @pl.when(j >= gate_tile)
    def _():
        blocks(gate)


def _ret_in_proj(x, w_all, layer, cos_t, sin_t, sub_pref):
    T, K = x.shape
    tn = RET_QK_WIDTH
    tm = _tile(T, 1024)
    half = RET_DK // 2
    return pl.pallas_call(
        functools.partial(_ret_in_kernel, sub=_tile(tm, sub_pref)),
        grid=(T // tm, RET_IN_WIDTH // tn),
        in_specs=[pl.BlockSpec((tm, K), lambda i, j: (i, 0)),
                  pl.BlockSpec((None, K, tn), lambda i, j: (layer, 0, j)),
                  pl.BlockSpec((tm, half), lambda i, j: (i, 0)),
                  pl.BlockSpec((tm, half), lambda i, j: (i, 0))],
        out_specs=pl.BlockSpec((tm, tn), lambda i, j: (i, j)),
        out_shape=jax.ShapeDtypeStruct((T, RET_IN_WIDTH), BF16),
        compiler_params=_params(("parallel", "arbitrary"), ((tm, K), BF16, 2), ((K, tn), BF16, 2),
                                ((tm, tn), BF16, 2), ((tm, half), F32, 4), ((tm // 4, tn), F32, 4)),
        name="ret_in_proj",
    )(x, w_all, cos_t, sin_t)


def _mla_prep_kernel(cq_ref, ckv_ref, kpe_ref, m_ref, qg_ref, kvg_ref, wq_ref, wkv_ref,
                     q_ref, k_ref, v_ref):
    scale = (MLA_NOPE + MLA_ROPE) ** -0.5
    m = m_ref[...]
    cqn = _rmsnorm_rows(cq_ref[...].astype(F32), qg_ref[...]).astype(BF16)
    q = jnp.dot(cqn, wq_ref[...], preferred_element_type=F32)
    for h in range(MLA_HEADS):
        c0 = h * MLA_QK_PAD
        q_ref[:, c0:c0 + MLA_NOPE] = (q[:, c0:c0 + MLA_NOPE] * scale).astype(BF16)
        y = q[:, c0 + MLA_NOPE:c0 + MLA_QK_PAD] * m
        y = y + pltpu.roll(y, MLA_ROPE, axis=1)
        q_ref[:, c0 + MLA_NOPE:c0 + MLA_QK_PAD] = (y * scale).astype(BF16)
    ckvn = _rmsnorm_rows(ckv_ref[...].astype(F32), kvg_ref[...]).astype(BF16)
    kv = jnp.dot(ckvn, wkv_ref[...], preferred_element_type=F32)
    yk = kpe_ref[...].astype(F32) * m
    yk = yk + pltpu.roll(yk, MLA_ROPE, axis=1)
    lane = lax.broadcasted_iota(jnp.int32, yk.shape, 1)
    k_rot = jnp.where(lane < MLA_ROPE, yk, 0.0).astype(BF16)
    for h in range(MLA_HEADS):
        c0 = h * MLA_QK_PAD
        k_ref[:, c0:c0 + MLA_NOPE] = kv[:, h * MLA_NOPE:(h + 1) * MLA_NOPE].astype(BF16)
        k_ref[:, c0 + MLA_NOPE:c0 + MLA_QK_PAD] = k_rot
    v_ref[...] = kv[:, MLA_HEADS * MLA_NOPE:].astype(BF16)


def _mla_prep(z, m_tab, q_norm_g, kv_norm_g, wq, wkv):
    T = z.shape[0]
    tm = _tile(T, 512)
    qkw = MLA_HEADS * MLA_QK_PAD
    u_v = 2 * SGU_WIDTH
    return pl.pallas_call(
        _mla_prep_kernel,
        grid=(T // tm,),
        in_specs=[
            pl.BlockSpec((tm, Q_LORA), lambda i: (i, u_v // Q_LORA)),
            pl.BlockSpec((tm, KV_LORA), lambda i: (i, (u_v + Q_LORA) // KV_LORA)),
            pl.BlockSpec((tm, 2 * MLA_ROPE), lambda i: (i, (u_v + Q_LORA + KV_LORA) // (2 * MLA_ROPE))),
            pl.BlockSpec((tm, V7X_LANES), lambda i: (i, 0)),
            pl.BlockSpec((1, Q_LORA), lambda i: (0, 0)),
            pl.BlockSpec((1, KV_LORA), lambda i: (0, 0)),
            _const_spec((Q_LORA, qkw), lambda i: (0, 0)),
            _const_spec((KV_LORA, qkw), lambda i: (0, 0)),
        ],
        out_specs=[pl.BlockSpec((tm, qkw), lambda i: (i, 0)),
                   pl.BlockSpec((tm, qkw), lambda i: (i, 0)),
                   pl.BlockSpec((tm, MLA_OUT), lambda i: (i, 0))],
        out_shape=[jax.ShapeDtypeStruct((T, qkw), BF16), jax.ShapeDtypeStruct((T, qkw), BF16),
                   jax.ShapeDtypeStruct((T, MLA_OUT), BF16)],
        compiler_params=_params(("parallel",), ((tm, qkw), BF16, 8), ((tm, qkw), F32, 3),
                                ((Q_LORA + KV_LORA, qkw), BF16, 1)),
        name="mla_prep",
    )(z, z, z, m_tab, q_norm_g.reshape(1, -1), kv_norm_g.reshape(1, -1), wq, wkv)


def _mla_attn_kernel(q_ref, k_ref, v_ref, o_ref, va_ref, *, tq, rowsum_on_mxu):
    S = q_ref.shape[0]
    r = lax.broadcasted_iota(jnp.int32, (tq, tq), 0) // CHUNK
    c = lax.broadcasted_iota(jnp.int32, (tq, tq), 1) // CHUNK
    bias = jnp.where(c <= r, 0.0, -1e30).astype(F32)
    if rowsum_on_mxu:
        va_ref[:, 0:MLA_VDIM] = v_ref[...]
        va_ref[:, MLA_VDIM:] = jnp.ones((S, MLA_VDIM), BF16)
    for qi in range(S // tq):
        past = qi * tq
        q = q_ref[past:past + tq, :]
        s = lax.dot_general(q, k_ref[0:past + tq, :], (((1,), (1,)), ((), ())),
                            preferred_element_type=F32)
        s_diag = s[:, past:] + bias
        s = s_diag if qi == 0 else jnp.concatenate([s[:, :past], s_diag], axis=1)
        mx = jnp.max(s, axis=-1, keepdims=True)
        p = jnp.exp(s - mx)
        if rowsum_on_mxu:
            oa = jnp.dot(p.astype(BF16), va_ref[0:past + tq, :], preferred_element_type=F32)
            o_ref[past:past + tq, :] = (oa[:, :MLA_VDIM] / oa[:, MLA_VDIM:]).astype(o_ref.dtype)
        else:
            l = jnp.sum(p, axis=-1, keepdims=True)
            o = jnp.dot(p.astype(BF16), v_ref[0:past + tq, :], preferred_element_type=F32)
            o_ref[past:past + tq, :] = (o / l).astype(o_ref.dtype)


def _mla_attn(q, k, v, B, S, rowsum_on_mxu):
    tq = _tile(S, 256, CHUNK)
    q3 = q.reshape(B, S, -1)
    k3 = k.reshape(B, S, -1)
    v3 = v.reshape(B, S, -1)
    out = pl.pallas_call(
        functools.partial(_mla_attn_kernel, tq=tq, rowsum_on_mxu=rowsum_on_mxu),
        grid=(B, MLA_HEADS),
        scratch_shapes=[pltpu.VMEM((S, 2 * MLA_VDIM), BF16)],
        in_specs=[pl.BlockSpec((None, S, MLA_QK_PAD), lambda b, h: (b, 0, h)),
                  pl.BlockSpec((None, S, MLA_QK_PAD), lambda b, h: (b, 0, h)),
                  pl.BlockSpec((None, S, MLA_VDIM), lambda b, h: (b, 0, h))],
        out_specs=pl.BlockSpec((None, S, MLA_VDIM), lambda b, h: (b, 0, h)),
        out_shape=jax.ShapeDtypeStruct((B, S, MLA_OUT), BF16),
        compiler_params=_params(("parallel", "parallel"), ((S, MLA_QK_PAD), BF16, 4),
                                ((S, MLA_VDIM), BF16, 4), ((tq, S), F32, 4)),
        name="mla_attn",
    )(q3, k3, v3)
    return out.reshape(B * S, MLA_OUT)


def _even_out_kernel(a_ref, u_ref, v_ref, h_ref, lng_ref, lnb_ref, ws_ref, bs_ref, wo_ref, g_ref,
                     o_ref, on_ref, wm_ref, bo_ref, acc_ref, *, sub):
    tm = a_ref.shape[0]
    pr = lax.broadcasted_iota(jnp.int32, (SGU_BLOCK, SGU_BLOCK), 0) // CHUNK
    pc = lax.broadcasted_iota(jnp.int32, (SGU_BLOCK, SGU_BLOCK), 1) // CHUNK
    for g in range(SGU_GROUPS):
        wm_ref[g] = jnp.where(pr >= pc, ws_ref[g], 0.0).astype(BF16)
    for r0 in range(0, tm, sub):
        rows = slice(r0, r0 + sub)
        gu = jax.nn.gelu(u_ref[rows, :].astype(F32))
        gv = jax.nn.gelu(v_ref[rows, :].astype(F32))
        mu = jnp.mean(gv, axis=-1, keepdims=True)
        xc = gv - mu
        y = xc * lax.rsqrt(jnp.mean(xc * xc, axis=-1, keepdims=True) + EPS)
        vn = (y * lng_ref[...] + lnb_ref[...]).astype(BF16)
        acc_ref[rows, :] = jnp.dot(a_ref[rows, :], wo_ref[0:MLA_OUT, :], preferred_element_type=F32)
        for g in range(SGU_GROUPS):
            bcol = bs_ref[:, g:g + 1]
            cols = slice(g * SGU_CH, (g + 1) * SGU_CH)
            for nb in range(sub // SGU_BLOCK):
                rr = slice(nb * SGU_BLOCK, (nb + 1) * SGU_BLOCK)
                s = jnp.dot(wm_ref[g], vn[rr, cols], preferred_element_type=F32) + bcol
                bo_ref[r0 + nb * SGU_BLOCK:r0 + (nb + 1) * SGU_BLOCK, cols] = (gu[rr, cols] * s).astype(BF16)
        acc = acc_ref[rows, :] + jnp.dot(bo_ref[rows, :], wo_ref[MLA_OUT:, :], preferred_element_type=F32)
        hnew = h_ref[rows, :] + acc
        o_ref[rows, :] = hnew
        on_ref[rows, :] = _rmsnorm_rows(hnew, g_ref[...]).astype(on_ref.dtype)


def _even_out(a, z, h, ln_g, ln_b, w_s, b_s, wo_all, layer, g_next):
    T, D = h.shape
    tm = _tile(T, 512, SGU_BLOCK)
    sub = _tile(tm, 256, SGU_BLOCK)
    row = lambda i: (i, 0)
    fix = lambda i: (0, 0)
    return pl.pallas_call(
        functools.partial(_even_out_kernel, sub=sub),
        grid=(T // tm,),
        in_specs=[pl.BlockSpec((tm, MLA_OUT), row),
                  pl.BlockSpec((tm, SGU_WIDTH), lambda i: (i, 0)),
                  pl.BlockSpec((tm, SGU_WIDTH), lambda i: (i, 1)),
                  pl.BlockSpec((tm, D), row),
                  pl.BlockSpec((1, SGU_WIDTH), fix),
                  pl.BlockSpec((1, SGU_WIDTH), fix),
                  _const_spec((SGU_GROUPS, SGU_BLOCK, SGU_BLOCK), lambda i: (0, 0, 0)),
                  pl.BlockSpec((SGU_BLOCK, SGU_GROUPS), fix),
                  _const_spec((None, MLA_OUT + SGU_WIDTH, D), lambda i: (layer, 0, 0)),
                  pl.BlockSpec((1, D), fix)],
        out_specs=[pl.BlockSpec((tm, D), row), pl.BlockSpec((tm, D), row)],
        out_shape=[jax.ShapeDtypeStruct((T, D), F32), jax.ShapeDtypeStruct((T, D), BF16)],
        scratch_shapes=[pltpu.VMEM((SGU_GROUPS, SGU_BLOCK, SGU_BLOCK), BF16), pltpu.VMEM((tm, SGU_WIDTH), BF16),
                        pltpu.VMEM((tm, D), F32)],
        compiler_params=_params(("parallel",), ((tm, D), F32, 7), ((tm, D), BF16, 6),
                                ((MLA_OUT + SGU_WIDTH, D), BF16, 1), ((tm, SGU_WIDTH), F32, 4)),
        name="even_out",
    )(a, z, z, h, ln_g.reshape(1, -1), ln_b.reshape(1, -1), w_s, b_s.T, wo_all, g_next.reshape(1, D))


def _retention_kernel(q_ref, k_ref, v_ref, gate_ref, dm_ref, qd_ref, kd_ref, sd_ref,
                      gng_ref, gnb_ref, o_ref, state_ref, *, L):
    S = q_ref.shape[0]
    state_ref[...] = jnp.zeros_like(state_ref)
    dm = dm_ref[...]
    qd = qd_ref[...]
    kd = kd_ref[...]
    sd = sd_ref[...]
    gng = gng_ref[...]
    gnb = gnb_ref[...]

    def step(c):
        rows = slice(c * L, (c + 1) * L)
        qb = q_ref[rows, :]
        kb = k_ref[rows, :]
        vb = v_ref[rows, :]
        sc = lax.dot_general(qb, kb, (((1,), (1,)), ((), ())), preferred_element_type=F32) * dm
        intra = jnp.dot(sc.astype(BF16), vb, preferred_element_type=F32)
        st = state_ref[...]
        cross = jnp.dot(qb, st.astype(BF16), preferred_element_type=F32) * qd
        kz = (kb.astype(F32) * kd).astype(BF16)
        state_ref[...] = st * sd + lax.dot_general(kz, vb, (((0,), (0,)), ((), ())),
                                                   preferred_element_type=F32)
        out = intra + cross
        mu = jnp.mean(out, axis=-1, keepdims=True)
        xc = out - mu
        y = xc * lax.rsqrt(jnp.mean(xc * xc, axis=-1, keepdims=True) + EPS)
        y = y * gng + gnb
        o_ref[rows, :] = (y * gate_ref[rows, :].astype(F32)).astype(o_ref.dtype)

    for c in range(S // L):
        step(c)


def _retention_tables(L):
    log_g = jnp.log1p(-(2.0 ** (-5.0 - jnp.arange(RET_HEADS, dtype=F32))))
    idx = jnp.arange(L, dtype=F32)
    chunk = jnp.arange(L) // CHUNK
    decay = jnp.exp(log_g[:, None, None] * jnp.abs(idx[:, None] - idx[None, :]))
    dm = jnp.where(chunk[None, :, None] >= chunk[None, None, :], decay, 0.0)
    qd = jnp.exp(log_g[:, None] * (idx + 1.0)[None, :])[:, :, None]
    kd = jnp.exp(log_g[:, None] * (L - 1 - idx)[None, :])[:, :, None]
    sd = jnp.broadcast_to(jnp.exp(log_g * L)[:, None, None], (RET_HEADS, 1, RET_DV))
    return dm, qd, kd, sd


def _retention(z, gn_g, gn_b, B, S):
    L = _tile(S, RET_SUPER, CHUNK)
    z3 = z.reshape(B, S, RET_IN_WIDTH)
    dm, qd, kd, sd = _retention_tables(L)
    qk_blocks = RET_QK_WIDTH // RET_DK
    v_first = 2 * RET_QK_WIDTH // RET_DV
    g_first = v_first + RET_V_WIDTH // RET_DV
    out = pl.pallas_call(
        functools.partial(_retention_kernel, L=L),
        grid=(B, RET_HEADS),
        in_specs=[pl.BlockSpec((None, S, RET_DK), lambda b, h: (b, 0, h)),
                  pl.BlockSpec((None, S, RET_DK), lambda b, h: (b, 0, qk_blocks + h)),
                  pl.BlockSpec((None, S, RET_DV), lambda b, h: (b, 0, v_first + h)),
                  pl.BlockSpec((None, S, RET_DV), lambda b, h: (b, 0, g_first + h)),
                  pl.BlockSpec((None, L, L), lambda b, h: (h, 0, 0)),
                  pl.BlockSpec((None, L, 1), lambda b, h: (h, 0, 0)),
                  pl.BlockSpec((None, L, 1), lambda b, h: (h, 0, 0)),
                  pl.BlockSpec((None, 1, RET_DV), lambda b, h: (h, 0, 0)),
                  pl.BlockSpec((1, RET_DV), lambda b, h: (0, h)),
                  pl.BlockSpec((1, RET_DV), lambda b, h: (0, h))],
        out_specs=pl.BlockSpec((None, S, RET_DV), lambda b, h: (b, 0, h)),
        out_shape=jax.ShapeDtypeStruct((B, S, RET_V_WIDTH), BF16),
        scratch_shapes=[pltpu.VMEM((RET_DK, RET_DV), F32)],
        compiler_params=_params(("parallel", "arbitrary"), ((S, RET_DK), BF16, 4), ((S, RET_DV), BF16, 6),
                                ((L, L), F32, 4), ((L, V7X_LANES), F32, 4),
                                ((RET_DK, RET_DV), F32, 4), ((L, RET_DV), F32, 8)),
        name="retention",
    )(z3, z3, z3, z3, dm, qd, kd, sd, gn_g.reshape(1, -1), gn_b.reshape(1, -1))
    return out.reshape(B * S, RET_V_WIDTH)


def _res_mm_kernel(x_ref, w_ref, h_ref, g_ref, o_ref, on_ref, *, sub):
    for r0 in range(0, x_ref.shape[0], sub):
        rows = slice(r0, r0 + sub)
        hnew = h_ref[rows, :] + jnp.dot(x_ref[rows, :], w_ref[...], preferred_element_type=F32)
        o_ref[rows, :] = hnew
        on_ref[rows, :] = _rmsnorm_rows(hnew, g_ref[...]).astype(on_ref.dtype)


def _res_matmul(x, w_all, layer, h, g_next, tm_pref, sub_pref):
    T, K = x.shape
    D = h.shape[1]
    tm = _tile(T, tm_pref)
    row = lambda i: (i, 0)
    fix = lambda i: (0, 0)
    return pl.pallas_call(
        functools.partial(_res_mm_kernel, sub=_tile(tm, sub_pref)),
        grid=(T // tm,),
        in_specs=[pl.BlockSpec((tm, K), row), _const_spec((None, K, D), lambda i: (layer, 0, 0)),
                  pl.BlockSpec((tm, D), row), pl.BlockSpec((1, D), fix)],
        out_specs=[pl.BlockSpec((tm, D), row), pl.BlockSpec((tm, D), row)],
        out_shape=[jax.ShapeDtypeStruct((T, D), F32), jax.ShapeDtypeStruct((T, D), BF16)],
        compiler_params=_params(("parallel",), ((tm, K), BF16, 2), ((K, D), BF16, 1),
                                ((tm, D), F32, 6), ((tm, D), BF16, 2)),
        name="res_proj",
    )(x, w_all, h, g_next.reshape(1, D))


def _ffn_up_kernel(x_ref, wg_ref, wv_ref, cwg_ref, cwv_ref, cbg_ref, cbv_ref, o_ref, wb_ref, e_ref,
                   *, tiles_per_seq, sub):
    tm = x_ref.shape[0]
    tn = o_ref.shape[1]
    halo = V7X_SUBLANES
    lanes = V7X_LANES
    nslab = tn // lanes
    i = pl.program_id(1)
    first = (i % tiles_per_seq) == 0

    @pl.when(i == 0)
    def _():
        wb_ref[:, 0:tn] = wg_ref[...].astype(BF16)
        wb_ref[:, tn:2 * tn] = wv_ref[...].astype(BF16)

    @pl.when(first)
    def _():
        e_ref[:, 0:halo, :] = jnp.zeros((e_ref.shape[0], halo, lanes), F32)

    @pl.when(jnp.logical_not(first))
    def _():
        e_ref[:, 0:halo, :] = e_ref[:, tm:tm + halo, :]

    def conv(k, r0, n, cw_ref, cb_ref, kc):
        cols = slice(kc * lanes, (kc + 1) * lanes)
        c = cb_ref[:, cols]
        for j in range(CONV_WIDTH):
            off = halo + r0 - (CONV_WIDTH - 1) + j
            c = c + e_ref[k, off:off + n, :] * cw_ref[j:j + 1, cols]
        return c

    n = sub
    for r0 in range(0, tm, n):
        a = jnp.dot(x_ref[r0:r0 + n, :], wb_ref[...], preferred_element_type=F32)
        for k in range(2 * nslab):
            e_ref[k, halo + r0:halo + r0 + n, :] = a[:, k * lanes:(k + 1) * lanes]
        for k in range(nslab):
            gate = conv(k, r0, n, cwg_ref, cbg_ref, k)
            val = conv(k + nslab, r0, n, cwv_ref, cbv_ref, k)
            o_ref[r0:r0 + n, k * lanes:(k + 1) * lanes] = (jax.nn.gelu(gate) * val).astype(o_ref.dtype)


def _ffn_up(hn, w_up_all, conv_w_all, conv_b_all, layer, S, tm_pref, sub_pref, tn):
    T, D = hn.shape
    tm = _tile(S, tm_pref)
    sub = _tile(tm, sub_pref)
    nj = D_FF // tn
    halo = V7X_SUBLANES
    conv_b3 = conv_b_all.reshape(conv_b_all.shape[0], 1, 2 * D_FF)
    gate_col = lambda j, i: (layer, 0, j)
    val_col = lambda j, i: (layer, 0, nj + j)
    return pl.pallas_call(
        functools.partial(_ffn_up_kernel, tiles_per_seq=S // tm, sub=sub),
        grid=(nj, T // tm),
        in_specs=[pl.BlockSpec((tm, D), lambda j, i: (i, 0)),
                  pl.BlockSpec((None, D, tn), gate_col),
                  pl.BlockSpec((None, D, tn), val_col),
                  pl.BlockSpec((None, CONV_WIDTH, tn), gate_col),
                  pl.BlockSpec((None, CONV_WIDTH, tn), val_col),
                  pl.BlockSpec((None, 1, tn), gate_col),
                  pl.BlockSpec((None, 1, tn), val_col)],
        out_specs=pl.BlockSpec((tm, tn), lambda j, i: (i, j)),
        out_shape=jax.ShapeDtypeStruct((T, D_FF), BF16),
        scratch_shapes=[pltpu.VMEM((D, 2 * tn), BF16),
                        pltpu.VMEM((2 * tn // V7X_LANES, tm + halo, V7X_LANES), F32)],
        compiler_params=_params(("arbitrary", "arbitrary"), ((tm, D), BF16, 2), ((D, tn), F32, 4),
                                ((D, 2 * tn), BF16, 1), ((tm + halo, 2 * tn), F32, 1),
                                ((sub, 2 * tn), F32, 4), ((tm, tn), BF16, 2)),
        name="ffn_up",
    )(hn, w_up_all, w_up_all, conv_w_all, conv_w_all, conv_b3, conv_b3)


def _ple_kernel(hn_ref, wg_ref, p_ref, wu_ref, h_ref, g_ref, *out_refs, sub):
    on_ref = out_refs[-1]
    for r0 in range(0, hn_ref.shape[0], sub):
        rows = slice(r0, r0 + sub)
        gate = jax.nn.sigmoid(jnp.dot(hn_ref[rows, :], wg_ref[...], preferred_element_type=F32))
        up = jnp.dot(p_ref[rows, :].astype(BF16), wu_ref[...], preferred_element_type=F32)
        hnew = h_ref[rows, :] + up * gate
        if len(out_refs) == 2:
            out_refs[0][rows, :] = hnew
        on_ref[rows, :] = _rmsnorm_rows(hnew, g_ref[...]).astype(on_ref.dtype)


def _ple(hn, wg_all, p_all, layer, wu_all, h, g_next, last, sub_pref):
    T, D = h.shape
    tm = _tile(T, 512)
    row = lambda i: (i, 0)
    fix = lambda i: (0, 0)
    out_specs = [pl.BlockSpec((tm, D), row), pl.BlockSpec((tm, D), row)]
    out_shape = [jax.ShapeDtypeStruct((T, D), F32), jax.ShapeDtypeStruct((T, D), BF16)]
    if last:
        out_specs, out_shape = out_specs[:1], out_shape[:1]
    return pl.pallas_call(
        functools.partial(_ple_kernel, sub=_tile(tm, sub_pref)),
        grid=(T // tm,),
        in_specs=[pl.BlockSpec((tm, D), row),
                  _const_spec((None, D, D), lambda i: (layer, 0, 0)),
                  pl.BlockSpec((None, tm, PLE_DIM), lambda i: (layer, i, 0)),
                  _const_spec((None, PLE_DIM, D), lambda i: (layer, 0, 0)),
                  pl.BlockSpec((tm, D), row),
                  pl.BlockSpec((1, D), fix)],
        out_specs=out_specs,
        out_shape=out_shape,
        compiler_params=_params(("parallel",), ((tm, D), BF16, 2), ((D, D), BF16, 1), ((PLE_DIM, D), BF16, 1),
                                ((tm, PLE_DIM), F32, 2), ((tm, D), F32, 8)),
        name="ple",
    )(hn, wg_all, p_all, wu_all, h, g_next.reshape(1, D))


def _rot_cols(w):
    half = MLA_ROPE // 2
    return jnp.concatenate([-w[..., half:], w[..., :half]], axis=-1)


def _even_in_weight(w):
    cq = w[..., :Q_LORA]
    ckv = w[..., Q_LORA:Q_LORA + KV_LORA]
    kpe = w[..., Q_LORA + KV_LORA:Q_LORA + KV_LORA + MLA_ROPE]
    uv = w[..., Q_LORA + KV_LORA + MLA_ROPE:]
    used = 2 * SGU_WIDTH + Q_LORA + KV_LORA + 2 * MLA_ROPE
    pad = jnp.zeros(w.shape[:-1] + (EVEN_IN_PAD - used,), w.dtype)
    return jnp.concatenate([uv, cq, ckv, kpe, _rot_cols(kpe), pad], axis=-1).astype(BF16)


def _q_up_weight(w):
    w = w.reshape(Q_LORA, MLA_HEADS, MLA_NOPE + MLA_ROPE)
    rope = w[..., MLA_NOPE:]
    w = jnp.concatenate([w[..., :MLA_NOPE], rope, _rot_cols(rope)], axis=-1)
    return w.reshape(Q_LORA, MLA_HEADS * MLA_QK_PAD).astype(BF16)


def _kv_up_weight(w):
    w = w.reshape(KV_LORA, MLA_HEADS, MLA_NOPE + MLA_VDIM)
    k = w[..., :MLA_NOPE].reshape(KV_LORA, -1)
    v = w[..., MLA_NOPE:].reshape(KV_LORA, -1)
    return jnp.concatenate([k, v], axis=1).astype(BF16)


def kernel(x, p, positions, mix_norm_g, even_w_in, mla_q_norm_g, mla_w_q_up, mla_kv_norm_g, mla_w_kv_up,
           sgu_ln_g, sgu_ln_b, sgu_w_s, sgu_b_s, even_w_out, ret_w_in, ret_gn_g, ret_gn_b, ret_w_out,
           ffn_norm_g, ffn_w_up, ffn_conv_w, ffn_conv_b, ffn_w_down, ple_norm_g, ple_w_gate, ple_w_up,
           final_norm_g):
    B, S, D = x.shape
    depth = p.shape[0]
    T = B * S
    assert D == D_MODEL and S % SGU_BLOCK == 0
    cos_t, sin_t, m_tab = _rope_tables(positions)
    h = x.reshape(T, D)
    p_all = p.reshape(depth, T, PLE_DIM)
    hn = _init_norm(h, mix_norm_g[0])
    even_in_b = _even_in_weight(even_w_in)
    even_out_b = even_w_out.astype(BF16)
    ret_in_b = ret_w_in.astype(BF16)
    ret_out_b = ret_w_out.astype(BF16)
    ffn_down_b = ffn_w_down.astype(BF16)
    ple_gate_b = ple_w_gate.astype(BF16)
    ple_up_b = ple_w_up.astype(BF16)
    for i in range(depth):
        j = i // 2
        if i % 2 == 0:
            z = _matmul(hn, even_in_b, j, 1024)
            q, k, v = _mla_prep(z, m_tab, mla_q_norm_g[j], mla_kv_norm_g[j],
                                _q_up_weight(mla_w_q_up[j]), _kv_up_weight(mla_w_kv_up[j]))
            a = _mla_attn(q, k, v, B, S, rowsum_on_mxu=(j == 1))
            h, hn = _even_out(a, z, h, sgu_ln_g[j], sgu_ln_b[j], sgu_w_s[j], sgu_b_s[j],
                              even_out_b, j, ffn_norm_g[i])
        else:
            z = _ret_in_proj(hn, ret_in_b, j, cos_t, sin_t, (256, 512)[j])
            r = _retention(z, ret_gn_g[j], ret_gn_b[j], B, S)
            h, hn = _res_matmul(r, ret_out_b, j, h, ffn_norm_g[i], 512, (256, 512)[j])
        up_tiles = ((2048, 512, 512), (2048, 1024, 512), (2048, 512, 256), (1024, 512, 512))[i]
        act = _ffn_up(hn, ffn_w_up, ffn_conv_w, ffn_conv_b, i, S, *up_tiles)
        down_tiles = ((256, 256), (512, 256), (256, 128), (512, 512))[i]
        h, hn = _res_matmul(act, ffn_down_b, i, h, ple_norm_g[i], *down_tiles)
        if i < depth - 1:
            h, hn = _ple(hn, ple_gate_b, p_all, i, ple_up_b, h, mix_norm_g[i + 1], False, (256, 512, 128)[i])
        else:
            out, = _ple(hn, ple_gate_b, p_all, i, ple_up_b, h, final_norm_g, True, 256)
    return out.reshape(B, S, D)
```

```python
import functools
import math

import jax
import jax.numpy as jnp
from jax import lax
from jax.experimental import pallas as pl
from jax.experimental.pallas import tpu as pltpu

F32 = jnp.float32
BF16 = jnp.bfloat16

D_MODEL = 2048
CHUNK = 64
EPS = 1e-6
ROPE_THETA = 10000.0
MLA_HEADS = 8
MLA_NOPE = 128
MLA_ROPE = 64
MLA_VDIM = 128
Q_LORA = 512
KV_LORA = 256
MLA_OUT = MLA_HEADS * MLA_VDIM
MLA_QK_PAD = 256
SGU_BLOCK = 128
SGU_GROUPS = 8
SGU_CH = 128
SGU_WIDTH = SGU_GROUPS * SGU_CH
EVEN_IN_PAD = 3072
RET_HEADS = 8
RET_DK = 256
RET_DV = 512
RET_QK_WIDTH = RET_HEADS * RET_DK
RET_V_WIDTH = RET_HEADS * RET_DV
RET_IN_WIDTH = 2 * RET_QK_WIDTH + 2 * RET_V_WIDTH
RET_SUPER = 256
D_FF = 5632
CONV_WIDTH = 3
PLE_DIM = 256

V7X_LANES = 128
V7X_SUBLANES = 8
V7X_VMEM_BUDGET = 58 * 1024 * 1024


def _nbytes(shape, dtype):
    return math.prod(shape) * jnp.dtype(dtype).itemsize


def _params(semantics, *buffers):
    need = sum(_nbytes(s, d) * n for s, d, n in buffers)
    return pltpu.CompilerParams(dimension_semantics=semantics,
                                vmem_limit_bytes=min(V7X_VMEM_BUDGET, need + (4 << 20)))


def _tile(n, pref, mult=V7X_SUBLANES):
    t = min(n, pref)
    while n % t or t % mult:
        t -= 1
    return t


def _const_spec(shape, index_map):
    return pl.BlockSpec(shape, index_map, pipeline_mode=pl.Buffered(1))


def _rmsnorm_rows(x, g):
    ms = jnp.mean(x * x, axis=-1, keepdims=True)
    return x * lax.rsqrt(ms + EPS) * g


def _rope_tables_kernel(pos_ref, fr_ref, fm_ref, cos_ref, sin_ref, m_ref):
    pos = pos_ref[...].astype(F32)
    ang = pos * fr_ref[...]
    cos_ref[...] = jnp.cos(ang)
    sin_ref[...] = jnp.sin(ang)
    angm = pos * fm_ref[...]
    lane = lax.broadcasted_iota(jnp.int32, angm.shape, 1)
    m_ref[...] = jnp.where(lane < MLA_ROPE, jnp.cos(angm), jnp.sin(angm))


def _rope_tables(positions):
    T = positions.size
    tm = _tile(T, 1024)
    half_r = RET_DK // 2
    half_m = MLA_ROPE // 2
    fr = (ROPE_THETA ** (-jnp.arange(half_r, dtype=F32) / half_r))[None, :]
    fm1 = ROPE_THETA ** (-jnp.arange(half_m, dtype=F32) / half_m)
    fm = jnp.tile(fm1, 4)[None, :]
    tab = jax.ShapeDtypeStruct((T, V7X_LANES), F32)
    row = pl.BlockSpec((tm, V7X_LANES), lambda i: (i, 0))
    frq = pl.BlockSpec((1, V7X_LANES), lambda i: (0, 0))
    return pl.pallas_call(
        _rope_tables_kernel,
        grid=(T // tm,),
        in_specs=[pl.BlockSpec((tm, 1), lambda i: (i, 0)), frq, frq],
        out_specs=[row, row, row],
        out_shape=[tab, tab, tab],
        compiler_params=_params(("parallel",), ((tm, V7X_LANES), F32, 16)),
        name="rope_tables",
    )(positions.reshape(T, 1), fr, fm)


def _init_norm_kernel(x_ref, g_ref, o_ref):
    o_ref[...] = _rmsnorm_rows(x_ref[...], g_ref[...]).astype(o_ref.dtype)


def _init_norm(x, g):
    T, D = x.shape
    tm = _tile(T, 512)
    return pl.pallas_call(
        _init_norm_kernel,
        grid=(T // tm,),
        in_specs=[pl.BlockSpec((tm, D), lambda i: (i, 0)), pl.BlockSpec((1, D), lambda i: (0, 0))],
        out_specs=pl.BlockSpec((tm, D), lambda i: (i, 0)),
        out_shape=jax.ShapeDtypeStruct((T, D), BF16),
        compiler_params=_params(("parallel",), ((tm, D), F32, 4), ((tm, D), BF16, 2)),
        name="init_norm",
    )(x, g.reshape(1, D))


def _mm_kernel(x_ref, w_ref, o_ref):
    o_ref[...] = jnp.dot(x_ref[...], w_ref[...], preferred_element_type=F32).astype(o_ref.dtype)


def _matmul(x, w_all, layer, tn):
    T, K = x.shape
    N = w_all.shape[2]
    tm = _tile(T, 1024)
    return pl.pallas_call(
        _mm_kernel,
        grid=(T // tm, N // tn),
        in_specs=[pl.BlockSpec((tm, K), lambda i, j: (i, 0)),
                  pl.BlockSpec((None, K, tn), lambda i, j: (layer, 0, j))],
        out_specs=pl.BlockSpec((tm, tn), lambda i, j: (i, j)),
        out_shape=jax.ShapeDtypeStruct((T, N), BF16),
        compiler_params=_params(("parallel", "arbitrary"), ((tm, K), BF16, 2), ((K, tn), BF16, 2),
                                ((tm, tn), BF16, 2), ((tm, tn), F32, 1)),
        name="proj",
    )(x, w_all)


def _ret_in_kernel(x_ref, w_ref, cos_ref, sin_ref, o_ref, *, sub):
    j = pl.program_id(1)
    tm = x_ref.shape[0]
    half = RET_DK // 2
    v_tile = 2
    gate_tile = 4

    def blocks(finish):
        for r0 in range(0, tm, sub):
            rows = slice(r0, r0 + sub)
            finish(jnp.dot(x_ref[rows, :], w_ref[...], preferred_element_type=F32), rows)

    def rotary(scale):
        def finish(a, rows):
            cs = cos_ref[rows, :]
            sn = sin_ref[rows, :]
            if scale != 1.0:
                cs, sn = cs * scale, sn * scale
            for h in range(RET_HEADS):
                c0 = h * RET_DK
                x1, x2 = a[:, c0:c0 + half], a[:, c0 + half:c0 + RET_DK]
                o_ref[rows, c0:c0 + half] = (x1 * cs - x2 * sn).astype(o_ref.dtype)
                o_ref[rows, c0 + half:c0 + RET_DK] = (x2 * cs + x1 * sn).astype(o_ref.dtype)
        return finish

    def plain(a, rows):
        o_ref[rows, :] = a.astype(o_ref.dtype)

    def gate(a, rows):
        o_ref[rows, :] = jax.nn.silu(a).astype(o_ref.dtype)

    @pl.when(j == 0)
    def _():
        blocks(rotary(1.0))

    @pl.when(j == 1)
    def _():
        blocks(rotary(RET_DK ** -0.5))

    @pl.when(jnp.logical_and(j >= v_tile, j < gate_tile))
    def _():
        blocks(plain)

    @pl.when(j >= gate_tile)
    def _():
        blocks(gate)


def _ret_in_proj(x, w_all, layer, cos_t, sin_t, sub_pref):
    T, K = x.shape
    tn = RET_QK_WIDTH
    tm = _tile(T, 1024)
    half = RET_DK // 2
    return pl.pallas_call(
        functools.partial(_ret_in_kernel, sub=_tile(tm, sub_pref)),
        grid=(T // tm, RET_IN_WIDTH // tn),
        in_specs=[pl.BlockSpec((tm, K), lambda i, j: (i, 0)),
                  pl.BlockSpec((None, K, tn), lambda i, j: (layer, 0, j)),
                  pl.BlockSpec((tm, half), lambda i, j: (i, 0)),
                  pl.BlockSpec((tm, half), lambda i, j: (i, 0))],
        out_specs=pl.BlockSpec((tm, tn), lambda i, j: (i, j)),
        out_shape=jax.ShapeDtypeStruct((T, RET_IN_WIDTH), BF16),
        compiler_params=_params(("parallel", "arbitrary"), ((tm, K), BF16, 2), ((K, tn), BF16, 2),
                                ((tm, tn), BF16, 2), ((tm, half), F32, 4), ((tm // 4, tn), F32, 4)),
        name="ret_in_proj",
    )(x, w_all, cos_t, sin_t)


def _mla_prep_kernel(cq_ref, ckv_ref, kpe_ref, m_ref, qg_ref, kvg_ref, wq_ref, wkv_ref,
                     q_ref, k_ref, v_ref):
    scale = (MLA_NOPE + MLA_ROPE) ** -0.5
    m = m_ref[...]
    cqn = _rmsnorm_rows(cq_ref[...].astype(F32), qg_ref[...]).astype(BF16)
    q = jnp.dot(cqn, wq_ref[...], preferred_element_type=F32)
    for h in range(MLA_HEADS):
        c0 = h * MLA_QK_PAD
        q_ref[:, c0:c0 + MLA_NOPE] = (q[:, c0:c0 + MLA_NOPE] * scale).astype(BF16)
        y = q[:, c0 + MLA_NOPE:c0 + MLA_QK_PAD] * m
        y = y + pltpu.roll(y, MLA_ROPE, axis=1)
        q_ref[:, c0 + MLA_NOPE:c0 + MLA_QK_PAD] = (y * scale).astype(BF16)
    ckvn = _rmsnorm_rows(ckv_ref[...].astype(F32), kvg_ref[...]).astype(BF16)
    kv = jnp.dot(ckvn, wkv_ref[...], preferred_element_type=F32)
    yk = kpe_ref[...].astype(F32) * m
    yk = yk + pltpu.roll(yk, MLA_ROPE, axis=1)
    lane = lax.broadcasted_iota(jnp.int32, yk.shape, 1)
    k_rot = jnp.where(lane < MLA_ROPE, yk, 0.0).astype(BF16)
    for h in range(MLA_HEADS):
        c0 = h * MLA_QK_PAD
        k_ref[:, c0:c0 + MLA_NOPE] = kv[:, h * MLA_NOPE:(h + 1) * MLA_NOPE].astype(BF16)
        k_ref[:, c0 + MLA_NOPE:c0 + MLA_QK_PAD] = k_rot
    v_ref[...] = kv[:, MLA_HEADS * MLA_NOPE:].astype(BF16)


def _mla_prep(z, m_tab, q_norm_g, kv_norm_g, wq, wkv, tm_pref):
    T = z.shape[0]
    tm = _tile(T, tm_pref)
    qkw = MLA_HEADS * MLA_QK_PAD
    u_v = 2 * SGU_WIDTH
    return pl.pallas_call(
        _mla_prep_kernel,
        grid=(T // tm,),
        in_specs=[
            pl.BlockSpec((tm, Q_LORA), lambda i: (i, u_v // Q_LORA)),
            pl.BlockSpec((tm, KV_LORA), lambda i: (i, (u_v + Q_LORA) // KV_LORA)),
            pl.BlockSpec((tm, 2 * MLA_ROPE), lambda i: (i, (u_v + Q_LORA + KV_LORA) // (2 * MLA_ROPE))),
            pl.BlockSpec((tm, V7X_LANES), lambda i: (i, 0)),
            pl.BlockSpec((1, Q_LORA), lambda i: (0, 0)),
            pl.BlockSpec((1, KV_LORA), lambda i: (0, 0)),
            _const_spec((Q_LORA, qkw), lambda i: (0, 0)),
            _const_spec((KV_LORA, qkw), lambda i: (0, 0)),
        ],
        out_specs=[pl.BlockSpec((tm, qkw), lambda i: (i, 0)),
                   pl.BlockSpec((tm, qkw), lambda i: (i, 0)),
                   pl.BlockSpec((tm, MLA_OUT), lambda i: (i, 0))],
        out_shape=[jax.ShapeDtypeStruct((T, qkw), BF16), jax.ShapeDtypeStruct((T, qkw), BF16),
                   jax.ShapeDtypeStruct((T, MLA_OUT), BF16)],
        compiler_params=_params(("parallel",), ((tm, qkw), BF16, 8), ((tm, qkw), F32, 3),
                                ((Q_LORA + KV_LORA, qkw), BF16, 1)),
        name="mla_prep",
    )(z, z, z, m_tab, q_norm_g.reshape(1, -1), kv_norm_g.reshape(1, -1), wq, wkv)


def _mla_attn_kernel(q_ref, k_ref, v_ref, o_ref, va_ref, *, tq, rowsum_on_mxu):
    S = q_ref.shape[0]
    r = lax.broadcasted_iota(jnp.int32, (tq, tq), 0) // CHUNK
    c = lax.broadcasted_iota(jnp.int32, (tq, tq), 1) // CHUNK
    bias = jnp.where(c <= r, 0.0, -1e30).astype(F32)
    if rowsum_on_mxu:
        va_ref[:, 0:MLA_VDIM] = v_ref[...]
        va_ref[:, MLA_VDIM:] = jnp.ones((S, MLA_VDIM), BF16)
    for qi in range(S // tq):
        past = qi * tq
        q = q_ref[past:past + tq, :]
        s = lax.dot_general(q, k_ref[0:past + tq, :], (((1,), (1,)), ((), ())),
                            preferred_element_type=F32)
        s_diag = s[:, past:] + bias
        s = s_diag if qi == 0 else jnp.concatenate([s[:, :past], s_diag], axis=1)
        mx = jnp.max(s, axis=-1, keepdims=True)
        p = jnp.exp(s - mx)
        if rowsum_on_mxu:
            oa = jnp.dot(p.astype(BF16), va_ref[0:past + tq, :], preferred_element_type=F32)
            o_ref[past:past + tq, :] = (oa[:, :MLA_VDIM] / oa[:, MLA_VDIM:]).astype(o_ref.dtype)
        else:
            l = jnp.sum(p, axis=-1, keepdims=True)
            o = jnp.dot(p.astype(BF16), v_ref[0:past + tq, :], preferred_element_type=F32)
            o_ref[past:past + tq, :] = (o / l).astype(o_ref.dtype)


def _mla_attn(q, k, v, B, S, rowsum_on_mxu):
    tq = _tile(S, 256, CHUNK)
    q3 = q.reshape(B, S, -1)
    k3 = k.reshape(B, S, -1)
    v3 = v.reshape(B, S, -1)
    out = pl.pallas_call(
        functools.partial(_mla_attn_kernel, tq=tq, rowsum_on_mxu=rowsum_on_mxu),
        grid=(B, MLA_HEADS),
        scratch_shapes=[pltpu.VMEM((S, 2 * MLA_VDIM), BF16)],
        in_specs=[pl.BlockSpec((None, S, MLA_QK_PAD), lambda b, h: (b, 0, h)),
                  pl.BlockSpec((None, S, MLA_QK_PAD), lambda b, h: (b, 0, h)),
                  pl.BlockSpec((None, S, MLA_VDIM), lambda b, h: (b, 0, h))],
        out_specs=pl.BlockSpec((None, S, MLA_VDIM), lambda b, h: (b, 0, h)),
        out_shape=jax.ShapeDtypeStruct((B, S, MLA_OUT), BF16),
        compiler_params=_params(("parallel", "parallel"), ((S, MLA_QK_PAD), BF16, 4),
                                ((S, MLA_VDIM), BF16, 4), ((tq, S), F32, 4)),
        name="mla_attn",
    )(q3, k3, v3)
    return out.reshape(B * S, MLA_OUT)


def _even_out_kernel(a_ref, u_ref, v_ref, h_ref, lng_ref, lnb_ref, ws_ref, bs_ref, wo_ref, g_ref,
                     o_ref, on_ref, wm_ref, bo_ref, acc_ref, *, sub):
    tm = a_ref.shape[0]
    pr = lax.broadcasted_iota(jnp.int32, (SGU_BLOCK, SGU_BLOCK), 0) // CHUNK
    pc = lax.broadcasted_iota(jnp.int32, (SGU_BLOCK, SGU_BLOCK), 1) // CHUNK
    for g in range(SGU_GROUPS):
        wm_ref[g] = jnp.where(pr >= pc, ws_ref[g], 0.0).astype(BF16)
    for r0 in range(0, tm, sub):
        rows = slice(r0, r0 + sub)
        gu = jax.nn.gelu(u_ref[rows, :].astype(F32))
        gv = jax.nn.gelu(v_ref[rows, :].astype(F32))
        mu = jnp.mean(gv, axis=-1, keepdims=True)
        xc = gv - mu
        y = xc * lax.rsqrt(jnp.mean(xc * xc, axis=-1, keepdims=True) + EPS)
        vn = (y * lng_ref[...] + lnb_ref[...]).astype(BF16)
        acc_ref[rows, :] = jnp.dot(a_ref[rows, :], wo_ref[0:MLA_OUT, :], preferred_element_type=F32)
        for g in range(SGU_GROUPS):
            bcol = bs_ref[:, g:g + 1]
            cols = slice(g * SGU_CH, (g + 1) * SGU_CH)
            for nb in range(sub // SGU_BLOCK):
                rr = slice(nb * SGU_BLOCK, (nb + 1) * SGU_BLOCK)
                s = jnp.dot(wm_ref[g], vn[rr, cols], preferred_element_type=F32) + bcol
                bo_ref[r0 + nb * SGU_BLOCK:r0 + (nb + 1) * SGU_BLOCK, cols] = (gu[rr, cols] * s).astype(BF16)
        acc = acc_ref[rows, :] + jnp.dot(bo_ref[rows, :], wo_ref[MLA_OUT:, :], preferred_element_type=F32)
        hnew = h_ref[rows, :] + acc
        o_ref[rows, :] = hnew
        on_ref[rows, :] = _rmsnorm_rows(hnew, g_ref[...]).astype(on_ref.dtype)


def _even_out(a, z, h, ln_g, ln_b, w_s, b_s, wo_all, layer, g_next, sub_pref):
    T, D = h.shape
    tm = _tile(T, 512, SGU_BLOCK)
    sub = _tile(tm, sub_pref, SGU_BLOCK)
    row = lambda i: (i, 0)
    fix = lambda i: (0, 0)
    return pl.pallas_call(
        functools.partial(_even_out_kernel, sub=sub),
        grid=(T // tm,),
        in_specs=[pl.BlockSpec((tm, MLA_OUT), row),
                  pl.BlockSpec((tm, SGU_WIDTH), lambda i: (i, 0)),
                  pl.BlockSpec((tm, SGU_WIDTH), lambda i: (i, 1)),
                  pl.BlockSpec((tm, D), row),
                  pl.BlockSpec((1, SGU_WIDTH), fix),
                  pl.BlockSpec((1, SGU_WIDTH), fix),
                  _const_spec((SGU_GROUPS, SGU_BLOCK, SGU_BLOCK), lambda i: (0, 0, 0)),
                  pl.BlockSpec((SGU_BLOCK, SGU_GROUPS), fix),
                  _const_spec((None, MLA_OUT + SGU_WIDTH, D), lambda i: (layer, 0, 0)),
                  pl.BlockSpec((1, D), fix)],
        out_specs=[pl.BlockSpec((tm, D), row), pl.BlockSpec((tm, D), row)],
        out_shape=[jax.ShapeDtypeStruct((T, D), F32), jax.ShapeDtypeStruct((T, D), BF16)],
        scratch_shapes=[pltpu.VMEM((SGU_GROUPS, SGU_BLOCK, SGU_BLOCK), BF16), pltpu.VMEM((tm, SGU_WIDTH), BF16),
                        pltpu.VMEM((tm, D), F32)],
        compiler_params=_params(("parallel",), ((tm, D), F32, 7), ((tm, D), BF16, 6),
                                ((MLA_OUT + SGU_WIDTH, D), BF16, 1), ((tm, SGU_WIDTH), F32, 4)),
        name="even_out",
    )(a, z, z, h, ln_g.reshape(1, -1), ln_b.reshape(1, -1), w_s, b_s.T, wo_all, g_next.reshape(1, D))


def _retention_kernel(q_ref, k_ref, v_ref, gate_ref, dm_ref, qd_ref, kd_ref, sd_ref,
                      gng_ref, gnb_ref, o_ref, state_ref, *, L):
    S = q_ref.shape[0]
    state_ref[...] = jnp.zeros_like(state_ref)
    dm = dm_ref[...]
    qd = qd_ref[...]
    kd = kd_ref[...]
    sd = sd_ref[...]
    gng = gng_ref[...]
    gnb = gnb_ref[...]

    def step(c):
        rows = slice(c * L, (c + 1) * L)
        qb = q_ref[rows, :]
        kb = k_ref[rows, :]
        vb = v_ref[rows, :]
        sc = lax.dot_general(qb, kb, (((1,), (1,)), ((), ())), preferred_element_type=F32) * dm
        intra = jnp.dot(sc.astype(BF16), vb, preferred_element_type=F32)
        st = state_ref[...]
        cross = jnp.dot(qb, st.astype(BF16), preferred_element_type=F32) * qd
        kz = (kb.astype(F32) * kd).astype(BF16)
        state_ref[...] = st * sd + lax.dot_general(kz, vb, (((0,), (0,)), ((), ())),
                                                   preferred_element_type=F32)
        out = intra + cross
        mu = jnp.mean(out, axis=-1, keepdims=True)
        xc = out - mu
        y = xc * lax.rsqrt(jnp.mean(xc * xc, axis=-1, keepdims=True) + EPS)
        y = y * gng + gnb
        o_ref[rows, :] = (y * gate_ref[rows, :].astype(F32)).astype(o_ref.dtype)

    for c in range(S // L):
        step(c)


def _retention_tables(L):
    log_g = jnp.log1p(-(2.0 ** (-5.0 - jnp.arange(RET_HEADS, dtype=F32))))
    idx = jnp.arange(L, dtype=F32)
    chunk = jnp.arange(L) // CHUNK
    decay = jnp.exp(log_g[:, None, None] * jnp.abs(idx[:, None] - idx[None, :]))
    dm = jnp.where(chunk[None, :, None] >= chunk[None, None, :], decay, 0.0)
    qd = jnp.exp(log_g[:, None] * (idx + 1.0)[None, :])[:, :, None]
    kd = jnp.exp(log_g[:, None] * (L - 1 - idx)[None, :])[:, :, None]
    sd = jnp.broadcast_to(jnp.exp(log_g * L)[:, None, None], (RET_HEADS, 1, RET_DV))
    return dm, qd, kd, sd


def _retention(z, gn_g, gn_b, B, S):
    L = _tile(S, RET_SUPER, CHUNK)
    z3 = z.reshape(B, S, RET_IN_WIDTH)
    dm, qd, kd, sd = _retention_tables(L)
    qk_blocks = RET_QK_WIDTH // RET_DK
    v_first = 2 * RET_QK_WIDTH // RET_DV
    g_first = v_first + RET_V_WIDTH // RET_DV
    out = pl.pallas_call(
        functools.partial(_retention_kernel, L=L),
        grid=(B, RET_HEADS),
        in_specs=[pl.BlockSpec((None, S, RET_DK), lambda b, h: (b, 0, h)),
                  pl.BlockSpec((None, S, RET_DK), lambda b, h: (b, 0, qk_blocks + h)),
                  pl.BlockSpec((None, S, RET_DV), lambda b, h: (b, 0, v_first + h)),
                  pl.BlockSpec((None, S, RET_DV), lambda b, h: (b, 0, g_first + h)),
                  pl.BlockSpec((None, L, L), lambda b, h: (h, 0, 0)),
                  pl.BlockSpec((None, L, 1), lambda b, h: (h, 0, 0)),
                  pl.BlockSpec((None, L, 1), lambda b, h: (h, 0, 0)),
                  pl.BlockSpec((None, 1, RET_DV), lambda b, h: (h, 0, 0)),
                  pl.BlockSpec((1, RET_DV), lambda b, h: (0, h)),
                  pl.BlockSpec((1, RET_DV), lambda b, h: (0, h))],
        out_specs=pl.BlockSpec((None, S, RET_DV), lambda b, h: (b, 0, h)),
        out_shape=jax.ShapeDtypeStruct((B, S, RET_V_WIDTH), BF16),
        scratch_shapes=[pltpu.VMEM((RET_DK, RET_DV), F32)],
        compiler_params=_params(("parallel", "arbitrary"), ((S, RET_DK), BF16, 4), ((S, RET_DV), BF16, 6),
                                ((L, L), F32, 4), ((L, V7X_LANES), F32, 4),
                                ((RET_DK, RET_DV), F32, 4), ((L, RET_DV), F32, 8)),
        name="retention",
    )(z3, z3, z3, z3, dm, qd, kd, sd, gn_g.reshape(1, -1), gn_b.reshape(1, -1))
    return out.reshape(B * S, RET_V_WIDTH)


def _res_mm_kernel(x_ref, w_ref, h_ref, g_ref, o_ref, on_ref, *, sub):
    for r0 in range(0, x_ref.shape[0], sub):
        rows = slice(r0, r0 + sub)
        hnew = h_ref[rows, :] + jnp.dot(x_ref[rows, :], w_ref[...], preferred_element_type=F32)
        o_ref[rows, :] = hnew
        on_ref[rows, :] = _rmsnorm_rows(hnew, g_ref[...]).astype(on_ref.dtype)


def _res_matmul(x, w_all, layer, h, g_next, tm_pref, sub_pref):
    T, K = x.shape
    D = h.shape[1]
    tm = _tile(T, tm_pref)
    row = lambda i: (i, 0)
    fix = lambda i: (0, 0)
    return pl.pallas_call(
        functools.partial(_res_mm_kernel, sub=_tile(tm, sub_pref)),
        grid=(T // tm,),
        in_specs=[pl.BlockSpec((tm, K), row), _const_spec((None, K, D), lambda i: (layer, 0, 0)),
                  pl.BlockSpec((tm, D), row), pl.BlockSpec((1, D), fix)],
        out_specs=[pl.BlockSpec((tm, D), row), pl.BlockSpec((tm, D), row)],
        out_shape=[jax.ShapeDtypeStruct((T, D), F32), jax.ShapeDtypeStruct((T, D), BF16)],
        compiler_params=_params(("parallel",), ((tm, K), BF16, 2), ((K, D), BF16, 1),
                                ((tm, D), F32, 6), ((tm, D), BF16, 2)),
        name="res_proj",
    )(x, w_all, h, g_next.reshape(1, D))


def _ffn_up_kernel(x_ref, wg_ref, wv_ref, cwg_ref, cwv_ref, cbg_ref, cbv_ref, o_ref, wb_ref, e_ref,
                   *, tiles_per_seq, sub):
    tm = x_ref.shape[0]
    tn = o_ref.shape[1]
    halo = V7X_SUBLANES
    lanes = V7X_LANES
    nslab = tn // lanes
    i = pl.program_id(1)
    first = (i % tiles_per_seq) == 0

    @pl.when(i == 0)
    def _():
        wb_ref[:, 0:tn] = wg_ref[...].astype(BF16)
        wb_ref[:, tn:2 * tn] = wv_ref[...].astype(BF16)

    @pl.when(first)
    def _():
        e_ref[:, 0:halo, :] = jnp.zeros((e_ref.shape[0], halo, lanes), F32)

    @pl.when(jnp.logical_not(first))
    def _():
        e_ref[:, 0:halo, :] = e_ref[:, tm:tm + halo, :]

    def conv(k, r0, n, cw_ref, cb_ref, kc):
        cols = slice(kc * lanes, (kc + 1) * lanes)
        c = cb_ref[:, cols]
        for j in range(CONV_WIDTH):
            off = halo + r0 - (CONV_WIDTH - 1) + j
            c = c + e_ref[k, off:off + n, :] * cw_ref[j:j + 1, cols]
        return c

    n = sub
    for r0 in range(0, tm, n):
        a = jnp.dot(x_ref[r0:r0 + n, :], wb_ref[...], preferred_element_type=F32)
        for k in range(2 * nslab):
            e_ref[k, halo + r0:halo + r0 + n, :] = a[:, k * lanes:(k + 1) * lanes]
        for k in range(nslab):
            gate = conv(k, r0, n, cwg_ref, cbg_ref, k)
            val = conv(k + nslab, r0, n, cwv_ref, cbv_ref, k)
            o_ref[r0:r0 + n, k * lanes:(k + 1) * lanes] = (jax.nn.gelu(gate) * val).astype(o_ref.dtype)


def _ffn_up(hn, w_up_all, conv_w_all, conv_b_all, layer, S, tm_pref, sub_pref, tn):
    T, D = hn.shape
    tm = _tile(S, tm_pref)
    sub = _tile(tm, sub_pref)
    nj = D_FF // tn
    halo = V7X_SUBLANES
    conv_b3 = conv_b_all.reshape(conv_b_all.shape[0], 1, 2 * D_FF)
    gate_col = lambda j, i: (layer, 0, j)
    val_col = lambda j, i: (layer, 0, nj + j)
    return pl.pallas_call(
        functools.partial(_ffn_up_kernel, tiles_per_seq=S // tm, sub=sub),
        grid=(nj, T // tm),
        in_specs=[pl.BlockSpec((tm, D), lambda j, i: (i, 0)),
                  pl.BlockSpec((None, D, tn), gate_col),
                  pl.BlockSpec((None, D, tn), val_col),
                  pl.BlockSpec((None, CONV_WIDTH, tn), gate_col),
                  pl.BlockSpec((None, CONV_WIDTH, tn), val_col),
                  pl.BlockSpec((None, 1, tn), gate_col),
                  pl.BlockSpec((None, 1, tn), val_col)],
        out_specs=pl.BlockSpec((tm, tn), lambda j, i: (i, j)),
        out_shape=jax.ShapeDtypeStruct((T, D_FF), BF16),
        scratch_shapes=[pltpu.VMEM((D, 2 * tn), BF16),
                        pltpu.VMEM((2 * tn // V7X_LANES, tm + halo, V7X_LANES), F32)],
        compiler_params=_params(("arbitrary", "arbitrary"), ((tm, D), BF16, 2), ((D, tn), F32, 4),
                                ((D, 2 * tn), BF16, 1), ((tm + halo, 2 * tn), F32, 1),
                                ((sub, 2 * tn), F32, 4), ((tm, tn), BF16, 2)),
        name="ffn_up",
    )(hn, w_up_all, w_up_all, conv_w_all, conv_w_all, conv_b3, conv_b3)


def _ple_kernel(hn_ref, wg_ref, p_ref, wu_ref, h_ref, g_ref, *out_refs, sub):
    on_ref = out_refs[-1]
    for r0 in range(0, hn_ref.shape[0], sub):
        rows = slice(r0, r0 + sub)
        gate = jax.nn.sigmoid(jnp.dot(hn_ref[rows, :], wg_ref[...], preferred_element_type=F32))
        up = jnp.dot(p_ref[rows, :].astype(BF16), wu_ref[...], preferred_element_type=F32)
        hnew = h_ref[rows, :] + up * gate
        if len(out_refs) == 2:
            out_refs[0][rows, :] = hnew
        on_ref[rows, :] = _rmsnorm_rows(hnew, g_ref[...]).astype(on_ref.dtype)


def _ple(hn, wg_all, p_all, layer, wu_all, h, g_next, last, sub_pref):
    T, D = h.shape
    tm = _tile(T, 512)
    row = lambda i: (i, 0)
    fix = lambda i: (0, 0)
    out_specs = [pl.BlockSpec((tm, D), row), pl.BlockSpec((tm, D), row)]
    out_shape = [jax.ShapeDtypeStruct((T, D), F32), jax.ShapeDtypeStruct((T, D), BF16)]
    if last:
        out_specs, out_shape = out_specs[:1], out_shape[:1]
    return pl.pallas_call(
        functools.partial(_ple_kernel, sub=_tile(tm, sub_pref)),
        grid=(T // tm,),
        in_specs=[pl.BlockSpec((tm, D), row),
                  _const_spec((None, D, D), lambda i: (layer, 0, 0)),
                  pl.BlockSpec((None, tm, PLE_DIM), lambda i: (layer, i, 0)),
                  _const_spec((None, PLE_DIM, D), lambda i: (layer, 0, 0)),
                  pl.BlockSpec((tm, D), row),
                  pl.BlockSpec((1, D), fix)],
        out_specs=out_specs,
        out_shape=out_shape,
        compiler_params=_params(("parallel",), ((tm, D), BF16, 2), ((D, D), BF16, 1), ((PLE_DIM, D), BF16, 1),
                                ((tm, PLE_DIM), F32, 2), ((tm, D), F32, 8)),
        name="ple",
    )(hn, wg_all, p_all, wu_all, h, g_next.reshape(1, D))


def _rot_cols(w):
    half = MLA_ROPE // 2
    return jnp.concatenate([-w[..., half:], w[..., :half]], axis=-1)


def _even_in_weight(w):
    cq = w[..., :Q_LORA]
    ckv = w[..., Q_LORA:Q_LORA + KV_LORA]
    kpe = w[..., Q_LORA + KV_LORA:Q_LORA + KV_LORA + MLA_ROPE]
    uv = w[..., Q_LORA + KV_LORA + MLA_ROPE:]
    used = 2 * SGU_WIDTH + Q_LORA + KV_LORA + 2 * MLA_ROPE
    pad = jnp.zeros(w.shape[:-1] + (EVEN_IN_PAD - used,), w.dtype)
    return jnp.concatenate([uv, cq, ckv, kpe, _rot_cols(kpe), pad], axis=-1).astype(BF16)


def _q_up_weight(w):
    w = w.reshape(Q_LORA, MLA_HEADS, MLA_NOPE + MLA_ROPE)
    rope = w[..., MLA_NOPE:]
    w = jnp.concatenate([w[..., :MLA_NOPE], rope, _rot_cols(rope)], axis=-1)
    return w.reshape(Q_LORA, MLA_HEADS * MLA_QK_PAD).astype(BF16)


def _kv_up_weight(w):
    w = w.reshape(KV_LORA, MLA_HEADS, MLA_NOPE + MLA_VDIM)
    k = w[..., :MLA_NOPE].reshape(KV_LORA, -1)
    v = w[..., MLA_NOPE:].reshape(KV_LORA, -1)
    return jnp.concatenate([k, v], axis=1).astype(BF16)


def kernel(x, p, positions, mix_norm_g, even_w_in, mla_q_norm_g, mla_w_q_up, mla_kv_norm_g, mla_w_kv_up,
           sgu_ln_g, sgu_ln_b, sgu_w_s, sgu_b_s, even_w_out, ret_w_in, ret_gn_g, ret_gn_b, ret_w_out,
           ffn_norm_g, ffn_w_up, ffn_conv_w, ffn_conv_b, ffn_w_down, ple_norm_g, ple_w_gate, ple_w_up,
           final_norm_g):
    B, S, D = x.shape
    depth = p.shape[0]
    T = B * S
    assert D == D_MODEL and S % SGU_BLOCK == 0
    cos_t, sin_t, m_tab = _rope_tables(positions)
    h = x.reshape(T, D)
    p_all = p.reshape(depth, T, PLE_DIM)
    hn = _init_norm(h, mix_norm_g[0])
    even_in_b = _even_in_weight(even_w_in)
    even_out_b = even_w_out.astype(BF16)
    ret_in_b = ret_w_in.astype(BF16)
    ret_out_b = ret_w_out.astype(BF16)
    ffn_down_b = ffn_w_down.astype(BF16)
    ple_gate_b = ple_w_gate.astype(BF16)
    ple_up_b = ple_w_up.astype(BF16)
    for i in range(depth):
        j = i // 2
        if i % 2 == 0:
            z = _matmul(hn, even_in_b, j, 1024)
            q, k, v = _mla_prep(z, m_tab, mla_q_norm_g[j], mla_kv_norm_g[j],
                                _q_up_weight(mla_w_q_up[j]), _kv_up_weight(mla_w_kv_up[j]), (512, 1024)[j])
            a = _mla_attn(q, k, v, B, S, rowsum_on_mxu=True)
            h, hn = _even_out(a, z, h, sgu_ln_g[j], sgu_ln_b[j], sgu_w_s[j], sgu_b_s[j],
                              even_out_b, j, ffn_norm_g[i], (256, 512)[j])
        else:
            z = _ret_in_proj(hn, ret_in_b, j, cos_t, sin_t, 256)
            r = _retention(z, ret_gn_g[j], ret_gn_b[j], B, S)
            h, hn = _res_matmul(r, ret_out_b, j, h, ffn_norm_g[i], 512, 512)
        act = _ffn_up(hn, ffn_w_up, ffn_conv_w, ffn_conv_b, i, S, 2048, 1024, 512)
        h, hn = _res_matmul(act, ffn_down_b, i, h, ple_norm_g[i], 512, 512)
        if i < depth - 1:
            h, hn = _ple(hn, ple_gate_b, p_all, i, ple_up_b, h, mix_norm_g[i + 1], False, 512)
        else:
            out, = _ple(hn, ple_gate_b, p_all, i, ple_up_b, h, final_norm_g, True, 512)
    return out.reshape(B, S, D)
```

```python
import functools
import math

import jax
import jax.numpy as jnp
from jax import lax
from jax.experimental import pallas as pl
from jax.experimental.pallas import tpu as pltpu

F32 = jnp.float32
BF16 = jnp.bfloat16

D_MODEL = 2048
CHUNK = 64
EPS = 1e-6
ROPE_THETA = 10000.0
MLA_HEADS = 8
MLA_NOPE = 128
MLA_ROPE = 64
MLA_VDIM = 128
Q_LORA = 512
KV_LORA = 256
MLA_OUT = MLA_HEADS * MLA_VDIM
MLA_QK_PAD = 256
SGU_BLOCK = 128
SGU_GROUPS = 8
SGU_CH = 128
SGU_WIDTH = SGU_GROUPS * SGU_CH
EVEN_IN_PAD = 3072
RET_HEADS = 8
RET_DK = 256
RET_DV = 512
RET_QK_WIDTH = RET_HEADS * RET_DK
RET_V_WIDTH = RET_HEADS * RET_DV
RET_IN_WIDTH = 2 * RET_QK_WIDTH + 2 * RET_V_WIDTH
RET_SUPER = 256
D_FF = 5632
CONV_WIDTH = 3
PLE_DIM = 256

V7X_LANES = 128
V7X_SUBLANES = 8
V7X_VMEM_BUDGET = 58 * 1024 * 1024


def _nbytes(shape, dtype):
    return math.prod(shape) * jnp.dtype(dtype).itemsize


def _params(semantics, *buffers):
    need = sum(_nbytes(s, d) * n for s, d, n in buffers)
    return pltpu.CompilerParams(dimension_semantics=semantics,
                                vmem_limit_bytes=min(V7X_VMEM_BUDGET, need + (4 << 20)))


def _tile(n, pref, mult=V7X_SUBLANES):
    t = min(n, pref)
    while n % t or t % mult:
        t -= 1
    return t


def _const_spec(shape, index_map):
    return pl.BlockSpec(shape, index_map, pipeline_mode=pl.Buffered(1))


def _rmsnorm_rows(x, g):
    ms = jnp.mean(x * x, axis=-1, keepdims=True)
    return x * lax.rsqrt(ms + EPS) * g


def _rope_tables_kernel(pos_ref, fr_ref, fm_ref, cos_ref, sin_ref, m_ref):
    pos = pos_ref[...].astype(F32)
    ang = pos * fr_ref[...]
    cos_ref[...] = jnp.cos(ang)
    sin_ref[...] = jnp.sin(ang)
    angm = pos * fm_ref[...]
    lane = lax.broadcasted_iota(jnp.int32, angm.shape, 1)
    m_ref[...] = jnp.where(lane < MLA_ROPE, jnp.cos(angm), jnp.sin(angm))


def _rope_tables(positions):
    T = positions.size
    tm = _tile(T, 1024)
    half_r = RET_DK // 2
    half_m = MLA_ROPE // 2
    fr = (ROPE_THETA ** (-jnp.arange(half_r, dtype=F32) / half_r))[None, :]
    fm1 = ROPE_THETA ** (-jnp.arange(half_m, dtype=F32) / half_m)
    fm = jnp.tile(fm1, 4)[None, :]
    tab = jax.ShapeDtypeStruct((T, V7X_LANES), F32)
    row = pl.BlockSpec((tm, V7X_LANES), lambda i: (i, 0))
    frq = pl.BlockSpec((1, V7X_LANES), lambda i: (0, 0))
    return pl.pallas_call(
        _rope_tables_kernel,
        grid=(T // tm,),
        in_specs=[pl.BlockSpec((tm, 1), lambda i: (i, 0)), frq, frq],
        out_specs=[row, row, row],
        out_shape=[tab, tab, tab],
        compiler_params=_params(("parallel",), ((tm, V7X_LANES), F32, 16)),
        name="rope_tables",
    )(positions.reshape(T, 1), fr, fm)


def _init_norm_kernel(x_ref, g_ref, o_ref):
    o_ref[...] = _rmsnorm_rows(x_ref[...], g_ref[...]).astype(o_ref.dtype)


def _init_norm(x, g):
    T, D = x.shape
    tm = _tile(T, 512)
    return pl.pallas_call(
        _init_norm_kernel,
        grid=(T // tm,),
        in_specs=[pl.BlockSpec((tm, D), lambda i: (i, 0)), pl.BlockSpec((1, D), lambda i: (0, 0))],
        out_specs=pl.BlockSpec((tm, D), lambda i: (i, 0)),
        out_shape=jax.ShapeDtypeStruct((T, D), BF16),
        compiler_params=_params(("parallel",), ((tm, D), F32, 4), ((tm, D), BF16, 2)),
        name="init_norm",
    )(x, g.reshape(1, D))


def _mm_kernel(x_ref, w_ref, o_ref):
    o_ref[...] = jnp.dot(x_ref[...], w_ref[...], preferred_element_type=F32).astype(o_ref.dtype)


def _matmul(x, w_all, layer, tn):
    T, K = x.shape
    N = w_all.shape[2]
    tm = _tile(T, 1024)
    return pl.pallas_call(
        _mm_kernel,
        grid=(T // tm, N // tn),
        in_specs=[pl.BlockSpec((tm, K), lambda i, j: (i, 0)),
                  pl.BlockSpec((None, K, tn), lambda i, j: (layer, 0, j))],
        out_specs=pl.BlockSpec((tm, tn), lambda i, j: (i, j)),
        out_shape=jax.ShapeDtypeStruct((T, N), BF16),
        compiler_params=_params(("parallel", "arbitrary"), ((tm, K), BF16, 2), ((K, tn), BF16, 2),
                                ((tm, tn), BF16, 2), ((tm, tn), F32, 1)),
        name="proj",
    )(x, w_all)


def _ret_in_kernel(x_ref, w_ref, cos_ref, sin_ref, o_ref, *, sub):
    j = pl.program_id(1)
    tm = x_ref.shape[0]
    half = RET_DK // 2
    v_tile = 2
    gate_tile = 4

    def blocks(finish):
        for r0 in range(0, tm, sub):
            rows = slice(r0, r0 + sub)
            finish(jnp.dot(x_ref[rows, :], w_ref[...], preferred_element_type=F32), rows)

    def rotary(scale):
        def finish(a, rows):
            cs = cos_ref[rows, :]
            sn = sin_ref[rows, :]
            if scale != 1.0:
                cs, sn = cs * scale, sn * scale
            for h in range(RET_HEADS):
                c0 = h * RET_DK
                x1, x2 = a[:, c0:c0 + half], a[:, c0 + half:c0 + RET_DK]
                o_ref[rows, c0:c0 + half] = (x1 * cs - x2 * sn).astype(o_ref.dtype)
                o_ref[rows, c0 + half:c0 + RET_DK] = (x2 * cs + x1 * sn).astype(o_ref.dtype)
        return finish

    def plain(a, rows):
        o_ref[rows, :] = a.astype(o_ref.dtype)

    def gate(a, rows):
        o_ref[rows, :] = jax.nn.silu(a).astype(o_ref.dtype)

    @pl.when(j == 0)
    def _():
        blocks(rotary(1.0))

    @pl.when(j == 1)
    def _():
        blocks(rotary(RET_DK ** -0.5))

    @pl.when(jnp.logical_and(j >= v_tile, j < gate_tile))
    def _():
        blocks(plain)

    @pl.when(j >= gate_tile)
    def _():
        blocks(gate)


def _ret_in_proj(x, w_all, layer, cos_t, sin_t, sub_pref):
    T, K = x.shape
    tn = RET_QK_WIDTH
    tm = _tile(T, 1024)
    half = RET_DK // 2
    return pl.pallas_call(
        functools.partial(_ret_in_kernel, sub=_tile(tm, sub_pref)),
        grid=(T // tm, RET_IN_WIDTH // tn),
        in_specs=[pl.BlockSpec((tm, K), lambda i, j: (i, 0)),
                  pl.BlockSpec((None, K, tn), lambda i, j: (layer, 0, j)),
                  pl.BlockSpec((tm, half), lambda i, j: (i, 0)),
                  pl.BlockSpec((tm, half), lambda i, j: (i, 0))],
        out_specs=pl.BlockSpec((tm, tn), lambda i, j: (i, j)),
        out_shape=jax.ShapeDtypeStruct((T, RET_IN_WIDTH), BF16),
        compiler_params=_params(("parallel", "arbitrary"), ((tm, K), BF16, 2), ((K, tn), BF16, 2),
                                ((tm, tn), BF16, 2), ((tm, half), F32, 4), ((tm // 4, tn), F32, 4)),
        name="ret_in_proj",
    )(x, w_all, cos_t, sin_t)


def _mla_prep_kernel(cq_ref, ckv_ref, kpe_ref, m_ref, qg_ref, kvg_ref, wq_ref, wkv_ref,
                     q_ref, k_ref, v_ref):
    scale = (MLA_NOPE + MLA_ROPE) ** -0.5
    m = m_ref[...]
    cqn = _rmsnorm_rows(cq_ref[...].astype(F32), qg_ref[...]).astype(BF16)
    q = jnp.dot(cqn, wq_ref[...], preferred_element_type=F32)
    for h in range(MLA_HEADS):
        c0 = h * MLA_QK_PAD
        q_ref[:, c0:c0 + MLA_NOPE] = (q[:, c0:c0 + MLA_NOPE] * scale).astype(BF16)
        y = q[:, c0 + MLA_NOPE:c0 + MLA_QK_PAD] * m
        y = y + pltpu.roll(y, MLA_ROPE, axis=1)
        q_ref[:, c0 + MLA_NOPE:c0 + MLA_QK_PAD] = (y * scale).astype(BF16)
    ckvn = _rmsnorm_rows(ckv_ref[...].astype(F32), kvg_ref[...]).astype(BF16)
    kv = jnp.dot(ckvn, wkv_ref[...], preferred_element_type=F32)
    yk = kpe_ref[...].astype(F32) * m
    yk = yk + pltpu.roll(yk, MLA_ROPE, axis=1)
    lane = lax.broadcasted_iota(jnp.int32, yk.shape, 1)
    k_rot = jnp.where(lane < MLA_ROPE, yk, 0.0).astype(BF16)
    for h in range(MLA_HEADS):
        c0 = h * MLA_QK_PAD
        k_ref[:, c0:c0 + MLA_NOPE] = kv[:, h * MLA_NOPE:(h + 1) * MLA_NOPE].astype(BF16)
        k_ref[:, c0 + MLA_NOPE:c0 + MLA_QK_PAD] = k_rot
    v_ref[...] = kv[:, MLA_HEADS * MLA_NOPE:].astype(BF16)


def _mla_prep(z, m_tab, q_norm_g, kv_norm_g, wq, wkv, tm_pref):
    T = z.shape[0]
    tm = _tile(T, tm_pref)
    qkw = MLA_HEADS * MLA_QK_PAD
    u_v = 2 * SGU_WIDTH
    return pl.pallas_call(
        _mla_prep_kernel,
        grid=(T // tm,),
        in_specs=[
            pl.BlockSpec((tm, Q_LORA), lambda i: (i, u_v // Q_LORA)),
            pl.BlockSpec((tm, KV_LORA), lambda i: (i, (u_v + Q_LORA) // KV_LORA)),
            pl.BlockSpec((tm, 2 * MLA_ROPE), lambda i: (i, (u_v + Q_LORA + KV_LORA) // (2 * MLA_ROPE))),
            pl.BlockSpec((tm, V7X_LANES), lambda i: (i, 0)),
            pl.BlockSpec((1, Q_LORA), lambda i: (0, 0)),
            pl.BlockSpec((1, KV_LORA), lambda i: (0, 0)),
            _const_spec((Q_LORA, qkw), lambda i: (0, 0)),
            _const_spec((KV_LORA, qkw), lambda i: (0, 0)),
        ],
        out_specs=[pl.BlockSpec((tm, qkw), lambda i: (i, 0)),
                   pl.BlockSpec((tm, qkw), lambda i: (i, 0)),
                   pl.BlockSpec((tm, MLA_OUT), lambda i: (i, 0))],
        out_shape=[jax.ShapeDtypeStruct((T, qkw), BF16), jax.ShapeDtypeStruct((T, qkw), BF16),
                   jax.ShapeDtypeStruct((T, MLA_OUT), BF16)],
        compiler_params=_params(("parallel",), ((tm, qkw), BF16, 8), ((tm, qkw), F32, 3),
                                ((Q_LORA + KV_LORA, qkw), BF16, 1)),
        name="mla_prep",
    )(z, z, z, m_tab, q_norm_g.reshape(1, -1), kv_norm_g.reshape(1, -1), wq, wkv)


def _mla_attn_kernel(q_ref, k_ref, v_ref, o_ref, va_ref, *, tq, rowsum_on_mxu):
    S = q_ref.shape[0]
    r = lax.broadcasted_iota(jnp.int32, (tq, tq), 0) // CHUNK
    c = lax.broadcasted_iota(jnp.int32, (tq, tq), 1) // CHUNK
    bias = jnp.where(c <= r, 0.0, -1e30).astype(F32)
    if rowsum_on_mxu:
        va_ref[:, 0:MLA_VDIM] = v_ref[...]
        va_ref[:, MLA_VDIM:] = jnp.ones((S, MLA_VDIM), BF16)
    for qi in range(S // tq):
        past = qi * tq
        q = q_ref[past:past + tq, :]
        s = lax.dot_general(q, k_ref[0:past + tq, :], (((1,), (1,)), ((), ())),
                            preferred_element_type=F32)
        s_diag = s[:, past:] + bias
        s = s_diag if qi == 0 else jnp.concatenate([s[:, :past], s_diag], axis=1)
        mx = jnp.max(s, axis=-1, keepdims=True)
        p = jnp.exp(s - mx)
        if rowsum_on_mxu:
            oa = jnp.dot(p.astype(BF16), va_ref[0:past + tq, :], preferred_element_type=F32)
            o_ref[past:past + tq, :] = (oa[:, :MLA_VDIM] / oa[:, MLA_VDIM:]).astype(o_ref.dtype)
        else:
            l = jnp.sum(p, axis=-1, keepdims=True)
            o = jnp.dot(p.astype(BF16), v_ref[0:past + tq, :], preferred_element_type=F32)
            o_ref[past:past + tq, :] = (o / l).astype(o_ref.dtype)


def _mla_attn(q, k, v, B, S, rowsum_on_mxu):
    tq = _tile(S, 256, CHUNK)
    q3 = q.reshape(B, S, -1)
    k3 = k.reshape(B, S, -1)
    v3 = v.reshape(B, S, -1)
    out = pl.pallas_call(
        functools.partial(_mla_attn_kernel, tq=tq, rowsum_on_mxu=rowsum_on_mxu),
        grid=(B, MLA_HEADS),
        scratch_shapes=[pltpu.VMEM((S, 2 * MLA_VDIM), BF16)],
        in_specs=[pl.BlockSpec((None, S, MLA_QK_PAD), lambda b, h: (b, 0, h)),
                  pl.BlockSpec((None, S, MLA_QK_PAD), lambda b, h: (b, 0, h)),
                  pl.BlockSpec((None, S, MLA_VDIM), lambda b, h: (b, 0, h))],
        out_specs=pl.BlockSpec((None, S, MLA_VDIM), lambda b, h: (b, 0, h)),
        out_shape=jax.ShapeDtypeStruct((B, S, MLA_OUT), BF16),
        compiler_params=_params(("parallel", "parallel"), ((S, MLA_QK_PAD), BF16, 4),
                                ((S, MLA_VDIM), BF16, 4), ((tq, S), F32, 4)),
        name="mla_attn",
    )(q3, k3, v3)
    return out.reshape(B * S, MLA_OUT)


def _even_out_kernel(a_ref, u_ref, v_ref, h_ref, lng_ref, lnb_ref, ws_ref, bs_ref, wo_ref, g_ref,
                     o_ref, on_ref, wm_ref, bo_ref, acc_ref, *, sub):
    tm = a_ref.shape[0]
    pr = lax.broadcasted_iota(jnp.int32, (SGU_BLOCK, SGU_BLOCK), 0) // CHUNK
    pc = lax.broadcasted_iota(jnp.int32, (SGU_BLOCK, SGU_BLOCK), 1) // CHUNK
    for g in range(SGU_GROUPS):
        wm_ref[g] = jnp.where(pr >= pc, ws_ref[g], 0.0).astype(BF16)
    for r0 in range(0, tm, sub):
        rows = slice(r0, r0 + sub)
        gu = jax.nn.gelu(u_ref[rows, :].astype(F32))
        gv = jax.nn.gelu(v_ref[rows, :].astype(F32))
        mu = jnp.mean(gv, axis=-1, keepdims=True)
        xc = gv - mu
        y = xc * lax.rsqrt(jnp.mean(xc * xc, axis=-1, keepdims=True) + EPS)
        vn = (y * lng_ref[...] + lnb_ref[...]).astype(BF16)
        acc_ref[rows, :] = jnp.dot(a_ref[rows, :], wo_ref[0:MLA_OUT, :], preferred_element_type=F32)
        for g in range(SGU_GROUPS):
            bcol = bs_ref[:, g:g + 1]
            cols = slice(g * SGU_CH, (g + 1) * SGU_CH)
            for nb in range(sub // SGU_BLOCK):
                rr = slice(nb * SGU_BLOCK, (nb + 1) * SGU_BLOCK)
                s = jnp.dot(wm_ref[g], vn[rr, cols], preferred_element_type=F32) + bcol
                bo_ref[r0 + nb * SGU_BLOCK:r0 + (nb + 1) * SGU_BLOCK, cols] = (gu[rr, cols] * s).astype(BF16)
        acc = acc_ref[rows, :] + jnp.dot(bo_ref[rows, :], wo_ref[MLA_OUT:, :], preferred_element_type=F32)
        hnew = h_ref[rows, :] + acc
        o_ref[rows, :] = hnew
        on_ref[rows, :] = _rmsnorm_rows(hnew, g_ref[...]).astype(on_ref.dtype)


def _even_out(a, z, h, ln_g, ln_b, w_s, b_s, wo_all, layer, g_next, sub_pref):
    T, D = h.shape
    tm = _tile(T, 512, SGU_BLOCK)
    sub = _tile(tm, sub_pref, SGU_BLOCK)
    row = lambda i: (i, 0)
    fix = lambda i: (0, 0)
    return pl.pallas_call(
        functools.partial(_even_out_kernel, sub=sub),
        grid=(T // tm,),
        in_specs=[pl.BlockSpec((tm, MLA_OUT), row),
                  pl.BlockSpec((tm, SGU_WIDTH), lambda i: (i, 0)),
                  pl.BlockSpec((tm, SGU_WIDTH), lambda i: (i, 1)),
                  pl.BlockSpec((tm, D), row),
                  pl.BlockSpec((1, SGU_WIDTH), fix),
                  pl.BlockSpec((1, SGU_WIDTH), fix),
                  _const_spec((SGU_GROUPS, SGU_BLOCK, SGU_BLOCK), lambda i: (0, 0, 0)),
                  pl.BlockSpec((SGU_BLOCK, SGU_GROUPS), fix),
                  _const_spec((None, MLA_OUT + SGU_WIDTH, D), lambda i: (layer, 0, 0)),
                  pl.BlockSpec((1, D), fix)],
        out_specs=[pl.BlockSpec((tm, D), row), pl.BlockSpec((tm, D), row)],
        out_shape=[jax.ShapeDtypeStruct((T, D), F32), jax.ShapeDtypeStruct((T, D), BF16)],
        scratch_shapes=[pltpu.VMEM((SGU_GROUPS, SGU_BLOCK, SGU_BLOCK), BF16), pltpu.VMEM((tm, SGU_WIDTH), BF16),
                        pltpu.VMEM((tm, D), F32)],
        compiler_params=_params(("parallel",), ((tm, D), F32, 7), ((tm, D), BF16, 6),
                                ((MLA_OUT + SGU_WIDTH, D), BF16, 1), ((tm, SGU_WIDTH), F32, 4)),
        name="even_out",
    )(a, z, z, h, ln_g.reshape(1, -1), ln_b.reshape(1, -1), w_s, b_s.T, wo_all, g_next.reshape(1, D))


def _retention_kernel(q_ref, k_ref, v_ref, gate_ref, dm_ref, qd_ref, kd_ref, sd_ref,
                      gng_ref, gnb_ref, o_ref, state_ref, *, L):
    S = q_ref.shape[0]
    state_ref[...] = jnp.zeros_like(state_ref)
    dm = dm_ref[...]
    qd = qd_ref[...]
    kd = kd_ref[...]
    sd = sd_ref[...]
    gng = gng_ref[...]
    gnb = gnb_ref[...]

    def step(c):
        rows = slice(c * L, (c + 1) * L)
        qb = q_ref[rows, :]
        kb = k_ref[rows, :]
        vb = v_ref[rows, :]
        sc = lax.dot_general(qb, kb, (((1,), (1,)), ((), ())), preferred_element_type=F32) * dm
        intra = jnp.dot(sc.astype(BF16), vb, preferred_element_type=F32)
        st = state_ref[...]
        cross = jnp.dot(qb, st.astype(BF16), preferred_element_type=F32) * qd
        kz = (kb.astype(F32) * kd).astype(BF16)
        state_ref[...] = st * sd + lax.dot_general(kz, vb, (((0,), (0,)), ((), ())),
                                                   preferred_element_type=F32)
        out = intra + cross
        mu = jnp.mean(out, axis=-1, keepdims=True)
        xc = out - mu
        y = xc * lax.rsqrt(jnp.mean(xc * xc, axis=-1, keepdims=True) + EPS)
        y = y * gng + gnb
        o_ref[rows, :] = (y * gate_ref[rows, :].astype(F32)).astype(o_ref.dtype)

    for c in range(S // L):
        step(c)


def _retention_tables(L):
    log_g = jnp.log1p(-(2.0 ** (-5.0 - jnp.arange(RET_HEADS, dtype=F32))))
    idx = jnp.arange(L, dtype=F32)
    chunk = jnp.arange(L) // CHUNK
    decay = jnp.exp(log_g[:, None, None] * jnp.abs(idx[:, None] - idx[None, :]))
    dm = jnp.where(chunk[None, :, None] >= chunk[None, None, :], decay, 0.0)
    qd = jnp.exp(log_g[:, None] * (idx + 1.0)[None, :])[:, :, None]
    kd = jnp.exp(log_g[:, None] * (L - 1 - idx)[None, :])[:, :, None]
    sd = jnp.broadcast_to(jnp.exp(log_g * L)[:, None, None], (RET_HEADS, 1, RET_DV))
    return dm, qd, kd, sd


def _retention(z, gn_g, gn_b, B, S):
    L = _tile(S, RET_SUPER, CHUNK)
    z3 = z.reshape(B, S, RET_IN_WIDTH)
    dm, qd, kd, sd = _retention_tables(L)
    qk_blocks = RET_QK_WIDTH // RET_DK
    v_first = 2 * RET_QK_WIDTH // RET_DV
    g_first = v_first + RET_V_WIDTH // RET_DV
    out = pl.pallas_call(
        functools.partial(_retention_kernel, L=L),
        grid=(B, RET_HEADS),
        in_specs=[pl.BlockSpec((None, S, RET_DK), lambda b, h: (b, 0, h)),
                  pl.BlockSpec((None, S, RET_DK), lambda b, h: (b, 0, qk_blocks + h)),
                  pl.BlockSpec((None, S, RET_DV), lambda b, h: (b, 0, v_first + h)),
                  pl.BlockSpec((None, S, RET_DV), lambda b, h: (b, 0, g_first + h)),
                  pl.BlockSpec((None, L, L), lambda b, h: (h, 0, 0)),
                  pl.BlockSpec((None, L, 1), lambda b, h: (h, 0, 0)),
                  pl.BlockSpec((None, L, 1), lambda b, h: (h, 0, 0)),
                  pl.BlockSpec((None, 1, RET_DV), lambda b, h: (h, 0, 0)),
                  pl.BlockSpec((1, RET_DV), lambda b, h: (0, h)),
                  pl.BlockSpec((1, RET_DV), lambda b, h: (0, h))],
        out_specs=pl.BlockSpec((None, S, RET_DV), lambda b, h: (b, 0, h)),
        out_shape=jax.ShapeDtypeStruct((B, S, RET_V_WIDTH), BF16),
        scratch_shapes=[pltpu.VMEM((RET_DK, RET_DV), F32)],
        compiler_params=_params(("parallel", "arbitrary"), ((S, RET_DK), BF16, 4), ((S, RET_DV), BF16, 6),
                                ((L, L), F32, 4), ((L, V7X_LANES), F32, 4),
                                ((RET_DK, RET_DV), F32, 4), ((L, RET_DV), F32, 8)),
        name="retention",
    )(z3, z3, z3, z3, dm, qd, kd, sd, gn_g.reshape(1, -1), gn_b.reshape(1, -1))
    return out.reshape(B * S, RET_V_WIDTH)


def _res_mm_kernel(x_ref, w_ref, h_ref, g_ref, o_ref, on_ref, *, sub):
    for r0 in range(0, x_ref.shape[0], sub):
        rows = slice(r0, r0 + sub)
        hnew = h_ref[rows, :] + jnp.dot(x_ref[rows, :], w_ref[...], preferred_element_type=F32)
        o_ref[rows, :] = hnew
        on_ref[rows, :] = _rmsnorm_rows(hnew, g_ref[...]).astype(on_ref.dtype)


def _res_matmul(x, w_all, layer, h, g_next, tm_pref, sub_pref):
    T, K = x.shape
    D = h.shape[1]
    tm = _tile(T, tm_pref)
    row = lambda i: (i, 0)
    fix = lambda i: (0, 0)
    return pl.pallas_call(
        functools.partial(_res_mm_kernel, sub=_tile(tm, sub_pref)),
        grid=(T // tm,),
        in_specs=[pl.BlockSpec((tm, K), row), _const_spec((None, K, D), lambda i: (layer, 0, 0)),
                  pl.BlockSpec((tm, D), row), pl.BlockSpec((1, D), fix)],
        out_specs=[pl.BlockSpec((tm, D), row), pl.BlockSpec((tm, D), row)],
        out_shape=[jax.ShapeDtypeStruct((T, D), F32), jax.ShapeDtypeStruct((T, D), BF16)],
        compiler_params=_params(("parallel",), ((tm, K), BF16, 2), ((K, D), BF16, 1),
                                ((tm, D), F32, 6), ((tm, D), BF16, 2)),
        name="res_proj",
    )(x, w_all, h, g_next.reshape(1, D))


def _ffn_up_kernel(x_ref, wg_ref, wv_ref, cwg_ref, cwv_ref, cbg_ref, cbv_ref,
                   pwg_ref, pwv_ref, pbg_ref, pbv_ref, o_hbm,
                   wb_ref, e_ref, el_ref, stage_ref, sem, *, tiles_per_seq, ni, n_tiles, sub):
    s = pl.program_id(0)
    tm = x_ref.shape[0]
    n_slots, _, tn = stage_ref.shape
    halo = V7X_SUBLANES
    lanes = V7X_LANES
    nslab = tn // lanes
    lead = tm - sub

    def out_copy(t):
        slot = t % n_slots
        return pltpu.make_async_copy(
            stage_ref.at[slot],
            o_hbm.at[pl.ds((t % ni) * tm, tm), pl.ds((t // ni) * tn, tn)],
            sem.at[slot])

    def conv(src_ref, k, r0, cw_ref, cb_ref, kc):
        cols = slice(kc * lanes, (kc + 1) * lanes)
        c = cb_ref[:, cols]
        for j in range(CONV_WIDTH):
            off = halo + r0 - (CONV_WIDTH - 1) + j
            c = c + src_ref[k, off:off + sub, :] * cw_ref[j:j + 1, cols]
        return c

    def finish(src_ref, r0, row0, slot, cwg, cwv, cbg, cbv):
        for k in range(nslab):
            gate = conv(src_ref, k, r0, cwg, cbg, k)
            val = conv(src_ref, k + nslab, r0, cwv, cbv, k)
            stage_ref[slot, row0:row0 + sub, k * lanes:(k + 1) * lanes] = (
                jax.nn.gelu(gate) * val).astype(stage_ref.dtype)

    def multiply(dst_ref, x0, r0):
        a = jnp.dot(x_ref[x0:x0 + sub, :], wb_ref[...], preferred_element_type=F32)
        for k in range(2 * nslab):
            dst_ref[k, halo + r0:halo + r0 + sub, :] = a[:, k * lanes:(k + 1) * lanes]

    def finish_previous_last_block():
        finish(el_ref, 0, lead, (s - 1) % n_slots, pwg_ref, pwv_ref, pbg_ref, pbv_ref)

    def tile(after_a_tile):
        cur = s % n_slots
        if after_a_tile:
            starts_sequence = (s % tiles_per_seq) == 0
            e_ref[:, 0:halo, :] = jnp.where(starts_sequence, 0.0, el_ref[:, sub:sub + halo, :])
            finish_previous_last_block()
        else:
            e_ref[:, 0:halo, :] = jnp.zeros((2 * nslab, halo, lanes), F32)
        for r0 in range(0, lead, sub):
            multiply(e_ref, r0, r0)
            finish(e_ref, r0, r0, cur, cwg_ref, cwv_ref, cbg_ref, cbv_ref)
        el_ref[:, 0:halo, :] = e_ref[:, lead:lead + halo, :]
        multiply(el_ref, lead, 0)

    @pl.when(s >= n_slots)
    def _():
        out_copy(s - n_slots).wait()

    @pl.when(jnp.logical_and(s % ni == 0, s < n_tiles))
    def _():
        wb_ref[:, 0:tn] = wg_ref[...].astype(BF16)
        wb_ref[:, tn:2 * tn] = wv_ref[...].astype(BF16)

    @pl.when(s == 0)
    def _():
        tile(False)

    @pl.when(jnp.logical_and(s > 0, s < n_tiles))
    def _():
        tile(True)

    @pl.when(s == n_tiles)
    def _():
        finish_previous_last_block()

    @pl.when(s > 0)
    def _():
        out_copy(s - 1).start()

    @pl.when(s == n_tiles)
    def _():
        for t in range(max(n_tiles - n_slots + 1, 0), n_tiles):
            out_copy(t).wait()


def _ffn_up(hn, w_up_all, conv_w_all, conv_b_all, layer, S, tm_pref, sub_pref, tn):
    T, D = hn.shape
    tm = _tile(S, tm_pref)
    sub = _tile(tm, sub_pref)
    assert tm // sub >= 2, "the last row block is finished one step late; it cannot be the only block"
    nj = D_FF // tn
    ni = T // tm
    n_tiles = nj * ni
    n_slots = 3
    halo = V7X_SUBLANES
    nslab2 = 2 * tn // V7X_LANES
    conv_b3 = conv_b_all.reshape(conv_b_all.shape[0], 1, 2 * D_FF)
    cur = lambda s: jnp.minimum(s, n_tiles - 1) // ni
    prev = lambda s: jnp.maximum(s - 1, 0) // ni
    gate_cur = lambda s: (layer, 0, cur(s))
    val_cur = lambda s: (layer, 0, nj + cur(s))
    gate_prev = lambda s: (layer, 0, prev(s))
    val_prev = lambda s: (layer, 0, nj + prev(s))
    return pl.pallas_call(
        functools.partial(_ffn_up_kernel, tiles_per_seq=S // tm, ni=ni, n_tiles=n_tiles, sub=sub),
        grid=(n_tiles + 1,),
        in_specs=[pl.BlockSpec((tm, D), lambda s: (jnp.minimum(s, n_tiles - 1) % ni, 0)),
                  pl.BlockSpec((None, D, tn), gate_cur),
                  pl.BlockSpec((None, D, tn), val_cur),
                  pl.BlockSpec((None, CONV_WIDTH, tn), gate_cur),
                  pl.BlockSpec((None, CONV_WIDTH, tn), val_cur),
                  pl.BlockSpec((None, 1, tn), gate_cur),
                  pl.BlockSpec((None, 1, tn), val_cur),
                  pl.BlockSpec((None, CONV_WIDTH, tn), gate_prev),
                  pl.BlockSpec((None, CONV_WIDTH, tn), val_prev),
                  pl.BlockSpec((None, 1, tn), gate_prev),
                  pl.BlockSpec((None, 1, tn), val_prev)],
        out_specs=pl.BlockSpec(memory_space=pl.ANY),
        out_shape=jax.ShapeDtypeStruct((T, D_FF), BF16),
        scratch_shapes=[pltpu.VMEM((D, 2 * tn), BF16),
                        pltpu.VMEM((nslab2, tm - sub + halo, V7X_LANES), F32),
                        pltpu.VMEM((nslab2, sub + halo, V7X_LANES), F32),
                        pltpu.VMEM((n_slots, tm, tn), BF16),
                        pltpu.SemaphoreType.DMA((n_slots,))],
        compiler_params=_params(("arbitrary",), ((tm, D), BF16, 2), ((D, tn), F32, 4),
                                ((D, 2 * tn), BF16, 1), ((tm + 2 * halo, 2 * tn), F32, 1),
                                ((sub, 2 * tn), F32, 1), ((n_slots, tm, tn), BF16, 1)),
        name="ffn_up",
    )(hn, w_up_all, w_up_all, conv_w_all, conv_w_all, conv_b3, conv_b3,
      conv_w_all, conv_w_all, conv_b3, conv_b3)


def _ple_kernel(hn_ref, wg_ref, p_ref, wu_ref, h_ref, g_ref, *out_refs, sub):
    on_ref = out_refs[-1]
    for r0 in range(0, hn_ref.shape[0], sub):
        rows = slice(r0, r0 + sub)
        gate = jax.nn.sigmoid(jnp.dot(hn_ref[rows, :], wg_ref[...], preferred_element_type=F32))
        up = jnp.dot(p_ref[rows, :].astype(BF16), wu_ref[...], preferred_element_type=F32)
        hnew = h_ref[rows, :] + up * gate
        if len(out_refs) == 2:
            out_refs[0][rows, :] = hnew
        on_ref[rows, :] = _rmsnorm_rows(hnew, g_ref[...]).astype(on_ref.dtype)


def _ple(hn, wg_all, p_all, layer, wu_all, h, g_next, last, sub_pref):
    T, D = h.shape
    tm = _tile(T, 512)
    row = lambda i: (i, 0)
    fix = lambda i: (0, 0)
    out_specs = [pl.BlockSpec((tm, D), row), pl.BlockSpec((tm, D), row)]
    out_shape = [jax.ShapeDtypeStruct((T, D), F32), jax.ShapeDtypeStruct((T, D), BF16)]
    if last:
        out_specs, out_shape = out_specs[:1], out_shape[:1]
    return pl.pallas_call(
        functools.partial(_ple_kernel, sub=_tile(tm, sub_pref)),
        grid=(T // tm,),
        in_specs=[pl.BlockSpec((tm, D), row),
                  _const_spec((None, D, D), lambda i: (layer, 0, 0)),
                  pl.BlockSpec((None, tm, PLE_DIM), lambda i: (layer, i, 0)),
                  _const_spec((None, PLE_DIM, D), lambda i: (layer, 0, 0)),
                  pl.BlockSpec((tm, D), row),
                  pl.BlockSpec((1, D), fix)],
        out_specs=out_specs,
        out_shape=out_shape,
        compiler_params=_params(("parallel",), ((tm, D), BF16, 2), ((D, D), BF16, 1), ((PLE_DIM, D), BF16, 1),
                                ((tm, PLE_DIM), F32, 2), ((tm, D), F32, 8)),
        name="ple",
    )(hn, wg_all, p_all, wu_all, h, g_next.reshape(1, D))


def _rot_cols(w):
    half = MLA_ROPE // 2
    return jnp.concatenate([-w[..., half:], w[..., :half]], axis=-1)


def _even_in_weight(w):
    cq = w[..., :Q_LORA]
    ckv = w[..., Q_LORA:Q_LORA + KV_LORA]
    kpe = w[..., Q_LORA + KV_LORA:Q_LORA + KV_LORA + MLA_ROPE]
    uv = w[..., Q_LORA + KV_LORA + MLA_ROPE:]
    used = 2 * SGU_WIDTH + Q_LORA + KV_LORA + 2 * MLA_ROPE
    pad = jnp.zeros(w.shape[:-1] + (EVEN_IN_PAD - used,), w.dtype)
    return jnp.concatenate([uv, cq, ckv, kpe, _rot_cols(kpe), pad], axis=-1).astype(BF16)


def _q_up_weight(w):
    w = w.reshape(Q_LORA, MLA_HEADS, MLA_NOPE + MLA_ROPE)
    rope = w[..., MLA_NOPE:]
    w = jnp.concatenate([w[..., :MLA_NOPE], rope, _rot_cols(rope)], axis=-1)
    return w.reshape(Q_LORA, MLA_HEADS * MLA_QK_PAD).astype(BF16)


def _kv_up_weight(w):
    w = w.reshape(KV_LORA, MLA_HEADS, MLA_NOPE + MLA_VDIM)
    k = w[..., :MLA_NOPE].reshape(KV_LORA, -1)
    v = w[..., MLA_NOPE:].reshape(KV_LORA, -1)
    return jnp.concatenate([k, v], axis=1).astype(BF16)


def kernel(x, p, positions, mix_norm_g, even_w_in, mla_q_norm_g, mla_w_q_up, mla_kv_norm_g, mla_w_kv_up,
           sgu_ln_g, sgu_ln_b, sgu_w_s, sgu_b_s, even_w_out, ret_w_in, ret_gn_g, ret_gn_b, ret_w_out,
           ffn_norm_g, ffn_w_up, ffn_conv_w, ffn_conv_b, ffn_w_down, ple_norm_g, ple_w_gate, ple_w_up,
           final_norm_g):
    B, S, D = x.shape
    depth = p.shape[0]
    T = B * S
    assert D == D_MODEL and S % SGU_BLOCK == 0
    cos_t, sin_t, m_tab = _rope_tables(positions)
    h = x.reshape(T, D)
    p_all = p.reshape(depth, T, PLE_DIM)
    hn = _init_norm(h, mix_norm_g[0])
    even_in_b = _even_in_weight(even_w_in)
    even_out_b = even_w_out.astype(BF16)
    ret_in_b = ret_w_in.astype(BF16)
    ret_out_b = ret_w_out.astype(BF16)
    ffn_down_b = ffn_w_down.astype(BF16)
    ple_gate_b = ple_w_gate.astype(BF16)
    ple_up_b = ple_w_up.astype(BF16)
    for i in range(depth):
        j = i // 2
        if i % 2 == 0:
            z = _matmul(hn, even_in_b, j, 1024)
            q, k, v = _mla_prep(z, m_tab, mla_q_norm_g[j], mla_kv_norm_g[j],
                                _q_up_weight(mla_w_q_up[j]), _kv_up_weight(mla_w_kv_up[j]), 1024)
            a = _mla_attn(q, k, v, B, S, rowsum_on_mxu=True)
            h, hn = _even_out(a, z, h, sgu_ln_g[j], sgu_ln_b[j], sgu_w_s[j], sgu_b_s[j],
                              even_out_b, j, ffn_norm_g[i], 512)
        else:
            z = _ret_in_proj(hn, ret_in_b, j, cos_t, sin_t, 256)
            r = _retention(z, ret_gn_g[j], ret_gn_b[j], B, S)
            h, hn = _res_matmul(r, ret_out_b, j, h, ffn_norm_g[i], 512, 512)
        act = _ffn_up(hn, ffn_w_up, ffn_conv_w, ffn_conv_b, i, S, 2048, 1024, 512)
        h, hn = _res_matmul(act, ffn_down_b, i, h, ple_norm_g[i], 512, 512)
        if i < depth - 1:
            h, hn = _ple(hn, ple_gate_b, p_all, i, ple_up_b, h, mix_norm_g[i + 1], False, 512)
        else:
            out, = _ple(hn, ple_gate_b, p_all, i, ple_up_b, h, final_norm_g, True, 512)
    return out.reshape(B, S, D)
```

```python
import functools
import math

import jax
import jax.numpy as jnp
from jax import lax
from jax.experimental import pallas as pl
from jax.experimental.pallas import tpu as pltpu

F32 = jnp.float32
BF16 = jnp.bfloat16

D_MODEL = 2048
CHUNK = 64
EPS = 1e-6
ROPE_THETA = 10000.0
MLA_HEADS = 8
MLA_NOPE = 128
MLA_ROPE = 64
MLA_VDIM = 128
Q_LORA = 512
KV_LORA = 256
MLA_OUT = MLA_HEADS * MLA_VDIM
MLA_QK_PAD = 256
SGU_BLOCK = 128
SGU_GROUPS = 8
SGU_CH = 128
SGU_WIDTH = SGU_GROUPS * SGU_CH
EVEN_IN_PAD = 3072
RET_HEADS = 8
RET_DK = 256
RET_DV = 512
RET_QK_WIDTH = RET_HEADS * RET_DK
RET_V_WIDTH = RET_HEADS * RET_DV
RET_IN_WIDTH = 2 * RET_QK_WIDTH + 2 * RET_V_WIDTH
RET_SUPER = 256
D_FF = 5632
CONV_WIDTH = 3
PLE_DIM = 256

V7X_LANES = 128
V7X_SUBLANES = 8
V7X_VMEM_BUDGET = 58 * 1024 * 1024


def _nbytes(shape, dtype):
    return math.prod(shape) * jnp.dtype(dtype).itemsize


def _params(semantics, *buffers):
    need = sum(_nbytes(s, d) * n for s, d, n in buffers)
    return pltpu.CompilerParams(dimension_semantics=semantics,
                                vmem_limit_bytes=min(V7X_VMEM_BUDGET, need + (4 << 20)))


def _tile(n, pref, mult=V7X_SUBLANES):
    t = min(n, pref)
    while n % t or t % mult:
        t -= 1
    return t


def _const_spec(shape, index_map):
    return pl.BlockSpec(shape, index_map, pipeline_mode=pl.Buffered(1))


def _rmsnorm_rows(x, g):
    ms = jnp.mean(x * x, axis=-1, keepdims=True)
    return x * lax.rsqrt(ms + EPS) * g


def _rope_tables_kernel(pos_ref, fr_ref, fm_ref, cos_ref, sin_ref, m_ref):
    pos = pos_ref[...].astype(F32)
    ang = pos * fr_ref[...]
    cos_ref[...] = jnp.cos(ang)
    sin_ref[...] = jnp.sin(ang)
    angm = pos * fm_ref[...]
    lane = lax.broadcasted_iota(jnp.int32, angm.shape, 1)
    m_ref[...] = jnp.where(lane < MLA_ROPE, jnp.cos(angm), jnp.sin(angm))


def _rope_tables(positions):
    T = positions.size
    tm = _tile(T, 1024)
    half_r = RET_DK // 2
    half_m = MLA_ROPE // 2
    fr = (ROPE_THETA ** (-jnp.arange(half_r, dtype=F32) / half_r))[None, :]
    fm1 = ROPE_THETA ** (-jnp.arange(half_m, dtype=F32) / half_m)
    fm = jnp.tile(fm1, 4)[None, :]
    tab = jax.ShapeDtypeStruct((T, V7X_LANES), F32)
    row = pl.BlockSpec((tm, V7X_LANES), lambda i: (i, 0))
    frq = pl.BlockSpec((1, V7X_LANES), lambda i: (0, 0))
    return pl.pallas_call(
        _rope_tables_kernel,
        grid=(T // tm,),
        in_specs=[pl.BlockSpec((tm, 1), lambda i: (i, 0)), frq, frq],
        out_specs=[row, row, row],
        out_shape=[tab, tab, tab],
        compiler_params=_params(("parallel",), ((tm, V7X_LANES), F32, 16)),
        name="rope_tables",
    )(positions.reshape(T, 1), fr, fm)


def _mm_kernel(x_ref, w_ref, o_ref):
    o_ref[...] = jnp.dot(x_ref[...], w_ref[...], preferred_element_type=F32).astype(o_ref.dtype)


def _matmul(x, w_all, layer, tn):
    T, K = x.shape
    N = w_all.shape[2]
    tm = _tile(T, 1024)
    return pl.pallas_call(
        _mm_kernel,
        grid=(T // tm, N // tn),
        in_specs=[pl.BlockSpec((tm, K), lambda i, j: (i, 0)),
                  pl.BlockSpec((None, K, tn), lambda i, j: (layer, 0, j))],
        out_specs=pl.BlockSpec((tm, tn), lambda i, j: (i, j)),
        out_shape=jax.ShapeDtypeStruct((T, N), BF16),
        compiler_params=_params(("parallel", "arbitrary"), ((tm, K), BF16, 2), ((K, tn), BF16, 2),
                                ((tm, tn), BF16, 2), ((tm, tn), F32, 1)),
        name="proj",
    )(x, w_all)


def _norm_mm_kernel(x_ref, g_ref, w_ref, o_ref, xn_ref):
    @pl.when(pl.program_id(1) == 0)
    def _():
        xn_ref[...] = _rmsnorm_rows(x_ref[...], g_ref[...]).astype(xn_ref.dtype)

    o_ref[...] = jnp.dot(xn_ref[...], w_ref[...], preferred_element_type=F32).astype(o_ref.dtype)


def _norm_matmul(x, g, w_all, layer, tn):
    T, K = x.shape
    N = w_all.shape[2]
    tm = _tile(T, 1024)
    return pl.pallas_call(
        _norm_mm_kernel,
        grid=(T // tm, N // tn),
        in_specs=[pl.BlockSpec((tm, K), lambda i, j: (i, 0)),
                  pl.BlockSpec((1, K), lambda i, j: (0, 0)),
                  pl.BlockSpec((None, K, tn), lambda i, j: (layer, 0, j))],
        out_specs=pl.BlockSpec((tm, tn), lambda i, j: (i, j)),
        out_shape=jax.ShapeDtypeStruct((T, N), BF16),
        scratch_shapes=[pltpu.VMEM((tm, K), BF16)],
        compiler_params=_params(("parallel", "arbitrary"), ((tm, K), F32, 3), ((tm, K), BF16, 1),
                                ((K, tn), BF16, 2), ((tm, tn), BF16, 2), ((tm, tn), F32, 1)),
        name="norm_proj",
    )(x, g.reshape(1, K), w_all)


def _ret_in_kernel(x_ref, w_ref, cos_ref, sin_ref, o_ref, *, sub):
    j = pl.program_id(1)
    tm = x_ref.shape[0]
    half = RET_DK // 2
    v_tile = 2
    gate_tile = 4

    def blocks(finish):
        for r0 in range(0, tm, sub):
            rows = slice(r0, r0 + sub)
            finish(jnp.dot(x_ref[rows, :], w_ref[...], preferred_element_type=F32), rows)

    def rotary(scale):
        def finish(a, rows):
            cs = cos_ref[rows, :]
            sn = sin_ref[rows, :]
            if scale != 1.0:
                cs, sn = cs * scale, sn * scale
            for h in range(RET_HEADS):
                c0 = h * RET_DK
                x1, x2 = a[:, c0:c0 + half], a[:, c0 + half:c0 + RET_DK]
                o_ref[rows, c0:c0 + half] = (x1 * cs - x2 * sn).astype(o_ref.dtype)
                o_ref[rows, c0 + half:c0 + RET_DK] = (x2 * cs + x1 * sn).astype(o_ref.dtype)
        return finish

    def plain(a, rows):
        o_ref[rows, :] = a.astype(o_ref.dtype)

    def gate(a, rows):
        o_ref[rows, :] = jax.nn.silu(a).astype(o_ref.dtype)

    @pl.when(j == 0)
    def _():
        blocks(rotary(1.0))

    @pl.when(j == 1)
    def _():
        blocks(rotary(RET_DK ** -0.5))

    @pl.when(jnp.logical_and(j >= v_tile, j < gate_tile))
    def _():
        blocks(plain)

    @pl.when(j >= gate_tile)
    def _():
        blocks(gate)


def _ret_in_proj(x, w_all, layer, cos_t, sin_t, sub_pref):
    T, K = x.shape
    tn = RET_QK_WIDTH
    tm = _tile(T, 1024)
    half = RET_DK // 2
    return pl.pallas_call(
        functools.partial(_ret_in_kernel, sub=_tile(tm, sub_pref)),
        grid=(T // tm, RET_IN_WIDTH // tn),
        in_specs=[pl.BlockSpec((tm, K), lambda i, j: (i, 0)),
                  pl.BlockSpec((None, K, tn), lambda i, j: (layer, 0, j)),
                  pl.BlockSpec((tm, half), lambda i, j: (i, 0)),
                  pl.BlockSpec((tm, half), lambda i, j: (i, 0))],
        out_specs=pl.BlockSpec((tm, tn), lambda i, j: (i, j)),
        out_shape=jax.ShapeDtypeStruct((T, RET_IN_WIDTH), BF16),
        compiler_params=_params(("parallel", "arbitrary"), ((tm, K), BF16, 2), ((K, tn), BF16, 2),
                                ((tm, tn), BF16, 2), ((tm, half), F32, 4), ((tm // 4, tn), F32, 4)),
        name="ret_in_proj",
    )(x, w_all, cos_t, sin_t)


def _mla_prep_kernel(cq_ref, ckv_ref, kpe_ref, m_ref, qg_ref, kvg_ref, wq_ref, wkv_ref,
                     q_ref, k_ref, v_ref):
    scale = (MLA_NOPE + MLA_ROPE) ** -0.5
    m = m_ref[...]
    cqn = _rmsnorm_rows(cq_ref[...].astype(F32), qg_ref[...]).astype(BF16)
    q = jnp.dot(cqn, wq_ref[...], preferred_element_type=F32)
    for h in range(MLA_HEADS):
        c0 = h * MLA_QK_PAD
        q_ref[:, c0:c0 + MLA_NOPE] = (q[:, c0:c0 + MLA_NOPE] * scale).astype(BF16)
        y = q[:, c0 + MLA_NOPE:c0 + MLA_QK_PAD] * m
        y = y + pltpu.roll(y, MLA_ROPE, axis=1)
        q_ref[:, c0 + MLA_NOPE:c0 + MLA_QK_PAD] = (y * scale).astype(BF16)
    ckvn = _rmsnorm_rows(ckv_ref[...].astype(F32), kvg_ref[...]).astype(BF16)
    kv = jnp.dot(ckvn, wkv_ref[...], preferred_element_type=F32)
    yk = kpe_ref[...].astype(F32) * m
    yk = yk + pltpu.roll(yk, MLA_ROPE, axis=1)
    lane = lax.broadcasted_iota(jnp.int32, yk.shape, 1)
    k_rot = jnp.where(lane < MLA_ROPE, yk, 0.0).astype(BF16)
    for h in range(MLA_HEADS):
        c0 = h * MLA_QK_PAD
        k_ref[:, c0:c0 + MLA_NOPE] = kv[:, h * MLA_NOPE:(h + 1) * MLA_NOPE].astype(BF16)
        k_ref[:, c0 + MLA_NOPE:c0 + MLA_QK_PAD] = k_rot
    v_ref[...] = kv[:, MLA_HEADS * MLA_NOPE:].astype(BF16)


def _mla_prep(z, m_tab, q_norm_g, kv_norm_g, wq, wkv, tm_pref):
    T = z.shape[0]
    tm = _tile(T, tm_pref)
    qkw = MLA_HEADS * MLA_QK_PAD
    u_v = 2 * SGU_WIDTH
    return pl.pallas_call(
        _mla_prep_kernel,
        grid=(T // tm,),
        in_specs=[
            pl.BlockSpec((tm, Q_LORA), lambda i: (i, u_v // Q_LORA)),
            pl.BlockSpec((tm, KV_LORA), lambda i: (i, (u_v + Q_LORA) // KV_LORA)),
            pl.BlockSpec((tm, 2 * MLA_ROPE), lambda i: (i, (u_v + Q_LORA + KV_LORA) // (2 * MLA_ROPE))),
            pl.BlockSpec((tm, V7X_LANES), lambda i: (i, 0)),
            pl.BlockSpec((1, Q_LORA), lambda i: (0, 0)),
            pl.BlockSpec((1, KV_LORA), lambda i: (0, 0)),
            _const_spec((Q_LORA, qkw), lambda i: (0, 0)),
            _const_spec((KV_LORA, qkw), lambda i: (0, 0)),
        ],
        out_specs=[pl.BlockSpec((tm, qkw), lambda i: (i, 0)),
                   pl.BlockSpec((tm, qkw), lambda i: (i, 0)),
                   pl.BlockSpec((tm, MLA_OUT), lambda i: (i, 0))],
        out_shape=[jax.ShapeDtypeStruct((T, qkw), BF16), jax.ShapeDtypeStruct((T, qkw), BF16),
                   jax.ShapeDtypeStruct((T, MLA_OUT), BF16)],
        compiler_params=_params(("parallel",), ((tm, qkw), BF16, 8), ((tm, qkw), F32, 3),
                                ((Q_LORA + KV_LORA, qkw), BF16, 1)),
        name="mla_prep",
    )(z, z, z, m_tab, q_norm_g.reshape(1, -1), kv_norm_g.reshape(1, -1), wq, wkv)


def _mla_attn_kernel(q_ref, k_ref, v_ref, o_ref, va_ref, *, tq, rowsum_on_mxu):
    S = q_ref.shape[0]
    r = lax.broadcasted_iota(jnp.int32, (tq, tq), 0) // CHUNK
    c = lax.broadcasted_iota(jnp.int32, (tq, tq), 1) // CHUNK
    bias = jnp.where(c <= r, 0.0, -1e30).astype(F32)
    if rowsum_on_mxu:
        va_ref[:, 0:MLA_VDIM] = v_ref[...]
        va_ref[:, MLA_VDIM:] = jnp.ones((S, MLA_VDIM), BF16)
    for qi in range(S // tq):
        past = qi * tq
        q = q_ref[past:past + tq, :]
        s = lax.dot_general(q, k_ref[0:past + tq, :], (((1,), (1,)), ((), ())),
                            preferred_element_type=F32)
        s_diag = s[:, past:] + bias
        s = s_diag if qi == 0 else jnp.concatenate([s[:, :past], s_diag], axis=1)
        mx = jnp.max(s, axis=-1, keepdims=True)
        p = jnp.exp(s - mx)
        if rowsum_on_mxu:
            oa = jnp.dot(p.astype(BF16), va_ref[0:past + tq, :], preferred_element_type=F32)
            o_ref[past:past + tq, :] = (oa[:, :MLA_VDIM] / oa[:, MLA_VDIM:]).astype(o_ref.dtype)
        else:
            l = jnp.sum(p, axis=-1, keepdims=True)
            o = jnp.dot(p.astype(BF16), v_ref[0:past + tq, :], preferred_element_type=F32)
            o_ref[past:past + tq, :] = (o / l).astype(o_ref.dtype)


def _mla_attn(q, k, v, B, S, rowsum_on_mxu):
    tq = _tile(S, 256, CHUNK)
    q3 = q.reshape(B, S, -1)
    k3 = k.reshape(B, S, -1)
    v3 = v.reshape(B, S, -1)
    out = pl.pallas_call(
        functools.partial(_mla_attn_kernel, tq=tq, rowsum_on_mxu=rowsum_on_mxu),
        grid=(B, MLA_HEADS),
        scratch_shapes=[pltpu.VMEM((S, 2 * MLA_VDIM), BF16)],
        in_specs=[pl.BlockSpec((None, S, MLA_QK_PAD), lambda b, h: (b, 0, h)),
                  pl.BlockSpec((None, S, MLA_QK_PAD), lambda b, h: (b, 0, h)),
                  pl.BlockSpec((None, S, MLA_VDIM), lambda b, h: (b, 0, h))],
        out_specs=pl.BlockSpec((None, S, MLA_VDIM), lambda b, h: (b, 0, h)),
        out_shape=jax.ShapeDtypeStruct((B, S, MLA_OUT), BF16),
        compiler_params=_params(("parallel", "parallel"), ((S, MLA_QK_PAD), BF16, 4),
                                ((S, MLA_VDIM), BF16, 4), ((tq, S), F32, 4)),
        name="mla_attn",
    )(q3, k3, v3)
    return out.reshape(B * S, MLA_OUT)


def _even_out_kernel(a_ref, u_ref, v_ref, h_ref, lng_ref, lnb_ref, ws_ref, bs_ref, wo_ref, g_ref,
                     o_ref, on_ref, wm_ref, bo_ref, acc_ref, *, sub):
    tm = a_ref.shape[0]
    pr = lax.broadcasted_iota(jnp.int32, (SGU_BLOCK, SGU_BLOCK), 0) // CHUNK
    pc = lax.broadcasted_iota(jnp.int32, (SGU_BLOCK, SGU_BLOCK), 1) // CHUNK
    for g in range(SGU_GROUPS):
        wm_ref[g] = jnp.where(pr >= pc, ws_ref[g], 0.0).astype(BF16)
    for r0 in range(0, tm, sub):
        rows = slice(r0, r0 + sub)
        gu = jax.nn.gelu(u_ref[rows, :].astype(F32))
        gv = jax.nn.gelu(v_ref[rows, :].astype(F32))
        mu = jnp.mean(gv, axis=-1, keepdims=True)
        xc = gv - mu
        y = xc * lax.rsqrt(jnp.mean(xc * xc, axis=-1, keepdims=True) + EPS)
        vn = (y * lng_ref[...] + lnb_ref[...]).astype(BF16)
        acc_ref[rows, :] = jnp.dot(a_ref[rows, :], wo_ref[0:MLA_OUT, :], preferred_element_type=F32)
        for g in range(SGU_GROUPS):
            bcol = bs_ref[:, g:g + 1]
            cols = slice(g * SGU_CH, (g + 1) * SGU_CH)
            for nb in range(sub // SGU_BLOCK):
                rr = slice(nb * SGU_BLOCK, (nb + 1) * SGU_BLOCK)
                s = jnp.dot(wm_ref[g], vn[rr, cols], preferred_element_type=F32) + bcol
                bo_ref[r0 + nb * SGU_BLOCK:r0 + (nb + 1) * SGU_BLOCK, cols] = (gu[rr, cols] * s).astype(BF16)
        acc = acc_ref[rows, :] + jnp.dot(bo_ref[rows, :], wo_ref[MLA_OUT:, :], preferred_element_type=F32)
        hnew = h_ref[rows, :] + acc
        o_ref[rows, :] = hnew
        on_ref[rows, :] = _rmsnorm_rows(hnew, g_ref[...]).astype(on_ref.dtype)


def _even_out(a, z, h, ln_g, ln_b, w_s, b_s, wo_all, layer, g_next, sub_pref):
    T, D = h.shape
    tm = _tile(T, 512, SGU_BLOCK)
    sub = _tile(tm, sub_pref, SGU_BLOCK)
    row = lambda i: (i, 0)
    fix = lambda i: (0, 0)
    return pl.pallas_call(
        functools.partial(_even_out_kernel, sub=sub),
        grid=(T // tm,),
        in_specs=[pl.BlockSpec((tm, MLA_OUT), row),
                  pl.BlockSpec((tm, SGU_WIDTH), lambda i: (i, 0)),
                  pl.BlockSpec((tm, SGU_WIDTH), lambda i: (i, 1)),
                  pl.BlockSpec((tm, D), row),
                  pl.BlockSpec((1, SGU_WIDTH), fix),
                  pl.BlockSpec((1, SGU_WIDTH), fix),
                  _const_spec((SGU_GROUPS, SGU_BLOCK, SGU_BLOCK), lambda i: (0, 0, 0)),
                  pl.BlockSpec((SGU_BLOCK, SGU_GROUPS), fix),
                  _const_spec((None, MLA_OUT + SGU_WIDTH, D), lambda i: (layer, 0, 0)),
                  pl.BlockSpec((1, D), fix)],
        out_specs=[pl.BlockSpec((tm, D), row), pl.BlockSpec((tm, D), row)],
        out_shape=[jax.ShapeDtypeStruct((T, D), F32), jax.ShapeDtypeStruct((T, D), BF16)],
        scratch_shapes=[pltpu.VMEM((SGU_GROUPS, SGU_BLOCK, SGU_BLOCK), BF16), pltpu.VMEM((tm, SGU_WIDTH), BF16),
                        pltpu.VMEM((tm, D), F32)],
        compiler_params=_params(("parallel",), ((tm, D), F32, 7), ((tm, D), BF16, 6),
                                ((MLA_OUT + SGU_WIDTH, D), BF16, 1), ((tm, SGU_WIDTH), F32, 4)),
        name="even_out",
    )(a, z, z, h, ln_g.reshape(1, -1), ln_b.reshape(1, -1), w_s, b_s.T, wo_all, g_next.reshape(1, D))


def _retention_kernel(q_ref, k_ref, v_ref, gate_ref, dm_ref, qd_ref, kd_ref, sd_ref,
                      gng_ref, gnb_ref, o_ref, state_ref, *, L):
    S = q_ref.shape[0]
    state_ref[...] = jnp.zeros_like(state_ref)
    dm = dm_ref[...]
    qd = qd_ref[...]
    kd = kd_ref[...]
    sd = sd_ref[...]
    gng = gng_ref[...]
    gnb = gnb_ref[...]

    def step(c):
        rows = slice(c * L, (c + 1) * L)
        qb = q_ref[rows, :]
        kb = k_ref[rows, :]
        vb = v_ref[rows, :]
        sc = lax.dot_general(qb, kb, (((1,), (1,)), ((), ())), preferred_element_type=F32) * dm
        intra = jnp.dot(sc.astype(BF16), vb, preferred_element_type=F32)
        st = state_ref[...]
        cross = jnp.dot(qb, st.astype(BF16), preferred_element_type=F32) * qd
        kz = (kb.astype(F32) * kd).astype(BF16)
        state_ref[...] = st * sd + lax.dot_general(kz, vb, (((0,), (0,)), ((), ())),
                                                   preferred_element_type=F32)
        out = intra + cross
        mu = jnp.mean(out, axis=-1, keepdims=True)
        xc = out - mu
        y = xc * lax.rsqrt(jnp.mean(xc * xc, axis=-1, keepdims=True) + EPS)
        y = y * gng + gnb
        o_ref[rows, :] = (y * gate_ref[rows, :].astype(F32)).astype(o_ref.dtype)

    for c in range(S // L):
        step(c)


def _retention_tables(L):
    log_g = jnp.log1p(-(2.0 ** (-5.0 - jnp.arange(RET_HEADS, dtype=F32))))
    idx = jnp.arange(L, dtype=F32)
    chunk = jnp.arange(L) // CHUNK
    decay = jnp.exp(log_g[:, None, None] * jnp.abs(idx[:, None] - idx[None, :]))
    dm = jnp.where(chunk[None, :, None] >= chunk[None, None, :], decay, 0.0)
    qd = jnp.exp(log_g[:, None] * (idx + 1.0)[None, :])[:, :, None]
    kd = jnp.exp(log_g[:, None] * (L - 1 - idx)[None, :])[:, :, None]
    sd = jnp.broadcast_to(jnp.exp(log_g * L)[:, None, None], (RET_HEADS, 1, RET_DV))
    return dm, qd, kd, sd


def _retention(z, gn_g, gn_b, B, S):
    L = _tile(S, RET_SUPER, CHUNK)
    z3 = z.reshape(B, S, RET_IN_WIDTH)
    dm, qd, kd, sd = _retention_tables(L)
    qk_blocks = RET_QK_WIDTH // RET_DK
    v_first = 2 * RET_QK_WIDTH // RET_DV
    g_first = v_first + RET_V_WIDTH // RET_DV
    out = pl.pallas_call(
        functools.partial(_retention_kernel, L=L),
        grid=(B, RET_HEADS),
        in_specs=[pl.BlockSpec((None, S, RET_DK), lambda b, h: (b, 0, h)),
                  pl.BlockSpec((None, S, RET_DK), lambda b, h: (b, 0, qk_blocks + h)),
                  pl.BlockSpec((None, S, RET_DV), lambda b, h: (b, 0, v_first + h)),
                  pl.BlockSpec((None, S, RET_DV), lambda b, h: (b, 0, g_first + h)),
                  pl.BlockSpec((None, L, L), lambda b, h: (h, 0, 0)),
                  pl.BlockSpec((None, L, 1), lambda b, h: (h, 0, 0)),
                  pl.BlockSpec((None, L, 1), lambda b, h: (h, 0, 0)),
                  pl.BlockSpec((None, 1, RET_DV), lambda b, h: (h, 0, 0)),
                  pl.BlockSpec((1, RET_DV), lambda b, h: (0, h)),
                  pl.BlockSpec((1, RET_DV), lambda b, h: (0, h))],
        out_specs=pl.BlockSpec((None, S, RET_DV), lambda b, h: (b, 0, h)),
        out_shape=jax.ShapeDtypeStruct((B, S, RET_V_WIDTH), BF16),
        scratch_shapes=[pltpu.VMEM((RET_DK, RET_DV), F32)],
        compiler_params=_params(("parallel", "arbitrary"), ((S, RET_DK), BF16, 4), ((S, RET_DV), BF16, 6),
                                ((L, L), F32, 4), ((L, V7X_LANES), F32, 4),
                                ((RET_DK, RET_DV), F32, 4), ((L, RET_DV), F32, 8)),
        name="retention",
    )(z3, z3, z3, z3, dm, qd, kd, sd, gn_g.reshape(1, -1), gn_b.reshape(1, -1))
    return out.reshape(B * S, RET_V_WIDTH)


def _res_mm_kernel(x_ref, w_ref, h_ref, g_ref, o_ref, on_ref, *, sub):
    for r0 in range(0, x_ref.shape[0], sub):
        rows = slice(r0, r0 + sub)
        hnew = h_ref[rows, :] + jnp.dot(x_ref[rows, :], w_ref[...], preferred_element_type=F32)
        o_ref[rows, :] = hnew
        on_ref[rows, :] = _rmsnorm_rows(hnew, g_ref[...]).astype(on_ref.dtype)


def _res_matmul(x, w_all, layer, h, g_next, tm_pref, sub_pref):
    T, K = x.shape
    D = h.shape[1]
    tm = _tile(T, tm_pref)
    row = lambda i: (i, 0)
    fix = lambda i: (0, 0)
    return pl.pallas_call(
        functools.partial(_res_mm_kernel, sub=_tile(tm, sub_pref)),
        grid=(T // tm,),
        in_specs=[pl.BlockSpec((tm, K), row), _const_spec((None, K, D), lambda i: (layer, 0, 0)),
                  pl.BlockSpec((tm, D), row), pl.BlockSpec((1, D), fix)],
        out_specs=[pl.BlockSpec((tm, D), row), pl.BlockSpec((tm, D), row)],
        out_shape=[jax.ShapeDtypeStruct((T, D), F32), jax.ShapeDtypeStruct((T, D), BF16)],
        compiler_params=_params(("parallel",), ((tm, K), BF16, 2), ((K, D), BF16, 1),
                                ((tm, D), F32, 6), ((tm, D), BF16, 2)),
        name="res_proj",
    )(x, w_all, h, g_next.reshape(1, D))


def _ffn_up_kernel(x_ref, wg_ref, wv_ref, cwg_ref, cwv_ref, cbg_ref, cbv_ref, o_ref, wb_ref, e_ref,
                   *, tiles_per_seq, sub):
    tm = x_ref.shape[0]
    tn = o_ref.shape[1]
    halo = V7X_SUBLANES
    lanes = V7X_LANES
    nslab = tn // lanes
    i = pl.program_id(1)
    first = (i % tiles_per_seq) == 0

    @pl.when(i == 0)
    def _():
        wb_ref[:, 0:tn] = wg_ref[...].astype(BF16)
        wb_ref[:, tn:2 * tn] = wv_ref[...].astype(BF16)

    @pl.when(first)
    def _():
        e_ref[:, 0:halo, :] = jnp.zeros((e_ref.shape[0], halo, lanes), F32)

    @pl.when(jnp.logical_not(first))
    def _():
        e_ref[:, 0:halo, :] = e_ref[:, tm:tm + halo, :]

    def conv(k, r0, cw_ref, cb_ref, kc):
        cols = slice(kc * lanes, (kc + 1) * lanes)
        c = cb_ref[:, cols]
        for j in range(CONV_WIDTH):
            off = halo + r0 - (CONV_WIDTH - 1) + j
            c = c + e_ref[k, off:off + sub, :] * cw_ref[j:j + 1, cols]
        return c

    for r0 in range(0, tm, sub):
        a = jnp.dot(x_ref[r0:r0 + sub, :], wb_ref[...], preferred_element_type=F32)
        for k in range(2 * nslab):
            e_ref[k, halo + r0:halo + r0 + sub, :] = a[:, k * lanes:(k + 1) * lanes]
        for k in range(nslab):
            gate = conv(k, r0, cwg_ref, cbg_ref, k)
            val = conv(k + nslab, r0, cwv_ref, cbv_ref, k)
            o_ref[r0:r0 + sub, k * lanes:(k + 1) * lanes] = (jax.nn.gelu(gate) * val).astype(o_ref.dtype)


def _ffn_up(hn, w_up_all, conv_w_all, conv_b_all, layer, S):
    T, D = hn.shape
    tn = 512
    tm = _tile(S, 2048)
    sub = _tile(tm, 1024)
    nj = D_FF // tn
    halo = V7X_SUBLANES
    conv_b3 = conv_b_all.reshape(conv_b_all.shape[0], 1, 2 * D_FF)
    gate_col = lambda j, i: (layer, 0, j)
    val_col = lambda j, i: (layer, 0, nj + j)
    return pl.pallas_call(
        functools.partial(_ffn_up_kernel, tiles_per_seq=S // tm, sub=sub),
        grid=(nj, T // tm),
        in_specs=[pl.BlockSpec((tm, D), lambda j, i: (i, 0)),
                  pl.BlockSpec((None, D, tn), gate_col),
                  pl.BlockSpec((None, D, tn), val_col),
                  pl.BlockSpec((None, CONV_WIDTH, tn), gate_col),
                  pl.BlockSpec((None, CONV_WIDTH, tn), val_col),
                  pl.BlockSpec((None, 1, tn), gate_col),
                  pl.BlockSpec((None, 1, tn), val_col)],
        out_specs=pl.BlockSpec((tm, tn), lambda j, i: (i, j)),
        out_shape=jax.ShapeDtypeStruct((T, D_FF), BF16),
        scratch_shapes=[pltpu.VMEM((D, 2 * tn), BF16),
                        pltpu.VMEM((2 * tn // V7X_LANES, tm + halo, V7X_LANES), F32)],
        compiler_params=_params(("arbitrary", "arbitrary"), ((tm, D), BF16, 2), ((D, tn), F32, 4),
                                ((D, 2 * tn), BF16, 1), ((tm + halo, 2 * tn), F32, 1),
                                ((sub, 2 * tn), F32, 2), ((tm, tn), BF16, 2)),
        name="ffn_up",
    )(hn, w_up_all, w_up_all, conv_w_all, conv_w_all, conv_b3, conv_b3)


def _ple_kernel(hn_ref, wg_ref, p_ref, wu_ref, h_ref, g_ref, *out_refs, sub):
    on_ref = out_refs[-1]
    for r0 in range(0, hn_ref.shape[0], sub):
        rows = slice(r0, r0 + sub)
        gate = jax.nn.sigmoid(jnp.dot(hn_ref[rows, :], wg_ref[...], preferred_element_type=F32))
        up = jnp.dot(p_ref[rows, :].astype(BF16), wu_ref[...], preferred_element_type=F32)
        hnew = h_ref[rows, :] + up * gate
        if len(out_refs) == 2:
            out_refs[0][rows, :] = hnew
        on_ref[rows, :] = _rmsnorm_rows(hnew, g_ref[...]).astype(on_ref.dtype)


def _ple(hn, wg_all, p_all, layer, wu_all, h, g_next, last, sub_pref):
    T, D = h.shape
    tm = _tile(T, 512)
    row = lambda i: (i, 0)
    fix = lambda i: (0, 0)
    out_specs = [pl.BlockSpec((tm, D), row), pl.BlockSpec((tm, D), row)]
    out_shape = [jax.ShapeDtypeStruct((T, D), F32), jax.ShapeDtypeStruct((T, D), BF16)]
    if last:
        out_specs, out_shape = out_specs[:1], out_shape[:1]
    return pl.pallas_call(
        functools.partial(_ple_kernel, sub=_tile(tm, sub_pref)),
        grid=(T // tm,),
        in_specs=[pl.BlockSpec((tm, D), row),
                  _const_spec((None, D, D), lambda i: (layer, 0, 0)),
                  pl.BlockSpec((None, tm, PLE_DIM), lambda i: (layer, i, 0)),
                  _const_spec((None, PLE_DIM, D), lambda i: (layer, 0, 0)),
                  pl.BlockSpec((tm, D), row),
                  pl.BlockSpec((1, D), fix)],
        out_specs=out_specs,
        out_shape=out_shape,
        compiler_params=_params(("parallel",), ((tm, D), BF16, 2), ((D, D), BF16, 1), ((PLE_DIM, D), BF16, 1),
                                ((tm, PLE_DIM), F32, 2), ((tm, D), F32, 8)),
        name="ple",
    )(hn, wg_all, p_all, wu_all, h, g_next.reshape(1, D))


def _rot_cols(w):
    half = MLA_ROPE // 2
    return jnp.concatenate([-w[..., half:], w[..., :half]], axis=-1)


def _even_in_weight(w):
    cq = w[..., :Q_LORA]
    ckv = w[..., Q_LORA:Q_LORA + KV_LORA]
    kpe = w[..., Q_LORA + KV_LORA:Q_LORA + KV_LORA + MLA_ROPE]
    uv = w[..., Q_LORA + KV_LORA + MLA_ROPE:]
    used = 2 * SGU_WIDTH + Q_LORA + KV_LORA + 2 * MLA_ROPE
    pad = jnp.zeros(w.shape[:-1] + (EVEN_IN_PAD - used,), w.dtype)
    return jnp.concatenate([uv, cq, ckv, kpe, _rot_cols(kpe), pad], axis=-1).astype(BF16)


def _q_up_weight(w):
    w = w.reshape(Q_LORA, MLA_HEADS, MLA_NOPE + MLA_ROPE)
    rope = w[..., MLA_NOPE:]
    w = jnp.concatenate([w[..., :MLA_NOPE], rope, _rot_cols(rope)], axis=-1)
    return w.reshape(Q_LORA, MLA_HEADS * MLA_QK_PAD).astype(BF16)


def _kv_up_weight(w):
    w = w.reshape(KV_LORA, MLA_HEADS, MLA_NOPE + MLA_VDIM)
    k = w[..., :MLA_NOPE].reshape(KV_LORA, -1)
    v = w[..., MLA_NOPE:].reshape(KV_LORA, -1)
    return jnp.concatenate([k, v], axis=1).astype(BF16)


def kernel(x, p, positions, mix_norm_g, even_w_in, mla_q_norm_g, mla_w_q_up, mla_kv_norm_g, mla_w_kv_up,
           sgu_ln_g, sgu_ln_b, sgu_w_s, sgu_b_s, even_w_out, ret_w_in, ret_gn_g, ret_gn_b, ret_w_out,
           ffn_norm_g, ffn_w_up, ffn_conv_w, ffn_conv_b, ffn_w_down, ple_norm_g, ple_w_gate, ple_w_up,
           final_norm_g):
    B, S, D = x.shape
    depth = p.shape[0]
    T = B * S
    assert D == D_MODEL and S % SGU_BLOCK == 0
    cos_t, sin_t, m_tab = _rope_tables(positions)
    h = x.reshape(T, D)
    p_all = p.reshape(depth, T, PLE_DIM)
    even_in_b = _even_in_weight(even_w_in)
    even_out_b = even_w_out.astype(BF16)
    ret_in_b = ret_w_in.astype(BF16)
    ret_out_b = ret_w_out.astype(BF16)
    ffn_down_b = ffn_w_down.astype(BF16)
    ple_gate_b = ple_w_gate.astype(BF16)
    ple_up_b = ple_w_up.astype(BF16)
    for i in range(depth):
        j = i // 2
        if i % 2 == 0:
            if i == 0:
                z = _norm_matmul(h, mix_norm_g[0], even_in_b, j, 1024)
            else:
                z = _matmul(hn, even_in_b, j, 1536)
            q, k, v = _mla_prep(z, m_tab, mla_q_norm_g[j], mla_kv_norm_g[j],
                                _q_up_weight(mla_w_q_up[j]), _kv_up_weight(mla_w_kv_up[j]), 1024)
            a = _mla_attn(q, k, v, B, S, rowsum_on_mxu=True)
            h, hn = _even_out(a, z, h, sgu_ln_g[j], sgu_ln_b[j], sgu_w_s[j], sgu_b_s[j],
                              even_out_b, j, ffn_norm_g[i], 512)
        else:
            z = _ret_in_proj(hn, ret_in_b, j, cos_t, sin_t, 256)
            r = _retention(z, ret_gn_g[j], ret_gn_b[j], B, S)
            h, hn = _res_matmul(r, ret_out_b, j, h, ffn_norm_g[i], 512, 512)
        act = _ffn_up(hn, ffn_w_up, ffn_conv_w, ffn_conv_b, i, S)
        h, hn = _res_matmul(act, ffn_down_b, i, h, ple_norm_g[i], 512, 512)
        if i < depth - 1:
            h, hn = _ple(hn, ple_gate_b, p_all, i, ple_up_b, h, mix_norm_g[i + 1], False, 512)
        else:
            out, = _ple(hn, ple_gate_b, p_all, i, ple_up_b, h, final_norm_g, True, 512)
    return out.reshape(B, S, D)
```

```python
import functools
import math

import jax
import jax.numpy as jnp
from jax import lax
from jax.experimental import pallas as pl
from jax.experimental.pallas import tpu as pltpu

F32 = jnp.float32
BF16 = jnp.bfloat16

D_MODEL = 2048
CHUNK = 64
EPS = 1e-6
ROPE_THETA = 10000.0
MLA_HEADS = 8
MLA_NOPE = 128
MLA_ROPE = 64
MLA_VDIM = 128
Q_LORA = 512
KV_LORA = 256
MLA_OUT = MLA_HEADS * MLA_VDIM
MLA_QK_PAD = 256
SGU_BLOCK = 128
SGU_GROUPS = 8
SGU_CH = 128
SGU_WIDTH = SGU_GROUPS * SGU_CH
EVEN_IN_PAD = 3072
RET_HEADS = 8
RET_DK = 256
RET_DV = 512
RET_QK_WIDTH = RET_HEADS * RET_DK
RET_V_WIDTH = RET_HEADS * RET_DV
RET_IN_WIDTH = 2 * RET_QK_WIDTH + 2 * RET_V_WIDTH
RET_SUPER = 256
D_FF = 5632
CONV_WIDTH = 3
PLE_DIM = 256

V7X_LANES = 128
V7X_SUBLANES = 8
V7X_VMEM_BUDGET = 58 * 1024 * 1024


def _nbytes(shape, dtype):
    return math.prod(shape) * jnp.dtype(dtype).itemsize


def _params(semantics, *buffers):
    need = sum(_nbytes(s, d) * n for s, d, n in buffers)
    return pltpu.CompilerParams(dimension_semantics=semantics,
                                vmem_limit_bytes=min(V7X_VMEM_BUDGET, need + (4 << 20)))


def _tile(n, pref, mult=V7X_SUBLANES):
    t = min(n, pref)
    while n % t or t % mult:
        t -= 1
    return t


def _const_spec(shape, index_map):
    return pl.BlockSpec(shape, index_map, pipeline_mode=pl.Buffered(1))


def _rmsnorm_rows(x, g):
    ms = jnp.mean(x * x, axis=-1, keepdims=True)
    return x * lax.rsqrt(ms + EPS) * g


def _rope_tables_kernel(pos_ref, fr_ref, fm_ref, cos_ref, sin_ref, m_ref):
    pos = pos_ref[...].astype(F32)
    ang = pos * fr_ref[...]
    cos_ref[...] = jnp.cos(ang)
    sin_ref[...] = jnp.sin(ang)
    angm = pos * fm_ref[...]
    lane = lax.broadcasted_iota(jnp.int32, angm.shape, 1)
    m_ref[...] = jnp.where(lane < MLA_ROPE, jnp.cos(angm), jnp.sin(angm))


def _rope_tables(positions):
    T = positions.size
    tm = _tile(T, 1024)
    half_r = RET_DK // 2
    half_m = MLA_ROPE // 2
    fr = (ROPE_THETA ** (-jnp.arange(half_r, dtype=F32) / half_r))[None, :]
    fm1 = ROPE_THETA ** (-jnp.arange(half_m, dtype=F32) / half_m)
    fm = jnp.tile(fm1, 4)[None, :]
    tab = jax.ShapeDtypeStruct((T, V7X_LANES), F32)
    row = pl.BlockSpec((tm, V7X_LANES), lambda i: (i, 0))
    frq = pl.BlockSpec((1, V7X_LANES), lambda i: (0, 0))
    return pl.pallas_call(
        _rope_tables_kernel,
        grid=(T // tm,),
        in_specs=[pl.BlockSpec((tm, 1), lambda i: (i, 0)), frq, frq],
        out_specs=[row, row, row],
        out_shape=[tab, tab, tab],
        compiler_params=_params(("parallel",), ((tm, V7X_LANES), F32, 16)),
        name="rope_tables",
    )(positions.reshape(T, 1), fr, fm)


def _mm_kernel(x_ref, w_ref, o_ref):
    o_ref[...] = jnp.dot(x_ref[...], w_ref[...], preferred_element_type=F32).astype(o_ref.dtype)


def _matmul(x, w_all, layer):
    T, K = x.shape
    N = w_all.shape[2]
    tm = _tile(T, 1024)
    tn = _tile(N, 1536, V7X_LANES)
    return pl.pallas_call(
        _mm_kernel,
        grid=(T // tm, N // tn),
        in_specs=[pl.BlockSpec((tm, K), lambda i, j: (i, 0)),
                  pl.BlockSpec((None, K, tn), lambda i, j: (layer, 0, j))],
        out_specs=pl.BlockSpec((tm, tn), lambda i, j: (i, j)),
        out_shape=jax.ShapeDtypeStruct((T, N), BF16),
        compiler_params=_params(("parallel", "arbitrary"), ((tm, K), BF16, 2), ((K, tn), BF16, 2),
                                ((tm, tn), BF16, 2), ((tm, tn), F32, 1)),
        name="proj",
    )(x, w_all)


def _norm_mm_kernel(x_ref, g_ref, w_ref, o_ref, xn_ref):
    @pl.when(pl.program_id(1) == 0)
    def _():
        xn_ref[...] = _rmsnorm_rows(x_ref[...], g_ref[...]).astype(xn_ref.dtype)

    o_ref[...] = jnp.dot(xn_ref[...], w_ref[...], preferred_element_type=F32).astype(o_ref.dtype)


def _norm_matmul(x, g, w_all, layer):
    T, K = x.shape
    N = w_all.shape[2]
    tm = _tile(T, 1024)
    tn = _tile(N, 1536, V7X_LANES)
    return pl.pallas_call(
        _norm_mm_kernel,
        grid=(T // tm, N // tn),
        in_specs=[pl.BlockSpec((tm, K), lambda i, j: (i, 0)),
                  pl.BlockSpec((1, K), lambda i, j: (0, 0)),
                  pl.BlockSpec((None, K, tn), lambda i, j: (layer, 0, j))],
        out_specs=pl.BlockSpec((tm, tn), lambda i, j: (i, j)),
        out_shape=jax.ShapeDtypeStruct((T, N), BF16),
        scratch_shapes=[pltpu.VMEM((tm, K), BF16)],
        compiler_params=_params(("parallel", "arbitrary"), ((tm, K), F32, 3), ((tm, K), BF16, 1),
                                ((K, tn), BF16, 2), ((tm, tn), BF16, 2), ((tm, tn), F32, 1)),
        name="norm_proj",
    )(x, g.reshape(1, K), w_all)


def _ret_in_kernel(x_ref, w_ref, cos_ref, sin_ref, o_ref, *, sub):
    j = pl.program_id(1)
    tm = x_ref.shape[0]
    half = RET_DK // 2
    v_tile = 2
    gate_tile = 4

    def blocks(finish):
        for r0 in range(0, tm, sub):
            rows = slice(r0, r0 + sub)
            finish(jnp.dot(x_ref[rows, :], w_ref[...], preferred_element_type=F32), rows)

    def rotary(scale):
        def finish(a, rows):
            cs = cos_ref[rows, :]
            sn = sin_ref[rows, :]
            if scale != 1.0:
                cs, sn = cs * scale, sn * scale
            for h in range(RET_HEADS):
                c0 = h * RET_DK
                x1, x2 = a[:, c0:c0 + half], a[:, c0 + half:c0 + RET_DK]
                o_ref[rows, c0:c0 + half] = (x1 * cs - x2 * sn).astype(o_ref.dtype)
                o_ref[rows, c0 + half:c0 + RET_DK] = (x2 * cs + x1 * sn).astype(o_ref.dtype)
        return finish

    def plain(a, rows):
        o_ref[rows, :] = a.astype(o_ref.dtype)

    def gate(a, rows):
        o_ref[rows, :] = jax.nn.silu(a).astype(o_ref.dtype)

    @pl.when(j == 0)
    def _():
        blocks(rotary(1.0))

    @pl.when(j == 1)
    def _():
        blocks(rotary(RET_DK ** -0.5))

    @pl.when(jnp.logical_and(j >= v_tile, j < gate_tile))
    def _():
        blocks(plain)

    @pl.when(j >= gate_tile)
    def _():
        blocks(gate)


def _ret_in_proj(x, w_all, layer, cos_t, sin_t):
    T, K = x.shape
    tn = RET_QK_WIDTH
    tm = _tile(T, 1024)
    half = RET_DK // 2
    return pl.pallas_call(
        functools.partial(_ret_in_kernel, sub=_tile(tm, 256)),
        grid=(T // tm, RET_IN_WIDTH // tn),
        in_specs=[pl.BlockSpec((tm, K), lambda i, j: (i, 0)),
                  pl.BlockSpec((None, K, tn), lambda i, j: (layer, 0, j)),
                  pl.BlockSpec((tm, half), lambda i, j: (i, 0)),
                  pl.BlockSpec((tm, half), lambda i, j: (i, 0))],
        out_specs=pl.BlockSpec((tm, tn), lambda i, j: (i, j)),
        out_shape=jax.ShapeDtypeStruct((T, RET_IN_WIDTH), BF16),
        compiler_params=_params(("parallel", "arbitrary"), ((tm, K), BF16, 2), ((K, tn), BF16, 2),
                                ((tm, tn), BF16, 2), ((tm, half), F32, 4), ((tm // 4, tn), F32, 4)),
        name="ret_in_proj",
    )(x, w_all, cos_t, sin_t)


def _mla_prep_kernel(cq_ref, ckv_ref, kpe_ref, m_ref, qg_ref, kvg_ref, wq_ref, wkv_ref,
                     q_ref, k_ref, v_ref):
    scale = (MLA_NOPE + MLA_ROPE) ** -0.5
    m = m_ref[...]
    cqn = _rmsnorm_rows(cq_ref[...].astype(F32), qg_ref[...]).astype(BF16)
    q = jnp.dot(cqn, wq_ref[...], preferred_element_type=F32)
    for h in range(MLA_HEADS):
        c0 = h * MLA_QK_PAD
        q_ref[:, c0:c0 + MLA_NOPE] = (q[:, c0:c0 + MLA_NOPE] * scale).astype(BF16)
        y = q[:, c0 + MLA_NOPE:c0 + MLA_QK_PAD] * m
        y = y + pltpu.roll(y, MLA_ROPE, axis=1)
        q_ref[:, c0 + MLA_NOPE:c0 + MLA_QK_PAD] = (y * scale).astype(BF16)
    ckvn = _rmsnorm_rows(ckv_ref[...].astype(F32), kvg_ref[...]).astype(BF16)
    kv = jnp.dot(ckvn, wkv_ref[...], preferred_element_type=F32)
    yk = kpe_ref[...].astype(F32) * m
    yk = yk + pltpu.roll(yk, MLA_ROPE, axis=1)
    lane = lax.broadcasted_iota(jnp.int32, yk.shape, 1)
    k_rot = jnp.where(lane < MLA_ROPE, yk, 0.0).astype(BF16)
    for h in range(MLA_HEADS):
        c0 = h * MLA_QK_PAD
        k_ref[:, c0:c0 + MLA_NOPE] = kv[:, h * MLA_NOPE:(h + 1) * MLA_NOPE].astype(BF16)
        k_ref[:, c0 + MLA_NOPE:c0 + MLA_QK_PAD] = k_rot
    v_ref[...] = kv[:, MLA_HEADS * MLA_NOPE:].astype(BF16)


def _mla_prep(z, m_tab, q_norm_g, kv_norm_g, wq, wkv):
    T = z.shape[0]
    tm = _tile(T, 1024)
    qkw = MLA_HEADS * MLA_QK_PAD
    u_v = 2 * SGU_WIDTH
    return pl.pallas_call(
        _mla_prep_kernel,
        grid=(T // tm,),
        in_specs=[
            pl.BlockSpec((tm, Q_LORA), lambda i: (i, u_v // Q_LORA)),
            pl.BlockSpec((tm, KV_LORA), lambda i: (i, (u_v + Q_LORA) // KV_LORA)),
            pl.BlockSpec((tm, 2 * MLA_ROPE), lambda i: (i, (u_v + Q_LORA + KV_LORA) // (2 * MLA_ROPE))),
            pl.BlockSpec((tm, V7X_LANES), lambda i: (i, 0)),
            pl.BlockSpec((1, Q_LORA), lambda i: (0, 0)),
            pl.BlockSpec((1, KV_LORA), lambda i: (0, 0)),
            _const_spec((Q_LORA, qkw), lambda i: (0, 0)),
            _const_spec((KV_LORA, qkw), lambda i: (0, 0)),
        ],
        out_specs=[pl.BlockSpec((tm, qkw), lambda i: (i, 0)),
                   pl.BlockSpec((tm, qkw), lambda i: (i, 0)),
                   pl.BlockSpec((tm, MLA_OUT), lambda i: (i, 0))],
        out_shape=[jax.ShapeDtypeStruct((T, qkw), BF16), jax.ShapeDtypeStruct((T, qkw), BF16),
                   jax.ShapeDtypeStruct((T, MLA_OUT), BF16)],
        compiler_params=_params(("parallel",), ((tm, qkw), BF16, 8), ((tm, qkw), F32, 3),
                                ((Q_LORA + KV_LORA, qkw), BF16, 1)),
        name="mla_prep",
    )(z, z, z, m_tab, q_norm_g.reshape(1, -1), kv_norm_g.reshape(1, -1), wq, wkv)


def _mla_attn_kernel(q_ref, k_ref, v_ref, o_ref, va_ref, *, tq):
    S = q_ref.shape[0]
    r = lax.broadcasted_iota(jnp.int32, (tq, tq), 0) // CHUNK
    c = lax.broadcasted_iota(jnp.int32, (tq, tq), 1) // CHUNK
    bias = jnp.where(c <= r, 0.0, -1e30).astype(F32)
    va_ref[:, 0:MLA_VDIM] = v_ref[...]
    va_ref[:, MLA_VDIM:] = jnp.ones((S, MLA_VDIM), BF16)
    for qi in range(S // tq):
        past = qi * tq
        q = q_ref[past:past + tq, :]
        s = lax.dot_general(q, k_ref[0:past + tq, :], (((1,), (1,)), ((), ())),
                            preferred_element_type=F32)
        s_diag = s[:, past:] + bias
        s = s_diag if qi == 0 else jnp.concatenate([s[:, :past], s_diag], axis=1)
        mx = jnp.max(s, axis=-1, keepdims=True)
        p = jnp.exp(s - mx)
        oa = jnp.dot(p.astype(BF16), va_ref[0:past + tq, :], preferred_element_type=F32)
        o_ref[past:past + tq, :] = (oa[:, :MLA_VDIM] / oa[:, MLA_VDIM:]).astype(o_ref.dtype)


def _mla_attn(q, k, v, B, S):
    tq = _tile(S, 256, CHUNK)
    q3 = q.reshape(B, S, -1)
    k3 = k.reshape(B, S, -1)
    v3 = v.reshape(B, S, -1)
    out = pl.pallas_call(
        functools.partial(_mla_attn_kernel, tq=tq),
        grid=(B, MLA_HEADS),
        scratch_shapes=[pltpu.VMEM((S, 2 * MLA_VDIM), BF16)],
        in_specs=[pl.BlockSpec((None, S, MLA_QK_PAD), lambda b, h: (b, 0, h)),
                  pl.BlockSpec((None, S, MLA_QK_PAD), lambda b, h: (b, 0, h)),
                  pl.BlockSpec((None, S, MLA_VDIM), lambda b, h: (b, 0, h))],
        out_specs=pl.BlockSpec((None, S, MLA_VDIM), lambda b, h: (b, 0, h)),
        out_shape=jax.ShapeDtypeStruct((B, S, MLA_OUT), BF16),
        compiler_params=_params(("parallel", "parallel"), ((S, MLA_QK_PAD), BF16, 4),
                                ((S, MLA_VDIM), BF16, 4), ((tq, S), F32, 4)),
        name="mla_attn",
    )(q3, k3, v3)
    return out.reshape(B * S, MLA_OUT)


def _even_out_kernel(a_ref, u_ref, v_ref, h_ref, lng_ref, lnb_ref, ws_ref, bs_ref, wo_ref, g_ref,
                     o_ref, on_ref, wm_ref, bo_ref, acc_ref):
    tm = a_ref.shape[0]
    pr = lax.broadcasted_iota(jnp.int32, (SGU_BLOCK, SGU_BLOCK), 0) // CHUNK
    pc = lax.broadcasted_iota(jnp.int32, (SGU_BLOCK, SGU_BLOCK), 1) // CHUNK
    for g in range(SGU_GROUPS):
        wm_ref[g] = jnp.where(pr >= pc, ws_ref[g], 0.0).astype(BF16)
    gu = jax.nn.gelu(u_ref[...].astype(F32))
    gv = jax.nn.gelu(v_ref[...].astype(F32))
    mu = jnp.mean(gv, axis=-1, keepdims=True)
    xc = gv - mu
    y = xc * lax.rsqrt(jnp.mean(xc * xc, axis=-1, keepdims=True) + EPS)
    vn = (y * lng_ref[...] + lnb_ref[...]).astype(BF16)
    acc_ref[...] = jnp.dot(a_ref[...], wo_ref[0:MLA_OUT, :], preferred_element_type=F32)
    for g in range(SGU_GROUPS):
        bcol = bs_ref[:, g:g + 1]
        cols = slice(g * SGU_CH, (g + 1) * SGU_CH)
        for nb in range(tm // SGU_BLOCK):
            rr = slice(nb * SGU_BLOCK, (nb + 1) * SGU_BLOCK)
            s = jnp.dot(wm_ref[g], vn[rr, cols], preferred_element_type=F32) + bcol
            bo_ref[rr, cols] = (gu[rr, cols] * s).astype(BF16)
    acc = acc_ref[...] + jnp.dot(bo_ref[...], wo_ref[MLA_OUT:, :], preferred_element_type=F32)
    hnew = h_ref[...] + acc
    o_ref[...] = hnew
    on_ref[...] = _rmsnorm_rows(hnew, g_ref[...]).astype(on_ref.dtype)


def _even_out(a, z, h, ln_g, ln_b, w_s, b_s, wo_all, layer, g_next):
    T, D = h.shape
    tm = _tile(T, 512, SGU_BLOCK)
    row = lambda i: (i, 0)
    fix = lambda i: (0, 0)
    return pl.pallas_call(
        _even_out_kernel,
        grid=(T // tm,),
        in_specs=[pl.BlockSpec((tm, MLA_OUT), row),
                  pl.BlockSpec((tm, SGU_WIDTH), lambda i: (i, 0)),
                  pl.BlockSpec((tm, SGU_WIDTH), lambda i: (i, 1)),
                  pl.BlockSpec((tm, D), row),
                  pl.BlockSpec((1, SGU_WIDTH), fix),
                  pl.BlockSpec((1, SGU_WIDTH), fix),
                  _const_spec((SGU_GROUPS, SGU_BLOCK, SGU_BLOCK), lambda i: (0, 0, 0)),
                  pl.BlockSpec((SGU_BLOCK, SGU_GROUPS), fix),
                  _const_spec((None, MLA_OUT + SGU_WIDTH, D), lambda i: (layer, 0, 0)),
                  pl.BlockSpec((1, D), fix)],
        out_specs=[pl.BlockSpec((tm, D), row), pl.BlockSpec((tm, D), row)],
        out_shape=[jax.ShapeDtypeStruct((T, D), F32), jax.ShapeDtypeStruct((T, D), BF16)],
        scratch_shapes=[pltpu.VMEM((SGU_GROUPS, SGU_BLOCK, SGU_BLOCK), BF16), pltpu.VMEM((tm, SGU_WIDTH), BF16),
                        pltpu.VMEM((tm, D), F32)],
        compiler_params=_params(("parallel",), ((tm, D), F32, 7), ((tm, D), BF16, 6),
                                ((MLA_OUT + SGU_WIDTH, D), BF16, 1), ((tm, SGU_WIDTH), F32, 4)),
        name="even_out",
    )(a, z, z, h, ln_g.reshape(1, -1), ln_b.reshape(1, -1), w_s, b_s.T, wo_all, g_next.reshape(1, D))


def _retention_kernel(q_ref, k_ref, v_ref, gate_ref, dm_ref, qd_ref, kd_ref, sd_ref,
                      gng_ref, gnb_ref, o_ref, state_ref, *, L):
    S = q_ref.shape[0]
    state_ref[...] = jnp.zeros_like(state_ref)
    dm = dm_ref[...]
    qd = qd_ref[...]
    kd = kd_ref[...]
    sd = sd_ref[...]
    gng = gng_ref[...]
    gnb = gnb_ref[...]

    def step(c):
        rows = slice(c * L, (c + 1) * L)
        qb = q_ref[rows, :]
        kb = k_ref[rows, :]
        vb = v_ref[rows, :]
        sc = lax.dot_general(qb, kb, (((1,), (1,)), ((), ())), preferred_element_type=F32) * dm
        intra = jnp.dot(sc.astype(BF16), vb, preferred_element_type=F32)
        st = state_ref[...]
        cross = jnp.dot(qb, st.astype(BF16), preferred_element_type=F32) * qd
        kz = (kb.astype(F32) * kd).astype(BF16)
        state_ref[...] = st * sd + lax.dot_general(kz, vb, (((0,), (0,)), ((), ())),
                                                   preferred_element_type=F32)
        out = intra + cross
        mu = jnp.mean(out, axis=-1, keepdims=True)
        xc = out - mu
        y = xc * lax.rsqrt(jnp.mean(xc * xc, axis=-1, keepdims=True) + EPS)
        y = y * gng + gnb
        o_ref[rows, :] = (y * gate_ref[rows, :].astype(F32)).astype(o_ref.dtype)

    for c in range(S // L):
        step(c)


def _retention_tables(L):
    log_g = jnp.log1p(-(2.0 ** (-5.0 - jnp.arange(RET_HEADS, dtype=F32))))
    idx = jnp.arange(L, dtype=F32)
    chunk = jnp.arange(L) // CHUNK
    decay = jnp.exp(log_g[:, None, None] * jnp.abs(idx[:, None] - idx[None, :]))
    dm = jnp.where(chunk[None, :, None] >= chunk[None, None, :], decay, 0.0)
    qd = jnp.exp(log_g[:, None] * (idx + 1.0)[None, :])[:, :, None]
    kd = jnp.exp(log_g[:, None] * (L - 1 - idx)[None, :])[:, :, None]
    sd = jnp.broadcast_to(jnp.exp(log_g * L)[:, None, None], (RET_HEADS, 1, RET_DV))
    return dm, qd, kd, sd


def _retention(z, gn_g, gn_b, B, S):
    L = _tile(S, RET_SUPER, CHUNK)
    z3 = z.reshape(B, S, RET_IN_WIDTH)
    dm, qd, kd, sd = _retention_tables(L)
    qk_blocks = RET_QK_WIDTH // RET_DK
    v_first = 2 * RET_QK_WIDTH // RET_DV
    g_first = v_first + RET_V_WIDTH // RET_DV
    out = pl.pallas_call(
        functools.partial(_retention_kernel, L=L),
        grid=(B, RET_HEADS),
        in_specs=[pl.BlockSpec((None, S, RET_DK), lambda b, h: (b, 0, h)),
                  pl.BlockSpec((None, S, RET_DK), lambda b, h: (b, 0, qk_blocks + h)),
                  pl.BlockSpec((None, S, RET_DV), lambda b, h: (b, 0, v_first + h)),
                  pl.BlockSpec((None, S, RET_DV), lambda b, h: (b, 0, g_first + h)),
                  pl.BlockSpec((None, L, L), lambda b, h: (h, 0, 0)),
                  pl.BlockSpec((None, L, 1), lambda b, h: (h, 0, 0)),
                  pl.BlockSpec((None, L, 1), lambda b, h: (h, 0, 0)),
                  pl.BlockSpec((None, 1, RET_DV), lambda b, h: (h, 0, 0)),
                  pl.BlockSpec((1, RET_DV), lambda b, h: (0, h)),
                  pl.BlockSpec((1, RET_DV), lambda b, h: (0, h))],
        out_specs=pl.BlockSpec((None, S, RET_DV), lambda b, h: (b, 0, h)),
        out_shape=jax.ShapeDtypeStruct((B, S, RET_V_WIDTH), BF16),
        scratch_shapes=[pltpu.VMEM((RET_DK, RET_DV), F32)],
        compiler_params=_params(("parallel", "arbitrary"), ((S, RET_DK), BF16, 4), ((S, RET_DV), BF16, 6),
                                ((L, L), F32, 4), ((L, V7X_LANES), F32, 4),
                                ((RET_DK, RET_DV), F32, 4), ((L, RET_DV), F32, 8)),
        name="retention",
    )(z3, z3, z3, z3, dm, qd, kd, sd, gn_g.reshape(1, -1), gn_b.reshape(1, -1))
    return out.reshape(B * S, RET_V_WIDTH)


def _res_mm_kernel(x_ref, w_ref, h_ref, g_ref, o_ref, on_ref):
    hnew = h_ref[...] + jnp.dot(x_ref[...], w_ref[...], preferred_element_type=F32)
    o_ref[...] = hnew
    on_ref[...] = _rmsnorm_rows(hnew, g_ref[...]).astype(on_ref.dtype)


def _res_matmul(x, w_all, layer, h, g_next):
    T, K = x.shape
    D = h.shape[1]
    tm = _tile(T, 512)
    row = lambda i: (i, 0)
    fix = lambda i: (0, 0)
    return pl.pallas_call(
        _res_mm_kernel,
        grid=(T // tm,),
        in_specs=[pl.BlockSpec((tm, K), row), _const_spec((None, K, D), lambda i: (layer, 0, 0)),
                  pl.BlockSpec((tm, D), row), pl.BlockSpec((1, D), fix)],
        out_specs=[pl.BlockSpec((tm, D), row), pl.BlockSpec((tm, D), row)],
        out_shape=[jax.ShapeDtypeStruct((T, D), F32), jax.ShapeDtypeStruct((T, D), BF16)],
        compiler_params=_params(("parallel",), ((tm, K), BF16, 2), ((K, D), BF16, 1),
                                ((tm, D), F32, 6), ((tm, D), BF16, 2)),
        name="res_proj",
    )(x, w_all, h, g_next.reshape(1, D))


def _ffn_up_kernel(x_ref, wg_ref, wv_ref, cwg_ref, cwv_ref, cbg_ref, cbv_ref, o_ref, wb_ref, e_ref,
                   *, tiles_per_seq, sub):
    tm = x_ref.shape[0]
    tn = o_ref.shape[1]
    halo = V7X_SUBLANES
    lanes = V7X_LANES
    nslab = tn // lanes
    i = pl.program_id(1)
    first = (i % tiles_per_seq) == 0

    @pl.when(i == 0)
    def _():
        wb_ref[:, 0:tn] = wg_ref[...].astype(BF16)
        wb_ref[:, tn:2 * tn] = wv_ref[...].astype(BF16)

    @pl.when(first)
    def _():
        e_ref[:, 0:halo, :] = jnp.zeros((e_ref.shape[0], halo, lanes), F32)

    @pl.when(jnp.logical_not(first))
    def _():
        e_ref[:, 0:halo, :] = e_ref[:, tm:tm + halo, :]

    def conv(k, r0, cw_ref, cb_ref, kc):
        cols = slice(kc * lanes, (kc + 1) * lanes)
        c = cb_ref[:, cols]
        for j in range(CONV_WIDTH):
            off = halo + r0 - (CONV_WIDTH - 1) + j
            c = c + e_ref[k, off:off + sub, :] * cw_ref[j:j + 1, cols]
        return c

    for r0 in range(0, tm, sub):
        a = jnp.dot(x_ref[r0:r0 + sub, :], wb_ref[...], preferred_element_type=F32)
        for k in range(2 * nslab):
            e_ref[k, halo + r0:halo + r0 + sub, :] = a[:, k * lanes:(k + 1) * lanes]
        for k in range(nslab):
            gate = conv(k, r0, cwg_ref, cbg_ref, k)
            val = conv(k + nslab, r0, cwv_ref, cbv_ref, k)
            o_ref[r0:r0 + sub, k * lanes:(k + 1) * lanes] = (jax.nn.gelu(gate) * val).astype(o_ref.dtype)


def _ffn_up(hn, w_up_all, conv_w_all, conv_b_all, layer, S):
    T, D = hn.shape
    tn = 512
    tm = _tile(S, 2048)
    sub = _tile(tm, 1024)
    nj = D_FF // tn
    halo = V7X_SUBLANES
    conv_b3 = conv_b_all.reshape(conv_b_all.shape[0], 1, 2 * D_FF)
    gate_col = lambda j, i: (layer, 0, j)
    val_col = lambda j, i: (layer, 0, nj + j)
    return pl.pallas_call(
        functools.partial(_ffn_up_kernel, tiles_per_seq=S // tm, sub=sub),
        grid=(nj, T // tm),
        in_specs=[pl.BlockSpec((tm, D), lambda j, i: (i, 0)),
                  pl.BlockSpec((None, D, tn), gate_col),
                  pl.BlockSpec((None, D, tn), val_col),
                  pl.BlockSpec((None, CONV_WIDTH, tn), gate_col),
                  pl.BlockSpec((None, CONV_WIDTH, tn), val_col),
                  pl.BlockSpec((None, 1, tn), gate_col),
                  pl.BlockSpec((None, 1, tn), val_col)],
        out_specs=pl.BlockSpec((tm, tn), lambda j, i: (i, j)),
        out_shape=jax.ShapeDtypeStruct((T, D_FF), BF16),
        scratch_shapes=[pltpu.VMEM((D, 2 * tn), BF16),
                        pltpu.VMEM((2 * tn // V7X_LANES, tm + halo, V7X_LANES), F32)],
        compiler_params=_params(("arbitrary", "arbitrary"), ((tm, D), BF16, 2), ((D, tn), F32, 4),
                                ((D, 2 * tn), BF16, 1), ((tm + halo, 2 * tn), F32, 1),
                                ((sub, 2 * tn), F32, 2), ((tm, tn), BF16, 2)),
        name="ffn_up",
    )(hn, w_up_all, w_up_all, conv_w_all, conv_w_all, conv_b3, conv_b3)


def _ple_kernel(hn_ref, wg_ref, p_ref, wu_ref, h_ref, g_ref, *out_refs):
    on_ref = out_refs[-1]
    gate = jax.nn.sigmoid(jnp.dot(hn_ref[...], wg_ref[...], preferred_element_type=F32))
    up = jnp.dot(p_ref[...].astype(BF16), wu_ref[...], preferred_element_type=F32)
    hnew = h_ref[...] + up * gate
    if len(out_refs) == 2:
        out_refs[0][...] = hnew
    on_ref[...] = _rmsnorm_rows(hnew, g_ref[...]).astype(on_ref.dtype)


def _ple(hn, wg_all, p_all, layer, wu_all, h, g_next, last):
    T, D = h.shape
    tm = _tile(T, 512)
    row = lambda i: (i, 0)
    fix = lambda i: (0, 0)
    out_specs = [pl.BlockSpec((tm, D), row), pl.BlockSpec((tm, D), row)]
    out_shape = [jax.ShapeDtypeStruct((T, D), F32), jax.ShapeDtypeStruct((T, D), BF16)]
    if last:
        out_specs, out_shape = out_specs[:1], out_shape[:1]
    return pl.pallas_call(
        _ple_kernel,
        grid=(T // tm,),
        in_specs=[pl.BlockSpec((tm, D), row),
                  _const_spec((None, D, D), lambda i: (layer, 0, 0)),
                  pl.BlockSpec((None, tm, PLE_DIM), lambda i: (layer, i, 0)),
                  _const_spec((None, PLE_DIM, D), lambda i: (layer, 0, 0)),
                  pl.BlockSpec((tm, D), row),
                  pl.BlockSpec((1, D), fix)],
        out_specs=out_specs,
        out_shape=out_shape,
        compiler_params=_params(("parallel",), ((tm, D), BF16, 2), ((D, D), BF16, 1), ((PLE_DIM, D), BF16, 1),
                                ((tm, PLE_DIM), F32, 2), ((tm, D), F32, 8)),
        name="ple",
    )(hn, wg_all, p_all, wu_all, h, g_next.reshape(1, D))


def _rot_cols(w):
    half = MLA_ROPE // 2
    return jnp.concatenate([-w[..., half:], w[..., :half]], axis=-1)


def _even_in_weight(w):
    w = w.astype(BF16)
    cq = w[..., :Q_LORA]
    ckv = w[..., Q_LORA:Q_LORA + KV_LORA]
    kpe = w[..., Q_LORA + KV_LORA:Q_LORA + KV_LORA + MLA_ROPE]
    uv = w[..., Q_LORA + KV_LORA + MLA_ROPE:]
    used = 2 * SGU_WIDTH + Q_LORA + KV_LORA + 2 * MLA_ROPE
    pad = jnp.zeros(w.shape[:-1] + (EVEN_IN_PAD - used,), w.dtype)
    return jnp.concatenate([uv, cq, ckv, kpe, _rot_cols(kpe), pad], axis=-1).astype(BF16)


def _q_up_weight(w):
    w = w.reshape(Q_LORA, MLA_HEADS, MLA_NOPE + MLA_ROPE)
    rope = w[..., MLA_NOPE:]
    w = jnp.concatenate([w[..., :MLA_NOPE], rope, _rot_cols(rope)], axis=-1)
    return w.reshape(Q_LORA, MLA_HEADS * MLA_QK_PAD).astype(BF16)


def _kv_up_weight(w):
    w = w.reshape(KV_LORA, MLA_HEADS, MLA_NOPE + MLA_VDIM)
    k = w[..., :MLA_NOPE].reshape(KV_LORA, -1)
    v = w[..., MLA_NOPE:].reshape(KV_LORA, -1)
    return jnp.concatenate([k, v], axis=1).astype(BF16)


def kernel(x, p, positions, mix_norm_g, even_w_in, mla_q_norm_g, mla_w_q_up, mla_kv_norm_g, mla_w_kv_up,
           sgu_ln_g, sgu_ln_b, sgu_w_s, sgu_b_s, even_w_out, ret_w_in, ret_gn_g, ret_gn_b, ret_w_out,
           ffn_norm_g, ffn_w_up, ffn_conv_w, ffn_conv_b, ffn_w_down, ple_norm_g, ple_w_gate, ple_w_up,
           final_norm_g):
    B, S, D = x.shape
    depth = p.shape[0]
    T = B * S
    assert D == D_MODEL and S % SGU_BLOCK == 0
    cos_t, sin_t, m_tab = _rope_tables(positions)
    h = x.reshape(T, D)
    p_all = p.reshape(depth, T, PLE_DIM)
    even_in_b = _even_in_weight(even_w_in)
    even_out_b = even_w_out.astype(BF16)
    ret_in_b = ret_w_in.astype(BF16)
    ret_out_b = ret_w_out.astype(BF16)
    ffn_down_b = ffn_w_down.astype(BF16)
    ple_gate_b = ple_w_gate.astype(BF16)
    ple_up_b = ple_w_up.astype(BF16)
    for i in range(depth):
        j = i // 2
        if i % 2 == 0:
            if i == 0:
                z = _norm_matmul(h, mix_norm_g[0], even_in_b, j)
            else:
                z = _matmul(hn, even_in_b, j)
            q, k, v = _mla_prep(z, m_tab, mla_q_norm_g[j], mla_kv_norm_g[j],
                                _q_up_weight(mla_w_q_up[j]), _kv_up_weight(mla_w_kv_up[j]))
            a = _mla_attn(q, k, v, B, S)
            h, hn = _even_out(a, z, h, sgu_ln_g[j], sgu_ln_b[j], sgu_w_s[j], sgu_b_s[j],
                              even_out_b, j, ffn_norm_g[i])
        else:
            z = _ret_in_proj(hn, ret_in_b, j, cos_t, sin_t)
            r = _retention(z, ret_gn_g[j], ret_gn_b[j], B, S)
            h, hn = _res_matmul(r, ret_out_b, j, h, ffn_norm_g[i])
        act = _ffn_up(hn, ffn_w_up, ffn_conv_w, ffn_conv_b, i, S)
        h, hn = _res_matmul(act, ffn_down_b, i, h, ple_norm_g[i])
        if i < depth - 1:
            h, hn = _ple(hn, ple_gate_b, p_all, i, ple_up_b, h, mix_norm_g[i + 1], False)
        else:
            out, = _ple(hn, ple_gate_b, p_all, i, ple_up_b, h, final_norm_g, True)
    return out.reshape(B, S, D)
```

```python
import functools
import math

import jax
import jax.numpy as jnp
from jax import lax
from jax.experimental import pallas as pl
from jax.experimental.pallas import tpu as pltpu

F32 = jnp.float32
BF16 = jnp.bfloat16

D_MODEL = 2048
CHUNK = 64
EPS = 1e-6
ROPE_THETA = 10000.0
MLA_HEADS = 8
MLA_NOPE = 128
MLA_ROPE = 64
MLA_VDIM = 128
Q_LORA = 512
KV_LORA = 256
MLA_OUT = MLA_HEADS * MLA_VDIM
MLA_QK_PAD = 256
SGU_BLOCK = 128
SGU_GROUPS = 8
SGU_CH = 128
SGU_WIDTH = SGU_GROUPS * SGU_CH
EVEN_IN_PAD = 3072
RET_HEADS = 8
RET_DK = 256
RET_DV = 512
RET_QK_WIDTH = RET_HEADS * RET_DK
RET_V_WIDTH = RET_HEADS * RET_DV
RET_IN_WIDTH = 2 * RET_QK_WIDTH + 2 * RET_V_WIDTH
RET_SUPER = 256
D_FF = 5632
CONV_WIDTH = 3
PLE_DIM = 256

V7X_LANES = 128
V7X_SUBLANES = 8
V7X_VMEM_BUDGET = 58 * 1024 * 1024


def _nbytes(shape, dtype):
    return math.prod(shape) * jnp.dtype(dtype).itemsize


def _params(semantics, *buffers):
    need = sum(_nbytes(s, d) * n for s, d, n in buffers)
    return pltpu.CompilerParams(dimension_semantics=semantics,
                                vmem_limit_bytes=min(V7X_VMEM_BUDGET, need + (4 << 20)))


def _tile(n, pref, mult=V7X_SUBLANES):
    t = min(n, pref)
    while n % t or t % mult:
        t -= 1
    return t


def _const_spec(shape, index_map):
    return pl.BlockSpec(shape, index_map, pipeline_mode=pl.Buffered(1))


def _rmsnorm_rows(x, g):
    ms = jnp.mean(x * x, axis=-1, keepdims=True)
    return x * lax.rsqrt(ms + EPS) * g


def _rope_tables_kernel(pos_ref, fr_ref, fm_ref, cos_ref, sin_ref, m_ref):
    pos = pos_ref[...].astype(F32)
    ang = pos * fr_ref[...]
    cos_ref[...] = jnp.cos(ang)
    sin_ref[...] = jnp.sin(ang)
    angm = pos * fm_ref[...]
    lane = lax.broadcasted_iota(jnp.int32, angm.shape, 1)
    m_ref[...] = jnp.where(lane < MLA_ROPE, jnp.cos(angm), jnp.sin(angm))


def _rope_tables(positions):
    T = positions.size
    tm = _tile(T, 1024)
    half_r = RET_DK // 2
    half_m = MLA_ROPE // 2
    fr = (ROPE_THETA ** (-jnp.arange(half_r, dtype=F32) / half_r))[None, :]
    fm1 = ROPE_THETA ** (-jnp.arange(half_m, dtype=F32) / half_m)
    fm = jnp.tile(fm1, 4)[None, :]
    tab = jax.ShapeDtypeStruct((T, V7X_LANES), F32)
    row = pl.BlockSpec((tm, V7X_LANES), lambda i: (i, 0))
    frq = pl.BlockSpec((1, V7X_LANES), lambda i: (0, 0))
    return pl.pallas_call(
        _rope_tables_kernel,
        grid=(T // tm,),
        in_specs=[pl.BlockSpec((tm, 1), lambda i: (i, 0)), frq, frq],
        out_specs=[row, row, row],
        out_shape=[tab, tab, tab],
        compiler_params=_params(("parallel",), ((tm, V7X_LANES), F32, 16)),
        name="rope_tables",
    )(positions.reshape(T, 1), fr, fm)


def _mm_kernel(x_ref, w_ref, o_ref):
    o_ref[...] = jnp.dot(x_ref[...], w_ref[...], preferred_element_type=F32).astype(o_ref.dtype)


def _matmul(x, w_all, layer):
    T, K = x.shape
    N = w_all.shape[2]
    tm = _tile(T, 1024)
    tn = _tile(N, 1536, V7X_LANES)
    return pl.pallas_call(
        _mm_kernel,
        grid=(T // tm, N // tn),
        in_specs=[pl.BlockSpec((tm, K), lambda i, j: (i, 0)),
                  pl.BlockSpec((None, K, tn), lambda i, j: (layer, 0, j))],
        out_specs=pl.BlockSpec((tm, tn), lambda i, j: (i, j)),
        out_shape=jax.ShapeDtypeStruct((T, N), BF16),
        compiler_params=_params(("parallel", "arbitrary"), ((tm, K), BF16, 2), ((K, tn), BF16, 2),
                                ((tm, tn), BF16, 2), ((tm, tn), F32, 1)),
        name="proj",
    )(x, w_all)


def _norm_mm_kernel(x_ref, g_ref, w_ref, o_ref, xn_ref):
    @pl.when(pl.program_id(1) == 0)
    def _():
        xn_ref[...] = _rmsnorm_rows(x_ref[...], g_ref[...]).astype(xn_ref.dtype)

    o_ref[...] = jnp.dot(xn_ref[...], w_ref[...], preferred_element_type=F32).astype(o_ref.dtype)


def _norm_matmul(x, g, w_all, layer):
    T, K = x.shape
    N = w_all.shape[2]
    tm = _tile(T, 1024)
    tn = _tile(N, 1536, V7X_LANES)
    return pl.pallas_call(
        _norm_mm_kernel,
        grid=(T // tm, N // tn),
        in_specs=[pl.BlockSpec((tm, K), lambda i, j: (i, 0)),
                  pl.BlockSpec((1, K), lambda i, j: (0, 0)),
                  pl.BlockSpec((None, K, tn), lambda i, j: (layer, 0, j))],
        out_specs=pl.BlockSpec((tm, tn), lambda i, j: (i, j)),
        out_shape=jax.ShapeDtypeStruct((T, N), BF16),
        scratch_shapes=[pltpu.VMEM((tm, K), BF16)],
        compiler_params=_params(("parallel", "arbitrary"), ((tm, K), F32, 3), ((tm, K), BF16, 1),
                                ((K, tn), BF16, 2), ((tm, tn), BF16, 2), ((tm, tn), F32, 1)),
        name="norm_proj",
    )(x, g.reshape(1, K), w_all)


def _ret_in_kernel(x_ref, w_ref, cos_ref, sin_ref, o_ref, *, sub, plain_sub):
    j = pl.program_id(1)
    tm = x_ref.shape[0]
    half = RET_DK // 2
    v_tile = 2
    gate_tile = 4

    def blocks(finish, sub=sub):
        for r0 in range(0, tm, sub):
            rows = slice(r0, r0 + sub)
            finish(jnp.dot(x_ref[rows, :], w_ref[...], preferred_element_type=F32), rows)

    def rotary(scale):
        def finish(a, rows):
            cs = cos_ref[rows, :]
            sn = sin_ref[rows, :]
            if scale != 1.0:
                cs, sn = cs * scale, sn * scale
            for h in range(RET_HEADS):
                c0 = h * RET_DK
                x1, x2 = a[:, c0:c0 + half], a[:, c0 + half:c0 + RET_DK]
                o_ref[rows, c0:c0 + half] = (x1 * cs - x2 * sn).astype(o_ref.dtype)
                o_ref[rows, c0 + half:c0 + RET_DK] = (x2 * cs + x1 * sn).astype(o_ref.dtype)
        return finish

    def plain(a, rows):
        o_ref[rows, :] = a.astype(o_ref.dtype)

    def gate(a, rows):
        o_ref[rows, :] = jax.nn.silu(a).astype(o_ref.dtype)

    @pl.when(j == 0)
    def _():
        blocks(rotary(1.0))

    @pl.when(j == 1)
    def _():
        blocks(rotary(RET_DK ** -0.5))

    @pl.when(jnp.logical_and(j >= v_tile, j < gate_tile))
    def _():
        blocks(plain, plain_sub)

    @pl.when(j >= gate_tile)
    def _():
        blocks(gate)


def _ret_in_proj(x, w_all, layer, cos_t, sin_t, plain_sub_pref):
    T, K = x.shape
    tn = RET_QK_WIDTH
    tm = _tile(T, 1024)
    half = RET_DK // 2
    return pl.pallas_call(
        functools.partial(_ret_in_kernel, sub=_tile(tm, 256), plain_sub=_tile(tm, plain_sub_pref)),
        grid=(T // tm, RET_IN_WIDTH // tn),
        in_specs=[pl.BlockSpec((tm, K), lambda i, j: (i, 0)),
                  pl.BlockSpec((None, K, tn), lambda i, j: (layer, 0, j)),
                  pl.BlockSpec((tm, half), lambda i, j: (i, 0)),
                  pl.BlockSpec((tm, half), lambda i, j: (i, 0))],
        out_specs=pl.BlockSpec((tm, tn), lambda i, j: (i, j)),
        out_shape=jax.ShapeDtypeStruct((T, RET_IN_WIDTH), BF16),
        compiler_params=_params(("parallel", "arbitrary"), ((tm, K), BF16, 2), ((K, tn), BF16, 2),
                                ((tm, tn), BF16, 2), ((tm, half), F32, 4), ((tm // 4, tn), F32, 4)),
        name="ret_in_proj",
    )(x, w_all, cos_t, sin_t)


def _mla_prep_kernel(cq_ref, ckv_ref, kpe_ref, m_ref, qg_ref, kvg_ref, wq_ref, wkv_ref,
                     q_ref, k_ref, v_ref):
    scale = (MLA_NOPE + MLA_ROPE) ** -0.5
    m = m_ref[...]
    cqn = _rmsnorm_rows(cq_ref[...].astype(F32), qg_ref[...]).astype(BF16)
    q = jnp.dot(cqn, wq_ref[...], preferred_element_type=F32)
    for h in range(MLA_HEADS):
        c0 = h * MLA_QK_PAD
        q_ref[:, c0:c0 + MLA_NOPE] = (q[:, c0:c0 + MLA_NOPE] * scale).astype(BF16)
        y = q[:, c0 + MLA_NOPE:c0 + MLA_QK_PAD] * m
        y = y + pltpu.roll(y, MLA_ROPE, axis=1)
        q_ref[:, c0 + MLA_NOPE:c0 + MLA_QK_PAD] = (y * scale).astype(BF16)
    ckvn = _rmsnorm_rows(ckv_ref[...].astype(F32), kvg_ref[...]).astype(BF16)
    kv = jnp.dot(ckvn, wkv_ref[...], preferred_element_type=F32)
    yk = kpe_ref[...].astype(F32) * m
    yk = yk + pltpu.roll(yk, MLA_ROPE, axis=1)
    lane = lax.broadcasted_iota(jnp.int32, yk.shape, 1)
    k_rot = jnp.where(lane < MLA_ROPE, yk, 0.0).astype(BF16)
    for h in range(MLA_HEADS):
        c0 = h * MLA_QK_PAD
        k_ref[:, c0:c0 + MLA_NOPE] = kv[:, h * MLA_NOPE:(h + 1) * MLA_NOPE].astype(BF16)
        k_ref[:, c0 + MLA_NOPE:c0 + MLA_QK_PAD] = k_rot
    v_ref[...] = kv[:, MLA_HEADS * MLA_NOPE:].astype(BF16)


def _mla_prep(z, m_tab, q_norm_g, kv_norm_g, wq, wkv):
    T = z.shape[0]
    tm = _tile(T, 1024)
    qkw = MLA_HEADS * MLA_QK_PAD
    u_v = 2 * SGU_WIDTH
    return pl.pallas_call(
        _mla_prep_kernel,
        grid=(T // tm,),
        in_specs=[
            pl.BlockSpec((tm, Q_LORA), lambda i: (i, u_v // Q_LORA)),
            pl.BlockSpec((tm, KV_LORA), lambda i: (i, (u_v + Q_LORA) // KV_LORA)),
            pl.BlockSpec((tm, 2 * MLA_ROPE), lambda i: (i, (u_v + Q_LORA + KV_LORA) // (2 * MLA_ROPE))),
            pl.BlockSpec((tm, V7X_LANES), lambda i: (i, 0)),
            pl.BlockSpec((1, Q_LORA), lambda i: (0, 0)),
            pl.BlockSpec((1, KV_LORA), lambda i: (0, 0)),
            _const_spec((Q_LORA, qkw), lambda i: (0, 0)),
            _const_spec((KV_LORA, qkw), lambda i: (0, 0)),
        ],
        out_specs=[pl.BlockSpec((tm, qkw), lambda i: (i, 0)),
                   pl.BlockSpec((tm, qkw), lambda i: (i, 0)),
                   pl.BlockSpec((tm, MLA_OUT), lambda i: (i, 0))],
        out_shape=[jax.ShapeDtypeStruct((T, qkw), BF16), jax.ShapeDtypeStruct((T, qkw), BF16),
                   jax.ShapeDtypeStruct((T, MLA_OUT), BF16)],
        compiler_params=_params(("parallel",), ((tm, qkw), BF16, 8), ((tm, qkw), F32, 3),
                                ((Q_LORA + KV_LORA, qkw), BF16, 1)),
        name="mla_prep",
    )(z, z, z, m_tab, q_norm_g.reshape(1, -1), kv_norm_g.reshape(1, -1), wq, wkv)


def _mla_attn_kernel(q_ref, k_ref, v_ref, o_ref, va_ref, *, tq):
    S = q_ref.shape[0]
    r = lax.broadcasted_iota(jnp.int32, (tq, tq), 0) // CHUNK
    c = lax.broadcasted_iota(jnp.int32, (tq, tq), 1) // CHUNK
    bias = jnp.where(c <= r, 0.0, -1e30).astype(F32)
    va_ref[:, 0:MLA_VDIM] = v_ref[...]
    va_ref[:, MLA_VDIM:] = jnp.ones((S, MLA_VDIM), BF16)
    for qi in range(S // tq):
        past = qi * tq
        q = q_ref[past:past + tq, :]
        s = lax.dot_general(q, k_ref[0:past + tq, :], (((1,), (1,)), ((), ())),
                            preferred_element_type=F32)
        s_diag = s[:, past:] + bias
        s = s_diag if qi == 0 else jnp.concatenate([s[:, :past], s_diag], axis=1)
        mx = jnp.max(s, axis=-1, keepdims=True)
        p = jnp.exp(s - mx)
        oa = jnp.dot(p.astype(BF16), va_ref[0:past + tq, :], preferred_element_type=F32)
        o_ref[past:past + tq, :] = (oa[:, :MLA_VDIM] / oa[:, MLA_VDIM:]).astype(o_ref.dtype)


def _mla_attn(q, k, v, B, S, tq_pref):
    tq = _tile(S, tq_pref, CHUNK)
    q3 = q.reshape(B, S, -1)
    k3 = k.reshape(B, S, -1)
    v3 = v.reshape(B, S, -1)
    out = pl.pallas_call(
        functools.partial(_mla_attn_kernel, tq=tq),
        grid=(B, MLA_HEADS),
        scratch_shapes=[pltpu.VMEM((S, 2 * MLA_VDIM), BF16)],
        in_specs=[pl.BlockSpec((None, S, MLA_QK_PAD), lambda b, h: (b, 0, h)),
                  pl.BlockSpec((None, S, MLA_QK_PAD), lambda b, h: (b, 0, h)),
                  pl.BlockSpec((None, S, MLA_VDIM), lambda b, h: (b, 0, h))],
        out_specs=pl.BlockSpec((None, S, MLA_VDIM), lambda b, h: (b, 0, h)),
        out_shape=jax.ShapeDtypeStruct((B, S, MLA_OUT), BF16),
        compiler_params=_params(("parallel", "parallel"), ((S, MLA_QK_PAD), BF16, 4),
                                ((S, MLA_VDIM), BF16, 4), ((tq, S), F32, 4)),
        name="mla_attn",
    )(q3, k3, v3)
    return out.reshape(B * S, MLA_OUT)


def _even_out_kernel(a_ref, u_ref, v_ref, h_ref, lng_ref, lnb_ref, ws_ref, bs_ref, wo_ref, g_ref,
                     o_ref, on_ref, wm_ref, bo_ref, acc_ref):
    tm = a_ref.shape[0]
    pr = lax.broadcasted_iota(jnp.int32, (SGU_BLOCK, SGU_BLOCK), 0) // CHUNK
    pc = lax.broadcasted_iota(jnp.int32, (SGU_BLOCK, SGU_BLOCK), 1) // CHUNK
    for g in range(SGU_GROUPS):
        wm_ref[g] = jnp.where(pr >= pc, ws_ref[g], 0.0).astype(BF16)
    gu = jax.nn.gelu(u_ref[...].astype(F32))
    gv = jax.nn.gelu(v_ref[...].astype(F32))
    mu = jnp.mean(gv, axis=-1, keepdims=True)
    xc = gv - mu
    y = xc * lax.rsqrt(jnp.mean(xc * xc, axis=-1, keepdims=True) + EPS)
    vn = (y * lng_ref[...] + lnb_ref[...]).astype(BF16)
    acc_ref[...] = jnp.dot(a_ref[...], wo_ref[0:MLA_OUT, :], preferred_element_type=F32)
    for g in range(SGU_GROUPS):
        bcol = bs_ref[:, g:g + 1]
        cols = slice(g * SGU_CH, (g + 1) * SGU_CH)
        for nb in range(tm // SGU_BLOCK):
            rr = slice(nb * SGU_BLOCK, (nb + 1) * SGU_BLOCK)
            s = jnp.dot(wm_ref[g], vn[rr, cols], preferred_element_type=F32) + bcol
            bo_ref[rr, cols] = (gu[rr, cols] * s).astype(BF16)
    acc = acc_ref[...] + jnp.dot(bo_ref[...], wo_ref[MLA_OUT:, :], preferred_element_type=F32)
    hnew = h_ref[...] + acc
    o_ref[...] = hnew
    on_ref[...] = _rmsnorm_rows(hnew, g_ref[...]).astype(on_ref.dtype)


def _even_out(a, z, h, ln_g, ln_b, w_s, b_s, wo_all, layer, g_next):
    T, D = h.shape
    tm = _tile(T, 512, SGU_BLOCK)
    row = lambda i: (i, 0)
    fix = lambda i: (0, 0)
    return pl.pallas_call(
        _even_out_kernel,
        grid=(T // tm,),
        in_specs=[pl.BlockSpec((tm, MLA_OUT), row),
                  pl.BlockSpec((tm, SGU_WIDTH), lambda i: (i, 0)),
                  pl.BlockSpec((tm, SGU_WIDTH), lambda i: (i, 1)),
                  pl.BlockSpec((tm, D), row),
                  pl.BlockSpec((1, SGU_WIDTH), fix),
                  pl.BlockSpec((1, SGU_WIDTH), fix),
                  _const_spec((SGU_GROUPS, SGU_BLOCK, SGU_BLOCK), lambda i: (0, 0, 0)),
                  pl.BlockSpec((SGU_BLOCK, SGU_GROUPS), fix),
                  _const_spec((None, MLA_OUT + SGU_WIDTH, D), lambda i: (layer, 0, 0)),
                  pl.BlockSpec((1, D), fix)],
        out_specs=[pl.BlockSpec((tm, D), row), pl.BlockSpec((tm, D), row)],
        out_shape=[jax.ShapeDtypeStruct((T, D), F32), jax.ShapeDtypeStruct((T, D), BF16)],
        scratch_shapes=[pltpu.VMEM((SGU_GROUPS, SGU_BLOCK, SGU_BLOCK), BF16), pltpu.VMEM((tm, SGU_WIDTH), BF16),
                        pltpu.VMEM((tm, D), F32)],
        compiler_params=_params(("parallel",), ((tm, D), F32, 7), ((tm, D), BF16, 6),
                                ((MLA_OUT + SGU_WIDTH, D), BF16, 1), ((tm, SGU_WIDTH), F32, 4)),
        name="even_out",
    )(a, z, z, h, ln_g.reshape(1, -1), ln_b.reshape(1, -1), w_s, b_s.T, wo_all, g_next.reshape(1, D))


def _retention_kernel(q_ref, k_ref, v_ref, gate_ref, dm_ref, qd_ref, kd_ref, sd_ref,
                      gng_ref, gnb_ref, o_ref, state_ref, *, L):
    S = q_ref.shape[0]
    state_ref[...] = jnp.zeros_like(state_ref)
    dm = dm_ref[...]
    qd = qd_ref[...]
    kd = kd_ref[...]
    sd = sd_ref[...]
    gng = gng_ref[...]
    gnb = gnb_ref[...]

    def step(c):
        rows = slice(c * L, (c + 1) * L)
        qb = q_ref[rows, :]
        kb = k_ref[rows, :]
        vb = v_ref[rows, :]
        sc = lax.dot_general(qb, kb, (((1,), (1,)), ((), ())), preferred_element_type=F32) * dm
        intra = jnp.dot(sc.astype(BF16), vb, preferred_element_type=F32)
        st = state_ref[...]
        cross = jnp.dot(qb, st.astype(BF16), preferred_element_type=F32) * qd
        kz = (kb.astype(F32) * kd).astype(BF16)
        state_ref[...] = st * sd + lax.dot_general(kz, vb, (((0,), (0,)), ((), ())),
                                                   preferred_element_type=F32)
        out = intra + cross
        mu = jnp.mean(out, axis=-1, keepdims=True)
        xc = out - mu
        y = xc * lax.rsqrt(jnp.mean(xc * xc, axis=-1, keepdims=True) + EPS)
        y = y * gng + gnb
        o_ref[rows, :] = (y * gate_ref[rows, :].astype(F32)).astype(o_ref.dtype)

    for c in range(S // L):
        step(c)


def _retention_tables(L):
    log_g = jnp.log1p(-(2.0 ** (-5.0 - jnp.arange(RET_HEADS, dtype=F32))))
    idx = jnp.arange(L, dtype=F32)
    chunk = jnp.arange(L) // CHUNK
    decay = jnp.exp(log_g[:, None, None] * jnp.abs(idx[:, None] - idx[None, :]))
    dm = jnp.where(chunk[None, :, None] >= chunk[None, None, :], decay, 0.0)
    qd = jnp.exp(log_g[:, None] * (idx + 1.0)[None, :])[:, :, None]
    kd = jnp.exp(log_g[:, None] * (L - 1 - idx)[None, :])[:, :, None]
    sd = jnp.broadcast_to(jnp.exp(log_g * L)[:, None, None], (RET_HEADS, 1, RET_DV))
    return dm, qd, kd, sd


def _retention(z, gn_g, gn_b, B, S, L_pref):
    L = _tile(S, L_pref, CHUNK)
    z3 = z.reshape(B, S, RET_IN_WIDTH)
    dm, qd, kd, sd = _retention_tables(L)
    qk_blocks = RET_QK_WIDTH // RET_DK
    v_first = 2 * RET_QK_WIDTH // RET_DV
    g_first = v_first + RET_V_WIDTH // RET_DV
    out = pl.pallas_call(
        functools.partial(_retention_kernel, L=L),
        grid=(B, RET_HEADS),
        in_specs=[pl.BlockSpec((None, S, RET_DK), lambda b, h: (b, 0, h)),
                  pl.BlockSpec((None, S, RET_DK), lambda b, h: (b, 0, qk_blocks + h)),
                  pl.BlockSpec((None, S, RET_DV), lambda b, h: (b, 0, v_first + h)),
                  pl.BlockSpec((None, S, RET_DV), lambda b, h: (b, 0, g_first + h)),
                  pl.BlockSpec((None, L, L), lambda b, h: (h, 0, 0)),
                  pl.BlockSpec((None, L, 1), lambda b, h: (h, 0, 0)),
                  pl.BlockSpec((None, L, 1), lambda b, h: (h, 0, 0)),
                  pl.BlockSpec((None, 1, RET_DV), lambda b, h: (h, 0, 0)),
                  pl.BlockSpec((1, RET_DV), lambda b, h: (0, h)),
                  pl.BlockSpec((1, RET_DV), lambda b, h: (0, h))],
        out_specs=pl.BlockSpec((None, S, RET_DV), lambda b, h: (b, 0, h)),
        out_shape=jax.ShapeDtypeStruct((B, S, RET_V_WIDTH), BF16),
        scratch_shapes=[pltpu.VMEM((RET_DK, RET_DV), F32)],
        compiler_params=_params(("parallel", "arbitrary"), ((S, RET_DK), BF16, 4), ((S, RET_DV), BF16, 6),
                                ((L, L), F32, 4), ((L, V7X_LANES), F32, 4),
                                ((RET_DK, RET_DV), F32, 4), ((L, RET_DV), F32, 8)),
        name="retention",
    )(z3, z3, z3, z3, dm, qd, kd, sd, gn_g.reshape(1, -1), gn_b.reshape(1, -1))
    return out.reshape(B * S, RET_V_WIDTH)


def _res_mm_kernel(x_ref, w_ref, h_ref, g_ref, o_ref, on_ref):
    hnew = h_ref[...] + jnp.dot(x_ref[...], w_ref[...], preferred_element_type=F32)
    o_ref[...] = hnew
    on_ref[...] = _rmsnorm_rows(hnew, g_ref[...]).astype(on_ref.dtype)


def _res_matmul(x, w_all, layer, h, g_next):
    T, K = x.shape
    D = h.shape[1]
    tm = _tile(T, 512)
    row = lambda i: (i, 0)
    fix = lambda i: (0, 0)
    return pl.pallas_call(
        _res_mm_kernel,
        grid=(T // tm,),
        in_specs=[pl.BlockSpec((tm, K), row), _const_spec((None, K, D), lambda i: (layer, 0, 0)),
                  pl.BlockSpec((tm, D), row), pl.BlockSpec((1, D), fix)],
        out_specs=[pl.BlockSpec((tm, D), row), pl.BlockSpec((tm, D), row)],
        out_shape=[jax.ShapeDtypeStruct((T, D), F32), jax.ShapeDtypeStruct((T, D), BF16)],
        compiler_params=_params(("parallel",), ((tm, K), BF16, 2), ((K, D), BF16, 1),
                                ((tm, D), F32, 6), ((tm, D), BF16, 2)),
        name="res_proj",
    )(x, w_all, h, g_next.reshape(1, D))


def _ffn_up_kernel(x_ref, wg_ref, wv_ref, cwg_ref, cwv_ref, cbg_ref, cbv_ref, o_ref, wb_ref, e_ref,
                   *, tiles_per_seq, sub):
    tm = x_ref.shape[0]
    tn = o_ref.shape[1]
    halo = V7X_SUBLANES
    lanes = V7X_LANES
    nslab = tn // lanes
    i = pl.program_id(1)
    first = (i % tiles_per_seq) == 0

    @pl.when(i == 0)
    def _():
        wb_ref[:, 0:tn] = wg_ref[...].astype(BF16)
        wb_ref[:, tn:2 * tn] = wv_ref[...].astype(BF16)

    @pl.when(first)
    def _():
        e_ref[:, 0:halo, :] = jnp.zeros((e_ref.shape[0], halo, lanes), F32)

    @pl.when(jnp.logical_not(first))
    def _():
        e_ref[:, 0:halo, :] = e_ref[:, tm:tm + halo, :]

    def conv(k, r0, cw_ref, cb_ref, kc):
        cols = slice(kc * lanes, (kc + 1) * lanes)
        c = cb_ref[:, cols]
        for j in range(CONV_WIDTH):
            off = halo + r0 - (CONV_WIDTH - 1) + j
            c = c + e_ref[k, off:off + sub, :] * cw_ref[j:j + 1, cols]
        return c

    for r0 in range(0, tm, sub):
        a = jnp.dot(x_ref[r0:r0 + sub, :], wb_ref[...], preferred_element_type=F32)
        for k in range(2 * nslab):
            e_ref[k, halo + r0:halo + r0 + sub, :] = a[:, k * lanes:(k + 1) * lanes]
        for k in range(nslab):
            gate = conv(k, r0, cwg_ref, cbg_ref, k)
            val = conv(k + nslab, r0, cwv_ref, cbv_ref, k)
            o_ref[r0:r0 + sub, k * lanes:(k + 1) * lanes] = (jax.nn.gelu(gate) * val).astype(o_ref.dtype)


def _ffn_up(hn, w_up_all, conv_w_all, conv_b_all, layer, S):
    T, D = hn.shape
    tn = 512
    tm = _tile(S, 2048)
    sub = _tile(tm, 1024)
    nj = D_FF // tn
    halo = V7X_SUBLANES
    conv_b3 = conv_b_all.reshape(conv_b_all.shape[0], 1, 2 * D_FF)
    gate_col = lambda j, i: (layer, 0, j)
    val_col = lambda j, i: (layer, 0, nj + j)
    return pl.pallas_call(
        functools.partial(_ffn_up_kernel, tiles_per_seq=S // tm, sub=sub),
        grid=(nj, T // tm),
        in_specs=[pl.BlockSpec((tm, D), lambda j, i: (i, 0)),
                  pl.BlockSpec((None, D, tn), gate_col),
                  pl.BlockSpec((None, D, tn), val_col),
                  pl.BlockSpec((None, CONV_WIDTH, tn), gate_col),
                  pl.BlockSpec((None, CONV_WIDTH, tn), val_col),
                  pl.BlockSpec((None, 1, tn), gate_col),
                  pl.BlockSpec((None, 1, tn), val_col)],
        out_specs=pl.BlockSpec((tm, tn), lambda j, i: (i, j)),
        out_shape=jax.ShapeDtypeStruct((T, D_FF), BF16),
        scratch_shapes=[pltpu.VMEM((D, 2 * tn), BF16),
                        pltpu.VMEM((2 * tn // V7X_LANES, tm + halo, V7X_LANES), F32)],
        compiler_params=_params(("arbitrary", "arbitrary"), ((tm, D), BF16, 2), ((D, tn), F32, 4),
                                ((D, 2 * tn), BF16, 1), ((tm + halo, 2 * tn), F32, 1),
                                ((sub, 2 * tn), F32, 2), ((tm, tn), BF16, 2)),
        name="ffn_up",
    )(hn, w_up_all, w_up_all, conv_w_all, conv_w_all, conv_b3, conv_b3)


def _ple_kernel(hn_ref, wg_ref, p_ref, wu_ref, h_ref, g_ref, *out_refs):
    on_ref = out_refs[-1]
    gate = jax.nn.sigmoid(jnp.dot(hn_ref[...], wg_ref[...], preferred_element_type=F32))
    up = jnp.dot(p_ref[...].astype(BF16), wu_ref[...], preferred_element_type=F32)
    hnew = h_ref[...] + up * gate
    if len(out_refs) == 2:
        out_refs[0][...] = hnew
    on_ref[...] = _rmsnorm_rows(hnew, g_ref[...]).astype(on_ref.dtype)


def _ple(hn, wg_all, p_all, layer, wu_all, h, g_next, last):
    T, D = h.shape
    tm = _tile(T, 512)
    row = lambda i: (i, 0)
    fix = lambda i: (0, 0)
    out_specs = [pl.BlockSpec((tm, D), row), pl.BlockSpec((tm, D), row)]
    out_shape = [jax.ShapeDtypeStruct((T, D), F32), jax.ShapeDtypeStruct((T, D), BF16)]
    if last:
        out_specs, out_shape = out_specs[:1], out_shape[:1]
    return pl.pallas_call(
        _ple_kernel,
        grid=(T // tm,),
        in_specs=[pl.BlockSpec((tm, D), row),
                  _const_spec((None, D, D), lambda i: (layer, 0, 0)),
                  pl.BlockSpec((None, tm, PLE_DIM), lambda i: (layer, i, 0)),
                  _const_spec((None, PLE_DIM, D), lambda i: (layer, 0, 0)),
                  pl.BlockSpec((tm, D), row),
                  pl.BlockSpec((1, D), fix)],
        out_specs=out_specs,
        out_shape=out_shape,
        compiler_params=_params(("parallel",), ((tm, D), BF16, 2), ((D, D), BF16, 1), ((PLE_DIM, D), BF16, 1),
                                ((tm, PLE_DIM), F32, 2), ((tm, D), F32, 8)),
        name="ple",
    )(hn, wg_all, p_all, wu_all, h, g_next.reshape(1, D))


def _rot_cols(w):
    half = MLA_ROPE // 2
    return jnp.concatenate([-w[..., half:], w[..., :half]], axis=-1)


def _even_in_weight(w):
    w = w.astype(BF16)
    cq = w[..., :Q_LORA]
    ckv = w[..., Q_LORA:Q_LORA + KV_LORA]
    kpe = w[..., Q_LORA + KV_LORA:Q_LORA + KV_LORA + MLA_ROPE]
    uv = w[..., Q_LORA + KV_LORA + MLA_ROPE:]
    used = 2 * SGU_WIDTH + Q_LORA + KV_LORA + 2 * MLA_ROPE
    pad = jnp.zeros(w.shape[:-1] + (EVEN_IN_PAD - used,), w.dtype)
    return jnp.concatenate([uv, cq, ckv, kpe, _rot_cols(kpe), pad], axis=-1).astype(BF16)


def _q_up_weight(w):
    w = w.reshape(Q_LORA, MLA_HEADS, MLA_NOPE + MLA_ROPE)
    rope = w[..., MLA_NOPE:]
    w = jnp.concatenate([w[..., :MLA_NOPE], rope, _rot_cols(rope)], axis=-1)
    return w.reshape(Q_LORA, MLA_HEADS * MLA_QK_PAD).astype(BF16)


def _kv_up_weight(w):
    w = w.reshape(KV_LORA, MLA_HEADS, MLA_NOPE + MLA_VDIM)
    k = w[..., :MLA_NOPE].reshape(KV_LORA, -1)
    v = w[..., MLA_NOPE:].reshape(KV_LORA, -1)
    return jnp.concatenate([k, v], axis=1).astype(BF16)


def kernel(x, p, positions, mix_norm_g, even_w_in, mla_q_norm_g, mla_w_q_up, mla_kv_norm_g, mla_w_kv_up,
           sgu_ln_g, sgu_ln_b, sgu_w_s, sgu_b_s, even_w_out, ret_w_in, ret_gn_g, ret_gn_b, ret_w_out,
           ffn_norm_g, ffn_w_up, ffn_conv_w, ffn_conv_b, ffn_w_down, ple_norm_g, ple_w_gate, ple_w_up,
           final_norm_g):
    B, S, D = x.shape
    depth = p.shape[0]
    T = B * S
    assert D == D_MODEL and S % SGU_BLOCK == 0
    cos_t, sin_t, m_tab = _rope_tables(positions)
    h = x.reshape(T, D)
    p_all = p.reshape(depth, T, PLE_DIM)
    even_in_b = _even_in_weight(even_w_in)
    even_out_b = even_w_out.astype(BF16)
    ret_in_b = ret_w_in.astype(BF16)
    ret_out_b = ret_w_out.astype(BF16)
    ffn_down_b = ffn_w_down.astype(BF16)
    ple_gate_b = ple_w_gate.astype(BF16)
    ple_up_b = ple_w_up.astype(BF16)
    for i in range(depth):
        j = i // 2
        if i % 2 == 0:
            if i == 0:
                z = _norm_matmul(h, mix_norm_g[0], even_in_b, j)
            else:
                z = _matmul(hn, even_in_b, j)
            q, k, v = _mla_prep(z, m_tab, mla_q_norm_g[j], mla_kv_norm_g[j],
                                _q_up_weight(mla_w_q_up[j]), _kv_up_weight(mla_w_kv_up[j]))
            a = _mla_attn(q, k, v, B, S, (256, 512)[j])
            h, hn = _even_out(a, z, h, sgu_ln_g[j], sgu_ln_b[j], sgu_w_s[j], sgu_b_s[j],
                              even_out_b, j, ffn_norm_g[i])
        else:
            z = _ret_in_proj(hn, ret_in_b, j, cos_t, sin_t, (256, 1024)[j])
            r = _retention(z, ret_gn_g[j], ret_gn_b[j], B, S, RET_SUPER)
            h, hn = _res_matmul(r, ret_out_b, j, h, ffn_norm_g[i])
        act = _ffn_up(hn, ffn_w_up, ffn_conv_w, ffn_conv_b, i, S)
        h, hn = _res_matmul(act, ffn_down_b, i, h, ple_norm_g[i])
        if i < depth - 1:
            h, hn = _ple(hn, ple_gate_b, p_all, i, ple_up_b, h, mix_norm_g[i + 1], False)
        else:
            out, = _ple(hn, ple_gate_b, p_all, i, ple_up_b, h, final_norm_g, True)
    return out.reshape(B, S, D)
```

```python
import functools
import math

import jax
import jax.numpy as jnp
from jax import lax
from jax.experimental import pallas as pl
from jax.experimental.pallas import tpu as pltpu

F32 = jnp.float32
BF16 = jnp.bfloat16

D_MODEL = 2048
CHUNK = 64
EPS = 1e-6
ROPE_THETA = 10000.0
MLA_HEADS = 8
MLA_NOPE = 128
MLA_ROPE = 64
MLA_VDIM = 128
Q_LORA = 512
KV_LORA = 256
MLA_OUT = MLA_HEADS * MLA_VDIM
MLA_QK_PAD = 256
SGU_BLOCK = 128
SGU_GROUPS = 8
SGU_CH = 128
SGU_WIDTH = SGU_GROUPS * SGU_CH
EVEN_IN_PAD = 3072
RET_HEADS = 8
RET_DK = 256
RET_DV = 512
RET_QK_WIDTH = RET_HEADS * RET_DK
RET_V_WIDTH = RET_HEADS * RET_DV
RET_IN_WIDTH = 2 * RET_QK_WIDTH + 2 * RET_V_WIDTH
RET_SUPER = 256
D_FF = 5632
CONV_WIDTH = 3
PLE_DIM = 256

V7X_LANES = 128
V7X_SUBLANES = 8
V7X_VMEM_BUDGET = 58 * 1024 * 1024
V7X_VMEM_UNLISTED = 4 * 1024 * 1024

PROJ_ROWS = 1024
PROJ_COLS = 1536
RES_ROWS = 512
RET_IN_BLOCK = 256
FFN_UP_ROWS = 2048
FFN_UP_BLOCK = 1024
FFN_UP_COLS = 512
ATTN_Q_ROWS = 512
TABLE_ROWS = 1024


def _nbytes(shape, dtype):
    return math.prod(shape) * jnp.dtype(dtype).itemsize


def _params(semantics, *buffers):
    need = sum(_nbytes(s, d) * n for s, d, n in buffers) + V7X_VMEM_UNLISTED
    return pltpu.CompilerParams(dimension_semantics=semantics,
                                vmem_limit_bytes=min(V7X_VMEM_BUDGET, need))


def _tile(n, pref, mult=V7X_SUBLANES):
    t = min(n, pref)
    while n % t or t % mult:
        t -= 1
    return t


def _const_spec(shape, index_map):
    return pl.BlockSpec(shape, index_map, pipeline_mode=pl.Buffered(1))


def _rmsnorm_rows(x, g):
    ms = jnp.mean(x * x, axis=-1, keepdims=True)
    return x * lax.rsqrt(ms + EPS) * g


def _rope_tables_kernel(pos_ref, fr_ref, fm_ref, cos_ref, sin_ref, m_ref):
    pos = pos_ref[...].astype(F32)
    ang = pos * fr_ref[...]
    cos_ref[...] = jnp.cos(ang)
    sin_ref[...] = jnp.sin(ang)
    angm = pos * fm_ref[...]
    lane = lax.broadcasted_iota(jnp.int32, angm.shape, 1)
    m_ref[...] = jnp.where(lane < MLA_ROPE, jnp.cos(angm), jnp.sin(angm))


def _rope_tables(positions):
    T = positions.size
    tm = _tile(T, TABLE_ROWS)
    half_r = RET_DK // 2
    half_m = MLA_ROPE // 2
    fr = (ROPE_THETA ** (-jnp.arange(half_r, dtype=F32) / half_r))[None, :]
    fm1 = ROPE_THETA ** (-jnp.arange(half_m, dtype=F32) / half_m)
    fm = jnp.tile(fm1, 4)[None, :]
    tab = jax.ShapeDtypeStruct((T, V7X_LANES), F32)
    row = pl.BlockSpec((tm, V7X_LANES), lambda i: (i, 0))
    frq = pl.BlockSpec((1, V7X_LANES), lambda i: (0, 0))
    return pl.pallas_call(
        _rope_tables_kernel,
        grid=(T // tm,),
        in_specs=[pl.BlockSpec((tm, 1), lambda i: (i, 0)), frq, frq],
        out_specs=[row, row, row],
        out_shape=[tab, tab, tab],
        compiler_params=_params(("parallel",), ((tm, V7X_LANES), F32, 16)),
        name="rope_tables",
    )(positions.reshape(T, 1), fr, fm)


def _mm_kernel(x_ref, w_ref, o_ref):
    o_ref[...] = jnp.dot(x_ref[...], w_ref[...], preferred_element_type=F32).astype(o_ref.dtype)


def _matmul(x, w_all, layer):
    T, K = x.shape
    N = w_all.shape[2]
    tm = _tile(T, PROJ_ROWS)
    tn = _tile(N, PROJ_COLS, V7X_LANES)
    return pl.pallas_call(
        _mm_kernel,
        grid=(T // tm, N // tn),
        in_specs=[pl.BlockSpec((tm, K), lambda i, j: (i, 0)),
                  pl.BlockSpec((None, K, tn), lambda i, j: (layer, 0, j))],
        out_specs=pl.BlockSpec((tm, tn), lambda i, j: (i, j)),
        out_shape=jax.ShapeDtypeStruct((T, N), BF16),
        compiler_params=_params(("parallel", "arbitrary"), ((tm, K), BF16, 2), ((K, tn), BF16, 2),
                                ((tm, tn), BF16, 2), ((tm, tn), F32, 1)),
        name="proj",
    )(x, w_all)


def _norm_mm_kernel(x_ref, g_ref, w_ref, o_ref, xn_ref):
    @pl.when(pl.program_id(1) == 0)
    def _():
        xn_ref[...] = _rmsnorm_rows(x_ref[...], g_ref[...]).astype(xn_ref.dtype)

    o_ref[...] = jnp.dot(xn_ref[...], w_ref[...], preferred_element_type=F32).astype(o_ref.dtype)


def _norm_matmul(x, g, w_all, layer):
    T, K = x.shape
    N = w_all.shape[2]
    tm = _tile(T, PROJ_ROWS)
    tn = _tile(N, PROJ_COLS, V7X_LANES)
    return pl.pallas_call(
        _norm_mm_kernel,
        grid=(T // tm, N // tn),
        in_specs=[pl.BlockSpec((tm, K), lambda i, j: (i, 0)),
                  pl.BlockSpec((1, K), lambda i, j: (0, 0)),
                  pl.BlockSpec((None, K, tn), lambda i, j: (layer, 0, j))],
        out_specs=pl.BlockSpec((tm, tn), lambda i, j: (i, j)),
        out_shape=jax.ShapeDtypeStruct((T, N), BF16),
        scratch_shapes=[pltpu.VMEM((tm, K), BF16)],
        compiler_params=_params(("parallel", "arbitrary"), ((tm, K), F32, 3), ((tm, K), BF16, 1),
                                ((K, tn), BF16, 2), ((tm, tn), BF16, 2), ((tm, tn), F32, 1)),
        name="norm_proj",
    )(x, g.reshape(1, K), w_all)


def _ret_in_kernel(x_ref, w_ref, cos_ref, sin_ref, o_ref, *, sub):
    j = pl.program_id(1)
    tm = x_ref.shape[0]
    half = RET_DK // 2
    v_tile = 2
    gate_tile = 4

    def blocks(finish):
        for r0 in range(0, tm, sub):
            rows = slice(r0, r0 + sub)
            finish(jnp.dot(x_ref[rows, :], w_ref[...], preferred_element_type=F32), rows)

    def rotary(scale):
        def finish(a, rows):
            cs = cos_ref[rows, :]
            sn = sin_ref[rows, :]
            if scale != 1.0:
                cs, sn = cs * scale, sn * scale
            for h in range(RET_HEADS):
                c0 = h * RET_DK
                x1, x2 = a[:, c0:c0 + half], a[:, c0 + half:c0 + RET_DK]
                o_ref[rows, c0:c0 + half] = (x1 * cs - x2 * sn).astype(o_ref.dtype)
                o_ref[rows, c0 + half:c0 + RET_DK] = (x2 * cs + x1 * sn).astype(o_ref.dtype)
        return finish

    def plain():
        o_ref[...] = jnp.dot(x_ref[...], w_ref[...], preferred_element_type=F32).astype(o_ref.dtype)

    def gate(a, rows):
        o_ref[rows, :] = jax.nn.silu(a).astype(o_ref.dtype)

    @pl.when(j == 0)
    def _():
        blocks(rotary(1.0))

    @pl.when(j == 1)
    def _():
        blocks(rotary(RET_DK ** -0.5))

    @pl.when(jnp.logical_and(j >= v_tile, j < gate_tile))
    def _():
        plain()

    @pl.when(j >= gate_tile)
    def _():
        blocks(gate)


def _ret_in_proj(x, w_all, layer, cos_t, sin_t):
    T, K = x.shape
    tn = RET_QK_WIDTH
    tm = _tile(T, PROJ_ROWS)
    half = RET_DK // 2
    return pl.pallas_call(
        functools.partial(_ret_in_kernel, sub=_tile(tm, RET_IN_BLOCK)),
        grid=(T // tm, RET_IN_WIDTH // tn),
        in_specs=[pl.BlockSpec((tm, K), lambda i, j: (i, 0)),
                  pl.BlockSpec((None, K, tn), lambda i, j: (layer, 0, j)),
                  pl.BlockSpec((tm, half), lambda i, j: (i, 0)),
                  pl.BlockSpec((tm, half), lambda i, j: (i, 0))],
        out_specs=pl.BlockSpec((tm, tn), lambda i, j: (i, j)),
        out_shape=jax.ShapeDtypeStruct((T, RET_IN_WIDTH), BF16),
        compiler_params=_params(("parallel", "arbitrary"), ((tm, K), BF16, 2), ((K, tn), BF16, 2),
                                ((tm, tn), BF16, 2), ((tm, half), F32, 4), ((tm // 4, tn), F32, 4)),
        name="ret_in_proj",
    )(x, w_all, cos_t, sin_t)


def _mla_prep_kernel(cq_ref, ckv_ref, kpe_ref, m_ref, qg_ref, kvg_ref, wq_ref, wkv_ref,
                     q_ref, k_ref, v_ref):
    scale = (MLA_NOPE + MLA_ROPE) ** -0.5
    m = m_ref[...]
    cqn = _rmsnorm_rows(cq_ref[...].astype(F32), qg_ref[...]).astype(BF16)
    q = jnp.dot(cqn, wq_ref[...], preferred_element_type=F32)
    for h in range(MLA_HEADS):
        c0 = h * MLA_QK_PAD
        q_ref[:, c0:c0 + MLA_NOPE] = (q[:, c0:c0 + MLA_NOPE] * scale).astype(BF16)
        y = q[:, c0 + MLA_NOPE:c0 + MLA_QK_PAD] * m
        y = y + pltpu.roll(y, MLA_ROPE, axis=1)
        q_ref[:, c0 + MLA_NOPE:c0 + MLA_QK_PAD] = (y * scale).astype(BF16)
    ckvn = _rmsnorm_rows(ckv_ref[...].astype(F32), kvg_ref[...]).astype(BF16)
    kv = jnp.dot(ckvn, wkv_ref[...], preferred_element_type=F32)
    yk = kpe_ref[...].astype(F32) * m
    yk = yk + pltpu.roll(yk, MLA_ROPE, axis=1)
    lane = lax.broadcasted_iota(jnp.int32, yk.shape, 1)
    k_rot = jnp.where(lane < MLA_ROPE, yk, 0.0).astype(BF16)
    for h in range(MLA_HEADS):
        c0 = h * MLA_QK_PAD
        k_ref[:, c0:c0 + MLA_NOPE] = kv[:, h * MLA_NOPE:(h + 1) * MLA_NOPE].astype(BF16)
        k_ref[:, c0 + MLA_NOPE:c0 + MLA_QK_PAD] = k_rot
    v_ref[...] = kv[:, MLA_HEADS * MLA_NOPE:].astype(BF16)


def _mla_prep(z, m_tab, q_norm_g, kv_norm_g, wq, wkv):
    T = z.shape[0]
    tm = _tile(T, PROJ_ROWS)
    qkw = MLA_HEADS * MLA_QK_PAD
    u_v = 2 * SGU_WIDTH
    return pl.pallas_call(
        _mla_prep_kernel,
        grid=(T // tm,),
        in_specs=[
            pl.BlockSpec((tm, Q_LORA), lambda i: (i, u_v // Q_LORA)),
            pl.BlockSpec((tm, KV_LORA), lambda i: (i, (u_v + Q_LORA) // KV_LORA)),
            pl.BlockSpec((tm, 2 * MLA_ROPE), lambda i: (i, (u_v + Q_LORA + KV_LORA) // (2 * MLA_ROPE))),
            pl.BlockSpec((tm, V7X_LANES), lambda i: (i, 0)),
            pl.BlockSpec((1, Q_LORA), lambda i: (0, 0)),
            pl.BlockSpec((1, KV_LORA), lambda i: (0, 0)),
            _const_spec((Q_LORA, qkw), lambda i: (0, 0)),
            _const_spec((KV_LORA, qkw), lambda i: (0, 0)),
        ],
        out_specs=[pl.BlockSpec((tm, qkw), lambda i: (i, 0)),
                   pl.BlockSpec((tm, qkw), lambda i: (i, 0)),
                   pl.BlockSpec((tm, MLA_OUT), lambda i: (i, 0))],
        out_shape=[jax.ShapeDtypeStruct((T, qkw), BF16), jax.ShapeDtypeStruct((T, qkw), BF16),
                   jax.ShapeDtypeStruct((T, MLA_OUT), BF16)],
        compiler_params=_params(("parallel",), ((tm, qkw), BF16, 8), ((tm, qkw), F32, 3),
                                ((Q_LORA + KV_LORA, qkw), BF16, 1)),
        name="mla_prep",
    )(z, z, z, m_tab, q_norm_g.reshape(1, -1), kv_norm_g.reshape(1, -1), wq, wkv)


def _mla_attn_kernel(q_ref, k_ref, v_ref, o_ref, va_ref, *, tq):
    S = q_ref.shape[0]
    r = lax.broadcasted_iota(jnp.int32, (tq, tq), 0) // CHUNK
    c = lax.broadcasted_iota(jnp.int32, (tq, tq), 1) // CHUNK
    bias = jnp.where(c <= r, 0.0, -1e30).astype(F32)
    va_ref[:, 0:MLA_VDIM] = v_ref[...]
    va_ref[:, MLA_VDIM:] = jnp.ones((S, MLA_VDIM), BF16)
    for qi in range(S // tq):
        past = qi * tq
        q = q_ref[past:past + tq, :]
        s = lax.dot_general(q, k_ref[0:past + tq, :], (((1,), (1,)), ((), ())),
                            preferred_element_type=F32)
        s_diag = s[:, past:] + bias
        s = s_diag if qi == 0 else jnp.concatenate([s[:, :past], s_diag], axis=1)
        mx = jnp.max(s, axis=-1, keepdims=True)
        p = jnp.exp(s - mx)
        oa = jnp.dot(p.astype(BF16), va_ref[0:past + tq, :], preferred_element_type=F32)
        o_ref[past:past + tq, :] = (oa[:, :MLA_VDIM] / oa[:, MLA_VDIM:]).astype(o_ref.dtype)


def _mla_attn(q, k, v, B, S):
    tq = _tile(S, ATTN_Q_ROWS, CHUNK)
    q3 = q.reshape(B, S, -1)
    k3 = k.reshape(B, S, -1)
    v3 = v.reshape(B, S, -1)
    out = pl.pallas_call(
        functools.partial(_mla_attn_kernel, tq=tq),
        grid=(B, MLA_HEADS),
        scratch_shapes=[pltpu.VMEM((S, 2 * MLA_VDIM), BF16)],
        in_specs=[pl.BlockSpec((None, S, MLA_QK_PAD), lambda b, h: (b, 0, h)),
                  pl.BlockSpec((None, S, MLA_QK_PAD), lambda b, h: (b, 0, h)),
                  pl.BlockSpec((None, S, MLA_VDIM), lambda b, h: (b, 0, h))],
        out_specs=pl.BlockSpec((None, S, MLA_VDIM), lambda b, h: (b, 0, h)),
        out_shape=jax.ShapeDtypeStruct((B, S, MLA_OUT), BF16),
        compiler_params=_params(("parallel", "parallel"), ((S, MLA_QK_PAD), BF16, 4),
                                ((S, MLA_VDIM), BF16, 4), ((tq, S), F32, 4)),
        name="mla_attn",
    )(q3, k3, v3)
    return out.reshape(B * S, MLA_OUT)


def _even_out_kernel(a_ref, u_ref, v_ref, h_ref, lng_ref, lnb_ref, ws_ref, bs_ref, wo_ref, g_ref,
                     o_ref, on_ref, wm_ref, bo_ref, acc_ref):
    tm = a_ref.shape[0]
    pr = lax.broadcasted_iota(jnp.int32, (SGU_BLOCK, SGU_BLOCK), 0) // CHUNK
    pc = lax.broadcasted_iota(jnp.int32, (SGU_BLOCK, SGU_BLOCK), 1) // CHUNK
    for g in range(SGU_GROUPS):
        wm_ref[g] = jnp.where(pr >= pc, ws_ref[g], 0.0).astype(BF16)
    gu = jax.nn.gelu(u_ref[...].astype(F32))
    gv = jax.nn.gelu(v_ref[...].astype(F32))
    mu = jnp.mean(gv, axis=-1, keepdims=True)
    xc = gv - mu
    y = xc * lax.rsqrt(jnp.mean(xc * xc, axis=-1, keepdims=True) + EPS)
    vn = (y * lng_ref[...] + lnb_ref[...]).astype(BF16)
    acc_ref[...] = jnp.dot(a_ref[...], wo_ref[0:MLA_OUT, :], preferred_element_type=F32)
    for g in range(SGU_GROUPS):
        bcol = bs_ref[:, g:g + 1]
        cols = slice(g * SGU_CH, (g + 1) * SGU_CH)
        for nb in range(tm // SGU_BLOCK):
            rr = slice(nb * SGU_BLOCK, (nb + 1) * SGU_BLOCK)
            s = jnp.dot(wm_ref[g], vn[rr, cols], preferred_element_type=F32) + bcol
            bo_ref[rr, cols] = (gu[rr, cols] * s).astype(BF16)
    acc = acc_ref[...] + jnp.dot(bo_ref[...], wo_ref[MLA_OUT:, :], preferred_element_type=F32)
    hnew = h_ref[...] + acc
    o_ref[...] = hnew
    on_ref[...] = _rmsnorm_rows(hnew, g_ref[...]).astype(on_ref.dtype)


def _even_out(a, z, h, ln_g, ln_b, w_s, b_s, wo_all, layer, g_next):
    T, D = h.shape
    tm = _tile(T, RES_ROWS, SGU_BLOCK)
    row = lambda i: (i, 0)
    fix = lambda i: (0, 0)
    return pl.pallas_call(
        _even_out_kernel,
        grid=(T // tm,),
        in_specs=[pl.BlockSpec((tm, MLA_OUT), row),
                  pl.BlockSpec((tm, SGU_WIDTH), lambda i: (i, 0)),
                  pl.BlockSpec((tm, SGU_WIDTH), lambda i: (i, 1)),
                  pl.BlockSpec((tm, D), row),
                  pl.BlockSpec((1, SGU_WIDTH), fix),
                  pl.BlockSpec((1, SGU_WIDTH), fix),
                  _const_spec((SGU_GROUPS, SGU_BLOCK, SGU_BLOCK), lambda i: (0, 0, 0)),
                  pl.BlockSpec((SGU_BLOCK, SGU_GROUPS), fix),
                  _const_spec((None, MLA_OUT + SGU_WIDTH, D), lambda i: (layer, 0, 0)),
                  pl.BlockSpec((1, D), fix)],
        out_specs=[pl.BlockSpec((tm, D), row), pl.BlockSpec((tm, D), row)],
        out_shape=[jax.ShapeDtypeStruct((T, D), F32), jax.ShapeDtypeStruct((T, D), BF16)],
        scratch_shapes=[pltpu.VMEM((SGU_GROUPS, SGU_BLOCK, SGU_BLOCK), BF16), pltpu.VMEM((tm, SGU_WIDTH), BF16),
                        pltpu.VMEM((tm, D), F32)],
        compiler_params=_params(("parallel",), ((tm, D), F32, 7), ((tm, D), BF16, 6),
                                ((MLA_OUT + SGU_WIDTH, D), BF16, 1), ((tm, SGU_WIDTH), F32, 4)),
        name="even_out",
    )(a, z, z, h, ln_g.reshape(1, -1), ln_b.reshape(1, -1), w_s, b_s.T, wo_all, g_next.reshape(1, D))


def _retention_kernel(q_ref, k_ref, v_ref, gate_ref, dm_ref, qd_ref, kd_ref, sd_ref,
                      gng_ref, gnb_ref, o_ref, state_ref, *, L):
    S = q_ref.shape[0]
    state_ref[...] = jnp.zeros_like(state_ref)
    dm = dm_ref[...]
    qd = qd_ref[...]
    kd = kd_ref[...]
    sd = sd_ref[...]
    gng = gng_ref[...]
    gnb = gnb_ref[...]

    def step(c):
        rows = slice(c * L, (c + 1) * L)
        qb = q_ref[rows, :]
        kb = k_ref[rows, :]
        vb = v_ref[rows, :]
        sc = lax.dot_general(qb, kb, (((1,), (1,)), ((), ())), preferred_element_type=F32) * dm
        intra = jnp.dot(sc.astype(BF16), vb, preferred_element_type=F32)
        st = state_ref[...]
        cross = jnp.dot(qb, st.astype(BF16), preferred_element_type=F32) * qd
        kz = (kb.astype(F32) * kd).astype(BF16)
        state_ref[...] = st * sd + lax.dot_general(kz, vb, (((0,), (0,)), ((), ())),
                                                   preferred_element_type=F32)
        out = intra + cross
        mu = jnp.mean(out, axis=-1, keepdims=True)
        xc = out - mu
        y = xc * lax.rsqrt(jnp.mean(xc * xc, axis=-1, keepdims=True) + EPS)
        y = y * gng + gnb
        o_ref[rows, :] = (y * gate_ref[rows, :].astype(F32)).astype(o_ref.dtype)

    for c in range(S // L):
        step(c)


def _retention_tables(L):
    log_g = jnp.log1p(-(2.0 ** (-5.0 - jnp.arange(RET_HEADS, dtype=F32))))
    idx = jnp.arange(L, dtype=F32)
    chunk = jnp.arange(L) // CHUNK
    decay = jnp.exp(log_g[:, None, None] * jnp.abs(idx[:, None] - idx[None, :]))
    dm = jnp.where(chunk[None, :, None] >= chunk[None, None, :], decay, 0.0)
    qd = jnp.exp(log_g[:, None] * (idx + 1.0)[None, :])[:, :, None]
    kd = jnp.exp(log_g[:, None] * (L - 1 - idx)[None, :])[:, :, None]
    sd = jnp.broadcast_to(jnp.exp(log_g * L)[:, None, None], (RET_HEADS, 1, RET_DV))
    return dm, qd, kd, sd


def _retention(z, gn_g, gn_b, B, S):
    L = _tile(S, RET_SUPER, CHUNK)
    z3 = z.reshape(B, S, RET_IN_WIDTH)
    dm, qd, kd, sd = _retention_tables(L)
    qk_blocks = RET_QK_WIDTH // RET_DK
    v_first = 2 * RET_QK_WIDTH // RET_DV
    g_first = v_first + RET_V_WIDTH // RET_DV
    out = pl.pallas_call(
        functools.partial(_retention_kernel, L=L),
        grid=(B, RET_HEADS),
        in_specs=[pl.BlockSpec((None, S, RET_DK), lambda b, h: (b, 0, h)),
                  pl.BlockSpec((None, S, RET_DK), lambda b, h: (b, 0, qk_blocks + h)),
                  pl.BlockSpec((None, S, RET_DV), lambda b, h: (b, 0, v_first + h)),
                  pl.BlockSpec((None, S, RET_DV), lambda b, h: (b, 0, g_first + h)),
                  pl.BlockSpec((None, L, L), lambda b, h: (h, 0, 0)),
                  pl.BlockSpec((None, L, 1), lambda b, h: (h, 0, 0)),
                  pl.BlockSpec((None, L, 1), lambda b, h: (h, 0, 0)),
                  pl.BlockSpec((None, 1, RET_DV), lambda b, h: (h, 0, 0)),
                  pl.BlockSpec((1, RET_DV), lambda b, h: (0, h)),
                  pl.BlockSpec((1, RET_DV), lambda b, h: (0, h))],
        out_specs=pl.BlockSpec((None, S, RET_DV), lambda b, h: (b, 0, h)),
        out_shape=jax.ShapeDtypeStruct((B, S, RET_V_WIDTH), BF16),
        scratch_shapes=[pltpu.VMEM((RET_DK, RET_DV), F32)],
        compiler_params=_params(("parallel", "arbitrary"), ((S, RET_DK), BF16, 4), ((S, RET_DV), BF16, 6),
                                ((L, L), F32, 4), ((L, V7X_LANES), F32, 4),
                                ((RET_DK, RET_DV), F32, 4), ((L, RET_DV), F32, 8)),
        name="retention",
    )(z3, z3, z3, z3, dm, qd, kd, sd, gn_g.reshape(1, -1), gn_b.reshape(1, -1))
    return out.reshape(B * S, RET_V_WIDTH)


def _res_mm_kernel(x_ref, w_ref, h_ref, g_ref, o_ref, on_ref):
    hnew = h_ref[...] + jnp.dot(x_ref[...], w_ref[...], preferred_element_type=F32)
    o_ref[...] = hnew
    on_ref[...] = _rmsnorm_rows(hnew, g_ref[...]).astype(on_ref.dtype)


def _res_matmul(x, w_all, layer, h, g_next):
    T, K = x.shape
    D = h.shape[1]
    tm = _tile(T, RES_ROWS)
    row = lambda i: (i, 0)
    fix = lambda i: (0, 0)
    return pl.pallas_call(
        _res_mm_kernel,
        grid=(T // tm,),
        in_specs=[pl.BlockSpec((tm, K), row), _const_spec((None, K, D), lambda i: (layer, 0, 0)),
                  pl.BlockSpec((tm, D), row), pl.BlockSpec((1, D), fix)],
        out_specs=[pl.BlockSpec((tm, D), row), pl.BlockSpec((tm, D), row)],
        out_shape=[jax.ShapeDtypeStruct((T, D), F32), jax.ShapeDtypeStruct((T, D), BF16)],
        compiler_params=_params(("parallel",), ((tm, K), BF16, 2), ((K, D), BF16, 1),
                                ((tm, D), F32, 6), ((tm, D), BF16, 2)),
        name="res_proj",
    )(x, w_all, h, g_next.reshape(1, D))


def _ffn_up_kernel(x_ref, wg_ref, wv_ref, cwg_ref, cwv_ref, cbg_ref, cbv_ref, o_ref, wb_ref, e_ref,
                   *, tiles_per_seq, sub):
    tm = x_ref.shape[0]
    tn = o_ref.shape[1]
    halo = V7X_SUBLANES
    lanes = V7X_LANES
    nslab = tn // lanes
    i = pl.program_id(1)
    first = (i % tiles_per_seq) == 0

    @pl.when(i == 0)
    def _():
        wb_ref[:, 0:tn] = wg_ref[...].astype(BF16)
        wb_ref[:, tn:2 * tn] = wv_ref[...].astype(BF16)

    @pl.when(first)
    def _():
        e_ref[:, 0:halo, :] = jnp.zeros((e_ref.shape[0], halo, lanes), F32)

    @pl.when(jnp.logical_not(first))
    def _():
        e_ref[:, 0:halo, :] = e_ref[:, tm:tm + halo, :]

    def conv(k, r0, cw_ref, cb_ref, kc):
        cols = slice(kc * lanes, (kc + 1) * lanes)
        c = cb_ref[:, cols]
        for j in range(CONV_WIDTH):
            off = halo + r0 - (CONV_WIDTH - 1) + j
            c = c + e_ref[k, off:off + sub, :] * cw_ref[j:j + 1, cols]
        return c

    for r0 in range(0, tm, sub):
        a = jnp.dot(x_ref[r0:r0 + sub, :], wb_ref[...], preferred_element_type=F32)
        for k in range(2 * nslab):
            e_ref[k, halo + r0:halo + r0 + sub, :] = a[:, k * lanes:(k + 1) * lanes]
        for k in range(nslab):
            gate = conv(k, r0, cwg_ref, cbg_ref, k)
            val = conv(k + nslab, r0, cwv_ref, cbv_ref, k)
            o_ref[r0:r0 + sub, k * lanes:(k + 1) * lanes] = (jax.nn.gelu(gate) * val).astype(o_ref.dtype)


def _ffn_up(hn, w_up_all, conv_w_all, conv_b_all, layer, S):
    T, D = hn.shape
    tn = FFN_UP_COLS
    tm = _tile(S, FFN_UP_ROWS)
    sub = _tile(tm, FFN_UP_BLOCK)
    nj = D_FF // tn
    halo = V7X_SUBLANES
    conv_b3 = conv_b_all.reshape(conv_b_all.shape[0], 1, 2 * D_FF)
    gate_col = lambda j, i: (layer, 0, j)
    val_col = lambda j, i: (layer, 0, nj + j)
    return pl.pallas_call(
        functools.partial(_ffn_up_kernel, tiles_per_seq=S // tm, sub=sub),
        grid=(nj, T // tm),
        in_specs=[pl.BlockSpec((tm, D), lambda j, i: (i, 0)),
                  pl.BlockSpec((None, D, tn), gate_col),
                  pl.BlockSpec((None, D, tn), val_col),
                  pl.BlockSpec((None, CONV_WIDTH, tn), gate_col),
                  pl.BlockSpec((None, CONV_WIDTH, tn), val_col),
                  pl.BlockSpec((None, 1, tn), gate_col),
                  pl.BlockSpec((None, 1, tn), val_col)],
        out_specs=pl.BlockSpec((tm, tn), lambda j, i: (i, j)),
        out_shape=jax.ShapeDtypeStruct((T, D_FF), BF16),
        scratch_shapes=[pltpu.VMEM((D, 2 * tn), BF16),
                        pltpu.VMEM((2 * tn // V7X_LANES, tm + halo, V7X_LANES), F32)],
        compiler_params=_params(("arbitrary", "arbitrary"), ((tm, D), BF16, 2), ((D, tn), F32, 4),
                                ((D, 2 * tn), BF16, 1), ((tm + halo, 2 * tn), F32, 1),
                                ((sub, 2 * tn), F32, 2), ((tm, tn), BF16, 2)),
        name="ffn_up",
    )(hn, w_up_all, w_up_all, conv_w_all, conv_w_all, conv_b3, conv_b3)


def _ple_kernel(hn_ref, wg_ref, p_ref, wu_ref, h_ref, g_ref, *out_refs):
    on_ref = out_refs[-1]
    gate = jax.nn.sigmoid(jnp.dot(hn_ref[...], wg_ref[...], preferred_element_type=F32))
    up = jnp.dot(p_ref[...].astype(BF16), wu_ref[...], preferred_element_type=F32)
    hnew = h_ref[...] + up * gate
    if len(out_refs) == 2:
        out_refs[0][...] = hnew
    on_ref[...] = _rmsnorm_rows(hnew, g_ref[...]).astype(on_ref.dtype)


def _ple(hn, wg_all, p_all, layer, wu_all, h, g_next, last):
    T, D = h.shape
    tm = _tile(T, RES_ROWS)
    row = lambda i: (i, 0)
    fix = lambda i: (0, 0)
    out_specs = [pl.BlockSpec((tm, D), row), pl.BlockSpec((tm, D), row)]
    out_shape = [jax.ShapeDtypeStruct((T, D), F32), jax.ShapeDtypeStruct((T, D), BF16)]
    if last:
        out_specs, out_shape = out_specs[:1], out_shape[:1]
    return pl.pallas_call(
        _ple_kernel,
        grid=(T // tm,),
        in_specs=[pl.BlockSpec((tm, D), row),
                  _const_spec((None, D, D), lambda i: (layer, 0, 0)),
                  pl.BlockSpec((None, tm, PLE_DIM), lambda i: (layer, i, 0)),
                  _const_spec((None, PLE_DIM, D), lambda i: (layer, 0, 0)),
                  pl.BlockSpec((tm, D), row),
                  pl.BlockSpec((1, D), fix)],
        out_specs=out_specs,
        out_shape=out_shape,
        compiler_params=_params(("parallel",), ((tm, D), BF16, 2), ((D, D), BF16, 1), ((PLE_DIM, D), BF16, 1),
                                ((tm, PLE_DIM), F32, 2), ((tm, D), F32, 8)),
        name="ple",
    )(hn, wg_all, p_all, wu_all, h, g_next.reshape(1, D))


def _rot_cols(w):
    half = MLA_ROPE // 2
    return jnp.concatenate([-w[..., half:], w[..., :half]], axis=-1)


def _even_in_weight(w):
    w = w.astype(BF16)
    cq = w[..., :Q_LORA]
    ckv = w[..., Q_LORA:Q_LORA + KV_LORA]
    kpe = w[..., Q_LORA + KV_LORA:Q_LORA + KV_LORA + MLA_ROPE]
    uv = w[..., Q_LORA + KV_LORA + MLA_ROPE:]
    used = 2 * SGU_WIDTH + Q_LORA + KV_LORA + 2 * MLA_ROPE
    pad = jnp.zeros(w.shape[:-1] + (EVEN_IN_PAD - used,), w.dtype)
    return jnp.concatenate([uv, cq, ckv, kpe, _rot_cols(kpe), pad], axis=-1).astype(BF16)


def _q_up_weight(w):
    w = w.reshape(Q_LORA, MLA_HEADS, MLA_NOPE + MLA_ROPE)
    rope = w[..., MLA_NOPE:]
    w = jnp.concatenate([w[..., :MLA_NOPE], rope, _rot_cols(rope)], axis=-1)
    return w.reshape(Q_LORA, MLA_HEADS * MLA_QK_PAD).astype(BF16)


def _kv_up_weight(w):
    w = w.reshape(KV_LORA, MLA_HEADS, MLA_NOPE + MLA_VDIM)
    k = w[..., :MLA_NOPE].reshape(KV_LORA, -1)
    v = w[..., MLA_NOPE:].reshape(KV_LORA, -1)
    return jnp.concatenate([k, v], axis=1).astype(BF16)


def kernel(x, p, positions, mix_norm_g, even_w_in, mla_q_norm_g, mla_w_q_up, mla_kv_norm_g, mla_w_kv_up,
           sgu_ln_g, sgu_ln_b, sgu_w_s, sgu_b_s, even_w_out, ret_w_in, ret_gn_g, ret_gn_b, ret_w_out,
           ffn_norm_g, ffn_w_up, ffn_conv_w, ffn_conv_b, ffn_w_down, ple_norm_g, ple_w_gate, ple_w_up,
           final_norm_g):
    B, S, D = x.shape
    depth = p.shape[0]
    T = B * S
    assert D == D_MODEL and S % SGU_BLOCK == 0
    cos_t, sin_t, m_tab = _rope_tables(positions)
    h = x.reshape(T, D)
    p_all = p.reshape(depth, T, PLE_DIM)
    even_in_b = _even_in_weight(even_w_in)
    even_out_b = even_w_out.astype(BF16)
    ret_in_b = ret_w_in.astype(BF16)
    ret_out_b = ret_w_out.astype(BF16)
    ffn_down_b = ffn_w_down.astype(BF16)
    ple_gate_b = ple_w_gate.astype(BF16)
    ple_up_b = ple_w_up.astype(BF16)
    for i in range(depth):
        j = i // 2
        if i % 2 == 0:
            if i == 0:
                z = _norm_matmul(h, mix_norm_g[0], even_in_b, j)
            else:
                z = _matmul(hn, even_in_b, j)
            q, k, v = _mla_prep(z, m_tab, mla_q_norm_g[j], mla_kv_norm_g[j],
                                _q_up_weight(mla_w_q_up[j]), _kv_up_weight(mla_w_kv_up[j]))
            a = _mla_attn(q, k, v, B, S)
            h, hn = _even_out(a, z, h, sgu_ln_g[j], sgu_ln_b[j], sgu_w_s[j], sgu_b_s[j],
                              even_out_b, j, ffn_norm_g[i])
        else:
            z = _ret_in_proj(hn, ret_in_b, j, cos_t, sin_t)
            r = _retention(z, ret_gn_g[j], ret_gn_b[j], B, S)
            h, hn = _res_matmul(r, ret_out_b, j, h, ffn_norm_g[i])
        act = _ffn_up(hn, ffn_w_up, ffn_conv_w, ffn_conv_b, i, S)
        h, hn = _res_matmul(act, ffn_down_b, i, h, ple_norm_g[i])
        if i < depth - 1:
            h, hn = _ple(hn, ple_gate_b, p_all, i, ple_up_b, h, mix_norm_g[i + 1], False)
        else:
            out, = _ple(hn, ple_gate_b, p_all, i, ple_up_b, h, final_norm_g, True)
    return out.reshape(B, S, D)
```

```python
import functools
import math

import jax
import jax.numpy as jnp
from jax import lax
from jax.experimental import pallas as pl
from jax.experimental.pallas import tpu as pltpu

F32 = jnp.float32
BF16 = jnp.bfloat16

D_MODEL = 2048
CHUNK = 64
EPS = 1e-6
ROPE_THETA = 10000.0
MLA_HEADS = 8
MLA_NOPE = 128
MLA_ROPE = 64
MLA_VDIM = 128
Q_LORA = 512
KV_LORA = 256
MLA_OUT = MLA_HEADS * MLA_VDIM
MLA_QK_PAD = 256
SGU_BLOCK = 128
SGU_GROUPS = 8
SGU_CH = 128
SGU_WIDTH = SGU_GROUPS * SGU_CH
EVEN_IN_PAD = 3072
RET_HEADS = 8
RET_DK = 256
RET_DV = 512
RET_QK_WIDTH = RET_HEADS * RET_DK
RET_V_WIDTH = RET_HEADS * RET_DV
RET_IN_WIDTH = 2 * RET_QK_WIDTH + 2 * RET_V_WIDTH
RET_SUPER = 256
D_FF = 5632
CONV_WIDTH = 3
PLE_DIM = 256

V7X_LANES = 128
V7X_SUBLANES = 8
V7X_VMEM_BUDGET = 58 * 1024 * 1024
V7X_VMEM_UNLISTED = 4 * 1024 * 1024

PROJ_ROWS = 1024
PROJ_COLS = 1536
RES_ROWS = 512
RET_IN_BLOCK = 256
FFN_UP_ROWS = 2048
FFN_UP_BLOCK = 1024
FFN_UP_COLS = 512
ATTN_Q_ROWS = 512
TABLE_ROWS = 1024


def _nbytes(shape, dtype):
    return math.prod(shape) * jnp.dtype(dtype).itemsize


def _params(semantics, *buffers):
    need = sum(_nbytes(s, d) * n for s, d, n in buffers) + V7X_VMEM_UNLISTED
    return pltpu.CompilerParams(dimension_semantics=semantics,
                                vmem_limit_bytes=min(V7X_VMEM_BUDGET, need))


def _tile(n, pref, mult=V7X_SUBLANES):
    t = min(n, pref)
    while n % t or t % mult:
        t -= 1
    return t


def _const_spec(shape, index_map):
    return pl.BlockSpec(shape, index_map, pipeline_mode=pl.Buffered(1))


def _rmsnorm_rows(x, g):
    ms = jnp.mean(x * x, axis=-1, keepdims=True)
    return x * lax.rsqrt(ms + EPS) * g


def _rope_tables_kernel(pos_ref, fr_ref, fm_ref, cos_ref, sin_ref, m_ref):
    pos = pos_ref[...].astype(F32)
    ang = pos * fr_ref[...]
    cos_ref[...] = jnp.cos(ang)
    sin_ref[...] = jnp.sin(ang)
    angm = pos * fm_ref[...]
    lane = lax.broadcasted_iota(jnp.int32, angm.shape, 1)
    m_ref[...] = jnp.where(lane < MLA_ROPE, jnp.cos(angm), jnp.sin(angm))


def _rope_tables(positions):
    T = positions.size
    tm = _tile(T, TABLE_ROWS)
    half_r = RET_DK // 2
    half_m = MLA_ROPE // 2
    fr = (ROPE_THETA ** (-jnp.arange(half_r, dtype=F32) / half_r))[None, :]
    fm1 = ROPE_THETA ** (-jnp.arange(half_m, dtype=F32) / half_m)
    fm = jnp.tile(fm1, 4)[None, :]
    tab = jax.ShapeDtypeStruct((T, V7X_LANES), F32)
    row = pl.BlockSpec((tm, V7X_LANES), lambda i: (i, 0))
    frq = pl.BlockSpec((1, V7X_LANES), lambda i: (0, 0))
    return pl.pallas_call(
        _rope_tables_kernel,
        grid=(T // tm,),
        in_specs=[pl.BlockSpec((tm, 1), lambda i: (i, 0)), frq, frq],
        out_specs=[row, row, row],
        out_shape=[tab, tab, tab],
        compiler_params=_params(("parallel",), ((tm, V7X_LANES), F32, 16)),
        name="rope_tables",
    )(positions.reshape(T, 1), fr, fm)


def _mm_kernel(x_ref, w_ref, o_ref):
    o_ref[...] = jnp.dot(x_ref[...], w_ref[...], preferred_element_type=F32).astype(o_ref.dtype)


def _matmul(x, w_all, layer):
    T, K = x.shape
    N = w_all.shape[2]
    tm = _tile(T, PROJ_ROWS)
    tn = _tile(N, PROJ_COLS, V7X_LANES)
    return pl.pallas_call(
        _mm_kernel,
        grid=(T // tm, N // tn),
        in_specs=[pl.BlockSpec((tm, K), lambda i, j: (i, 0)),
                  pl.BlockSpec((None, K, tn), lambda i, j: (layer, 0, j))],
        out_specs=pl.BlockSpec((tm, tn), lambda i, j: (i, j)),
        out_shape=jax.ShapeDtypeStruct((T, N), BF16),
        compiler_params=_params(("parallel", "arbitrary"), ((tm, K), BF16, 2), ((K, tn), BF16, 2),
                                ((tm, tn), BF16, 2), ((tm, tn), F32, 1)),
        name="proj",
    )(x, w_all)


def _norm_mm_kernel(x_ref, g_ref, w_ref, o_ref, xn_ref):
    @pl.when(pl.program_id(1) == 0)
    def _():
        xn_ref[...] = _rmsnorm_rows(x_ref[...], g_ref[...]).astype(xn_ref.dtype)

    o_ref[...] = jnp.dot(xn_ref[...], w_ref[...], preferred_element_type=F32).astype(o_ref.dtype)


def _norm_matmul(x, g, w_all, layer):
    T, K = x.shape
    N = w_all.shape[2]
    tm = _tile(T, PROJ_ROWS)
    tn = _tile(N, PROJ_COLS, V7X_LANES)
    return pl.pallas_call(
        _norm_mm_kernel,
        grid=(T // tm, N // tn),
        in_specs=[pl.BlockSpec((tm, K), lambda i, j: (i, 0)),
                  pl.BlockSpec((1, K), lambda i, j: (0, 0)),
                  pl.BlockSpec((None, K, tn), lambda i, j: (layer, 0, j))],
        out_specs=pl.BlockSpec((tm, tn), lambda i, j: (i, j)),
        out_shape=jax.ShapeDtypeStruct((T, N), BF16),
        scratch_shapes=[pltpu.VMEM((tm, K), BF16)],
        compiler_params=_params(("parallel", "arbitrary"), ((tm, K), F32, 3), ((tm, K), BF16, 1),
                                ((K, tn), BF16, 2), ((tm, tn), BF16, 2), ((tm, tn), F32, 1)),
        name="norm_proj",
    )(x, g.reshape(1, K), w_all)


def _ret_in_kernel(x_ref, w_ref, cos_ref, sin_ref, o_ref, *, sub):
    j = pl.program_id(1)
    tm = x_ref.shape[0]
    half = RET_DK // 2
    v_tile = 2
    gate_tile = 4

    def blocks(finish):
        for r0 in range(0, tm, sub):
            rows = slice(r0, r0 + sub)
            finish(jnp.dot(x_ref[rows, :], w_ref[...], preferred_element_type=F32), rows)

    def rotary(scale):
        def finish(a, rows):
            cs = cos_ref[rows, :]
            sn = sin_ref[rows, :]
            if scale != 1.0:
                cs, sn = cs * scale, sn * scale
            for h in range(RET_HEADS):
                c0 = h * RET_DK
                x1, x2 = a[:, c0:c0 + half], a[:, c0 + half:c0 + RET_DK]
                o_ref[rows, c0:c0 + half] = (x1 * cs - x2 * sn).astype(o_ref.dtype)
                o_ref[rows, c0 + half:c0 + RET_DK] = (x2 * cs + x1 * sn).astype(o_ref.dtype)
        return finish

    def plain():
        o_ref[...] = jnp.dot(x_ref[...], w_ref[...], preferred_element_type=F32).astype(o_ref.dtype)

    def gate(a, rows):
        o_ref[rows, :] = jax.nn.silu(a).astype(o_ref.dtype)

    @pl.when(j == 0)
    def _():
        blocks(rotary(1.0))

    @pl.when(j == 1)
    def _():
        blocks(rotary(RET_DK ** -0.5))

    @pl.when(jnp.logical_and(j >= v_tile, j < gate_tile))
    def _():
        plain()

    @pl.when(j >= gate_tile)
    def _():
        blocks(gate)


def _ret_in_proj(x, w_all, layer, cos_t, sin_t):
    T, K = x.shape
    tn = RET_QK_WIDTH
    tm = _tile(T, PROJ_ROWS)
    half = RET_DK // 2
    return pl.pallas_call(
        functools.partial(_ret_in_kernel, sub=_tile(tm, RET_IN_BLOCK)),
        grid=(T // tm, RET_IN_WIDTH // tn),
        in_specs=[pl.BlockSpec((tm, K), lambda i, j: (i, 0)),
                  pl.BlockSpec((None, K, tn), lambda i, j: (layer, 0, j)),
                  pl.BlockSpec((tm, half), lambda i, j: (i, 0)),
                  pl.BlockSpec((tm, half), lambda i, j: (i, 0))],
        out_specs=pl.BlockSpec((tm, tn), lambda i, j: (i, j)),
        out_shape=jax.ShapeDtypeStruct((T, RET_IN_WIDTH), BF16),
        compiler_params=_params(("parallel", "arbitrary"), ((tm, K), BF16, 2), ((K, tn), BF16, 2),
                                ((tm, tn), BF16, 2), ((tm, half), F32, 4), ((tm // 4, tn), F32, 4)),
        name="ret_in_proj",
    )(x, w_all, cos_t, sin_t)


def _mla_prep_kernel(cq_ref, ckv_ref, kpe_ref, m_ref, qg_ref, kvg_ref, wq_ref, wkv_ref,
                     q_ref, k_ref, v_ref):
    scale = (MLA_NOPE + MLA_ROPE) ** -0.5
    m = m_ref[...]
    cqn = _rmsnorm_rows(cq_ref[...].astype(F32), qg_ref[...]).astype(BF16)
    q = jnp.dot(cqn, wq_ref[...], preferred_element_type=F32)
    for h in range(MLA_HEADS):
        c0 = h * MLA_QK_PAD
        q_ref[:, c0:c0 + MLA_NOPE] = (q[:, c0:c0 + MLA_NOPE] * scale).astype(BF16)
        y = q[:, c0 + MLA_NOPE:c0 + MLA_QK_PAD] * m
        y = y + pltpu.roll(y, MLA_ROPE, axis=1)
        q_ref[:, c0 + MLA_NOPE:c0 + MLA_QK_PAD] = (y * scale).astype(BF16)
    ckvn = _rmsnorm_rows(ckv_ref[...].astype(F32), kvg_ref[...]).astype(BF16)
    kv = jnp.dot(ckvn, wkv_ref[...], preferred_element_type=F32)
    yk = kpe_ref[...].astype(F32) * m
    yk = yk + pltpu.roll(yk, MLA_ROPE, axis=1)
    lane = lax.broadcasted_iota(jnp.int32, yk.shape, 1)
    k_rot = jnp.where(lane < MLA_ROPE, yk, 0.0).astype(BF16)
    for h in range(MLA_HEADS):
        c0 = h * MLA_QK_PAD
        k_ref[:, c0:c0 + MLA_NOPE] = kv[:, h * MLA_NOPE:(h + 1) * MLA_NOPE].astype(BF16)
        k_ref[:, c0 + MLA_NOPE:c0 + MLA_QK_PAD] = k_rot
    v_ref[...] = kv[:, MLA_HEADS * MLA_NOPE:].astype(BF16)


def _mla_prep(z, m_tab, q_norm_g, kv_norm_g, wq, wkv):
    T = z.shape[0]
    tm = _tile(T, PROJ_ROWS)
    qkw = MLA_HEADS * MLA_QK_PAD
    u_v = 2 * SGU_WIDTH
    return pl.pallas_call(
        _mla_prep_kernel,
        grid=(T // tm,),
        in_specs=[
            pl.BlockSpec((tm, Q_LORA), lambda i: (i, u_v // Q_LORA)),
            pl.BlockSpec((tm, KV_LORA), lambda i: (i, (u_v + Q_LORA) // KV_LORA)),
            pl.BlockSpec((tm, 2 * MLA_ROPE), lambda i: (i, (u_v + Q_LORA + KV_LORA) // (2 * MLA_ROPE))),
            pl.BlockSpec((tm, V7X_LANES), lambda i: (i, 0)),
            pl.BlockSpec((1, Q_LORA), lambda i: (0, 0)),
            pl.BlockSpec((1, KV_LORA), lambda i: (0, 0)),
            _const_spec((Q_LORA, qkw), lambda i: (0, 0)),
            _const_spec((KV_LORA, qkw), lambda i: (0, 0)),
        ],
        out_specs=[pl.BlockSpec((tm, qkw), lambda i: (i, 0)),
                   pl.BlockSpec((tm, qkw), lambda i: (i, 0)),
                   pl.BlockSpec((tm, MLA_OUT), lambda i: (i, 0))],
        out_shape=[jax.ShapeDtypeStruct((T, qkw), BF16), jax.ShapeDtypeStruct((T, qkw), BF16),
                   jax.ShapeDtypeStruct((T, MLA_OUT), BF16)],
        compiler_params=_params(("parallel",), ((tm, qkw), BF16, 8), ((tm, qkw), F32, 3),
                                ((Q_LORA + KV_LORA, qkw), BF16, 1)),
        name="mla_prep",
    )(z, z, z, m_tab, q_norm_g.reshape(1, -1), kv_norm_g.reshape(1, -1), wq, wkv)


def _mla_attn_kernel(q_ref, k_ref, v_ref, o_ref, va_ref, *, tq):
    S = q_ref.shape[0]
    r = lax.broadcasted_iota(jnp.int32, (tq, tq), 0) // CHUNK
    c = lax.broadcasted_iota(jnp.int32, (tq, tq), 1) // CHUNK
    bias = jnp.where(c <= r, 0.0, -1e30).astype(F32)
    va_ref[:, 0:MLA_VDIM] = v_ref[...]
    va_ref[:, MLA_VDIM:] = jnp.ones((S, MLA_VDIM), BF16)
    def scores(qi):
        past = qi * tq
        s = lax.dot_general(q_ref[past:past + tq, :], k_ref[0:past + tq, :], (((1,), (1,)), ((), ())),
                            preferred_element_type=F32)
        s_diag = s[:, past:] + bias
        return s_diag if qi == 0 else jnp.concatenate([s[:, :past], s_diag], axis=1)

    n_q = S // tq
    s = scores(0)
    for qi in range(n_q):
        past = qi * tq
        p = jnp.exp(s - jnp.max(s, axis=-1, keepdims=True))
        if qi + 1 < n_q:
            s = scores(qi + 1)
        oa = jnp.dot(p.astype(BF16), va_ref[0:past + tq, :], preferred_element_type=F32)
        o_ref[past:past + tq, :] = (oa[:, :MLA_VDIM] / oa[:, MLA_VDIM:]).astype(o_ref.dtype)


def _mla_attn(q, k, v, B, S):
    tq = _tile(S, ATTN_Q_ROWS, CHUNK)
    q3 = q.reshape(B, S, -1)
    k3 = k.reshape(B, S, -1)
    v3 = v.reshape(B, S, -1)
    out = pl.pallas_call(
        functools.partial(_mla_attn_kernel, tq=tq),
        grid=(B, MLA_HEADS),
        scratch_shapes=[pltpu.VMEM((S, 2 * MLA_VDIM), BF16)],
        in_specs=[pl.BlockSpec((None, S, MLA_QK_PAD), lambda b, h: (b, 0, h)),
                  pl.BlockSpec((None, S, MLA_QK_PAD), lambda b, h: (b, 0, h)),
                  pl.BlockSpec((None, S, MLA_VDIM), lambda b, h: (b, 0, h))],
        out_specs=pl.BlockSpec((None, S, MLA_VDIM), lambda b, h: (b, 0, h)),
        out_shape=jax.ShapeDtypeStruct((B, S, MLA_OUT), BF16),
        compiler_params=_params(("parallel", "parallel"), ((S, MLA_QK_PAD), BF16, 4),
                                ((S, MLA_VDIM), BF16, 4), ((tq, S), F32, 4)),
        name="mla_attn",
    )(q3, k3, v3)
    return out.reshape(B * S, MLA_OUT)


def _even_out_kernel(a_ref, u_ref, v_ref, h_ref, lng_ref, lnb_ref, ws_ref, bs_ref, wo_ref, g_ref,
                     o_ref, on_ref, wm_ref, bo_ref, acc_ref):
    tm = a_ref.shape[0]
    pr = lax.broadcasted_iota(jnp.int32, (SGU_BLOCK, SGU_BLOCK), 0) // CHUNK
    pc = lax.broadcasted_iota(jnp.int32, (SGU_BLOCK, SGU_BLOCK), 1) // CHUNK
    for g in range(SGU_GROUPS):
        wm_ref[g] = jnp.where(pr >= pc, ws_ref[g], 0.0).astype(BF16)
    gu = jax.nn.gelu(u_ref[...].astype(F32))
    gv = jax.nn.gelu(v_ref[...].astype(F32))
    mu = jnp.mean(gv, axis=-1, keepdims=True)
    xc = gv - mu
    y = xc * lax.rsqrt(jnp.mean(xc * xc, axis=-1, keepdims=True) + EPS)
    vn = (y * lng_ref[...] + lnb_ref[...]).astype(BF16)
    acc_ref[...] = jnp.dot(a_ref[...], wo_ref[0:MLA_OUT, :], preferred_element_type=F32)
    for g in range(SGU_GROUPS):
        bcol = bs_ref[:, g:g + 1]
        cols = slice(g * SGU_CH, (g + 1) * SGU_CH)
        for nb in range(tm // SGU_BLOCK):
            rr = slice(nb * SGU_BLOCK, (nb + 1) * SGU_BLOCK)
            s = jnp.dot(wm_ref[g], vn[rr, cols], preferred_element_type=F32) + bcol
            bo_ref[rr, cols] = (gu[rr, cols] * s).astype(BF16)
    acc = acc_ref[...] + jnp.dot(bo_ref[...], wo_ref[MLA_OUT:, :], preferred_element_type=F32)
    hnew = h_ref[...] + acc
    o_ref[...] = hnew
    on_ref[...] = _rmsnorm_rows(hnew, g_ref[...]).astype(on_ref.dtype)


def _even_out(a, z, h, ln_g, ln_b, w_s, b_s, wo_all, layer, g_next):
    T, D = h.shape
    tm = _tile(T, RES_ROWS, SGU_BLOCK)
    row = lambda i: (i, 0)
    fix = lambda i: (0, 0)
    return pl.pallas_call(
        _even_out_kernel,
        grid=(T // tm,),
        in_specs=[pl.BlockSpec((tm, MLA_OUT), row),
                  pl.BlockSpec((tm, SGU_WIDTH), lambda i: (i, 0)),
                  pl.BlockSpec((tm, SGU_WIDTH), lambda i: (i, 1)),
                  pl.BlockSpec((tm, D), row),
                  pl.BlockSpec((1, SGU_WIDTH), fix),
                  pl.BlockSpec((1, SGU_WIDTH), fix),
                  _const_spec((SGU_GROUPS, SGU_BLOCK, SGU_BLOCK), lambda i: (0, 0, 0)),
                  pl.BlockSpec((SGU_BLOCK, SGU_GROUPS), fix),
                  _const_spec((None, MLA_OUT + SGU_WIDTH, D), lambda i: (layer, 0, 0)),
                  pl.BlockSpec((1, D), fix)],
        out_specs=[pl.BlockSpec((tm, D), row), pl.BlockSpec((tm, D), row)],
        out_shape=[jax.ShapeDtypeStruct((T, D), F32), jax.ShapeDtypeStruct((T, D), BF16)],
        scratch_shapes=[pltpu.VMEM((SGU_GROUPS, SGU_BLOCK, SGU_BLOCK), BF16), pltpu.VMEM((tm, SGU_WIDTH), BF16),
                        pltpu.VMEM((tm, D), F32)],
        compiler_params=_params(("parallel",), ((tm, D), F32, 7), ((tm, D), BF16, 6),
                                ((MLA_OUT + SGU_WIDTH, D), BF16, 1), ((tm, SGU_WIDTH), F32, 4)),
        name="even_out",
    )(a, z, z, h, ln_g.reshape(1, -1), ln_b.reshape(1, -1), w_s, b_s.T, wo_all, g_next.reshape(1, D))


def _retention_kernel(q_ref, k_ref, v_ref, gate_ref, dm_ref, qd_ref, kd_ref, sd_ref,
                      gng_ref, gnb_ref, o_ref, state_ref, *, L):
    S = q_ref.shape[0]
    state_ref[...] = jnp.zeros_like(state_ref)
    dm = dm_ref[...]
    qd = qd_ref[...]
    kd = kd_ref[...]
    sd = sd_ref[...]
    gng = gng_ref[...]
    gnb = gnb_ref[...]

    def step(c):
        rows = slice(c * L, (c + 1) * L)
        qb = q_ref[rows, :]
        kb = k_ref[rows, :]
        vb = v_ref[rows, :]
        sc = lax.dot_general(qb, kb, (((1,), (1,)), ((), ())), preferred_element_type=F32) * dm
        intra = jnp.dot(sc.astype(BF16), vb, preferred_element_type=F32)
        st = state_ref[...]
        cross = jnp.dot(qb, st.astype(BF16), preferred_element_type=F32) * qd
        kz = (kb.astype(F32) * kd).astype(BF16)
        state_ref[...] = st * sd + lax.dot_general(kz, vb, (((0,), (0,)), ((), ())),
                                                   preferred_element_type=F32)
        out = intra + cross
        mu = jnp.mean(out, axis=-1, keepdims=True)
        xc = out - mu
        y = xc * lax.rsqrt(jnp.mean(xc * xc, axis=-1, keepdims=True) + EPS)
        y = y * gng + gnb
        o_ref[rows, :] = (y * gate_ref[rows, :].astype(F32)).astype(o_ref.dtype)

    for c in range(S // L):
        step(c)


def _retention_tables(L):
    log_g = jnp.log1p(-(2.0 ** (-5.0 - jnp.arange(RET_HEADS, dtype=F32))))
    idx = jnp.arange(L, dtype=F32)
    chunk = jnp.arange(L) // CHUNK
    decay = jnp.exp(log_g[:, None, None] * jnp.abs(idx[:, None] - idx[None, :]))
    dm = jnp.where(chunk[None, :, None] >= chunk[None, None, :], decay, 0.0)
    qd = jnp.exp(log_g[:, None] * (idx + 1.0)[None, :])[:, :, None]
    kd = jnp.exp(log_g[:, None] * (L - 1 - idx)[None, :])[:, :, None]
    sd = jnp.broadcast_to(jnp.exp(log_g * L)[:, None, None], (RET_HEADS, 1, RET_DV))
    return dm, qd, kd, sd


def _retention(z, gn_g, gn_b, B, S):
    L = _tile(S, RET_SUPER, CHUNK)
    z3 = z.reshape(B, S, RET_IN_WIDTH)
    dm, qd, kd, sd = _retention_tables(L)
    qk_blocks = RET_QK_WIDTH // RET_DK
    v_first = 2 * RET_QK_WIDTH // RET_DV
    g_first = v_first + RET_V_WIDTH // RET_DV
    out = pl.pallas_call(
        functools.partial(_retention_kernel, L=L),
        grid=(B, RET_HEADS),
        in_specs=[pl.BlockSpec((None, S, RET_DK), lambda b, h: (b, 0, h)),
                  pl.BlockSpec((None, S, RET_DK), lambda b, h: (b, 0, qk_blocks + h)),
                  pl.BlockSpec((None, S, RET_DV), lambda b, h: (b, 0, v_first + h)),
                  pl.BlockSpec((None, S, RET_DV), lambda b, h: (b, 0, g_first + h)),
                  pl.BlockSpec((None, L, L), lambda b, h: (h, 0, 0)),
                  pl.BlockSpec((None, L, 1), lambda b, h: (h, 0, 0)),
                  pl.BlockSpec((None, L, 1), lambda b, h: (h, 0, 0)),
                  pl.BlockSpec((None, 1, RET_DV), lambda b, h: (h, 0, 0)),
                  pl.BlockSpec((1, RET_DV), lambda b, h: (0, h)),
                  pl.BlockSpec((1, RET_DV), lambda b, h: (0, h))],
        out_specs=pl.BlockSpec((None, S, RET_DV), lambda b, h: (b, 0, h)),
        out_shape=jax.ShapeDtypeStruct((B, S, RET_V_WIDTH), BF16),
        scratch_shapes=[pltpu.VMEM((RET_DK, RET_DV), F32)],
        compiler_params=_params(("parallel", "arbitrary"), ((S, RET_DK), BF16, 4), ((S, RET_DV), BF16, 6),
                                ((L, L), F32, 4), ((L, V7X_LANES), F32, 4),
                                ((RET_DK, RET_DV), F32, 4), ((L, RET_DV), F32, 8)),
        name="retention",
    )(z3, z3, z3, z3, dm, qd, kd, sd, gn_g.reshape(1, -1), gn_b.reshape(1, -1))
    return out.reshape(B * S, RET_V_WIDTH)


def _res_mm_kernel(x_ref, w_ref, h_ref, g_ref, o_ref, on_ref):
    hnew = h_ref[...] + jnp.dot(x_ref[...], w_ref[...], preferred_element_type=F32)
    o_ref[...] = hnew
    on_ref[...] = _rmsnorm_rows(hnew, g_ref[...]).astype(on_ref.dtype)


def _res_matmul(x, w_all, layer, h, g_next):
    T, K = x.shape
    D = h.shape[1]
    tm = _tile(T, RES_ROWS)
    row = lambda i: (i, 0)
    fix = lambda i: (0, 0)
    return pl.pallas_call(
        _res_mm_kernel,
        grid=(T // tm,),
        in_specs=[pl.BlockSpec((tm, K), row), _const_spec((None, K, D), lambda i: (layer, 0, 0)),
                  pl.BlockSpec((tm, D), row), pl.BlockSpec((1, D), fix)],
        out_specs=[pl.BlockSpec((tm, D), row), pl.BlockSpec((tm, D), row)],
        out_shape=[jax.ShapeDtypeStruct((T, D), F32), jax.ShapeDtypeStruct((T, D), BF16)],
        compiler_params=_params(("parallel",), ((tm, K), BF16, 2), ((K, D), BF16, 1),
                                ((tm, D), F32, 6), ((tm, D), BF16, 2)),
        name="res_proj",
    )(x, w_all, h, g_next.reshape(1, D))


def _ffn_up_kernel(x_ref, wg_ref, wv_ref, cwg_ref, cwv_ref, cbg_ref, cbv_ref, o_ref, wb_ref, e_ref,
                   *, tiles_per_seq, sub):
    tm = x_ref.shape[0]
    tn = o_ref.shape[1]
    halo = V7X_SUBLANES
    lanes = V7X_LANES
    nslab = tn // lanes
    i = pl.program_id(1)
    first = (i % tiles_per_seq) == 0

    @pl.when(i == 0)
    def _():
        wb_ref[:, 0:tn] = wg_ref[...].astype(BF16)
        wb_ref[:, tn:2 * tn] = wv_ref[...].astype(BF16)

    @pl.when(first)
    def _():
        e_ref[:, 0:halo, :] = jnp.zeros((e_ref.shape[0], halo, lanes), F32)

    @pl.when(jnp.logical_not(first))
    def _():
        e_ref[:, 0:halo, :] = e_ref[:, tm:tm + halo, :]

    def conv(k, r0, cw_ref, cb_ref, kc):
        cols = slice(kc * lanes, (kc + 1) * lanes)
        c = cb_ref[:, cols]
        for j in range(CONV_WIDTH):
            off = halo + r0 - (CONV_WIDTH - 1) + j
            c = c + e_ref[k, off:off + sub, :] * cw_ref[j:j + 1, cols]
        return c

    for r0 in range(0, tm, sub):
        a = jnp.dot(x_ref[r0:r0 + sub, :], wb_ref[...], preferred_element_type=F32)
        for k in range(2 * nslab):
            e_ref[k, halo + r0:halo + r0 + sub, :] = a[:, k * lanes:(k + 1) * lanes]
        for k in range(nslab):
            gate = conv(k, r0, cwg_ref, cbg_ref, k)
            val = conv(k + nslab, r0, cwv_ref, cbv_ref, k)
            o_ref[r0:r0 + sub, k * lanes:(k + 1) * lanes] = (jax.nn.gelu(gate) * val).astype(o_ref.dtype)


def _ffn_up(hn, w_up_all, conv_w_all, conv_b_all, layer, S):
    T, D = hn.shape
    tn = FFN_UP_COLS
    tm = _tile(S, FFN_UP_ROWS)
    sub = _tile(tm, FFN_UP_BLOCK)
    nj = D_FF // tn
    halo = V7X_SUBLANES
    conv_b3 = conv_b_all.reshape(conv_b_all.shape[0], 1, 2 * D_FF)
    gate_col = lambda j, i: (layer, 0, j)
    val_col = lambda j, i: (layer, 0, nj + j)
    return pl.pallas_call(
        functools.partial(_ffn_up_kernel, tiles_per_seq=S // tm, sub=sub),
        grid=(nj, T // tm),
        in_specs=[pl.BlockSpec((tm, D), lambda j, i: (i, 0)),
                  pl.BlockSpec((None, D, tn), gate_col),
                  pl.BlockSpec((None, D, tn), val_col),
                  pl.BlockSpec((None, CONV_WIDTH, tn), gate_col),
                  pl.BlockSpec((None, CONV_WIDTH, tn), val_col),
                  pl.BlockSpec((None, 1, tn), gate_col),
                  pl.BlockSpec((None, 1, tn), val_col)],
        out_specs=pl.BlockSpec((tm, tn), lambda j, i: (i, j)),
        out_shape=jax.ShapeDtypeStruct((T, D_FF), BF16),
        scratch_shapes=[pltpu.VMEM((D, 2 * tn), BF16),
                        pltpu.VMEM((2 * tn // V7X_LANES, tm + halo, V7X_LANES), F32)],
        compiler_params=_params(("arbitrary", "arbitrary"), ((tm, D), BF16, 2), ((D, tn), F32, 4),
                                ((D, 2 * tn), BF16, 1), ((tm + halo, 2 * tn), F32, 1),
                                ((sub, 2 * tn), F32, 2), ((tm, tn), BF16, 2)),
        name="ffn_up",
    )(hn, w_up_all, w_up_all, conv_w_all, conv_w_all, conv_b3, conv_b3)


def _ple_kernel(hn_ref, wg_ref, p_ref, wu_ref, h_ref, g_ref, *out_refs):
    on_ref = out_refs[-1]
    gate = jax.nn.sigmoid(jnp.dot(hn_ref[...], wg_ref[...], preferred_element_type=F32))
    up = jnp.dot(p_ref[...].astype(BF16), wu_ref[...], preferred_element_type=F32)
    hnew = h_ref[...] + up * gate
    if len(out_refs) == 2:
        out_refs[0][...] = hnew
    on_ref[...] = _rmsnorm_rows(hnew, g_ref[...]).astype(on_ref.dtype)


def _ple(hn, wg_all, p_all, layer, wu_all, h, g_next, last):
    T, D = h.shape
    tm = _tile(T, RES_ROWS)
    row = lambda i: (i, 0)
    fix = lambda i: (0, 0)
    out_specs = [pl.BlockSpec((tm, D), row), pl.BlockSpec((tm, D), row)]
    out_shape = [jax.ShapeDtypeStruct((T, D), F32), jax.ShapeDtypeStruct((T, D), BF16)]
    if last:
        out_specs, out_shape = out_specs[:1], out_shape[:1]
    return pl.pallas_call(
        _ple_kernel,
        grid=(T // tm,),
        in_specs=[pl.BlockSpec((tm, D), row),
                  _const_spec((None, D, D), lambda i: (layer, 0, 0)),
                  pl.BlockSpec((None, tm, PLE_DIM), lambda i: (layer, i, 0)),
                  _const_spec((None, PLE_DIM, D), lambda i: (layer, 0, 0)),
                  pl.BlockSpec((tm, D), row),
                  pl.BlockSpec((1, D), fix)],
        out_specs=out_specs,
        out_shape=out_shape,
        compiler_params=_params(("parallel",), ((tm, D), BF16, 2), ((D, D), BF16, 1), ((PLE_DIM, D), BF16, 1),
                                ((tm, PLE_DIM), F32, 2), ((tm, D), F32, 8)),
        name="ple",
    )(hn, wg_all, p_all, wu_all, h, g_next.reshape(1, D))


def _rot_cols(w):
    half = MLA_ROPE // 2
    return jnp.concatenate([-w[..., half:], w[..., :half]], axis=-1)


def _even_in_weight(w):
    w = w.astype(BF16)
    cq = w[..., :Q_LORA]
    ckv = w[..., Q_LORA:Q_LORA + KV_LORA]
    kpe = w[..., Q_LORA + KV_LORA:Q_LORA + KV_LORA + MLA_ROPE]
    uv = w[..., Q_LORA + KV_LORA + MLA_ROPE:]
    used = 2 * SGU_WIDTH + Q_LORA + KV_LORA + 2 * MLA_ROPE
    pad = jnp.zeros(w.shape[:-1] + (EVEN_IN_PAD - used,), w.dtype)
    return jnp.concatenate([uv, cq, ckv, kpe, _rot_cols(kpe), pad], axis=-1).astype(BF16)


def _q_up_weight(w):
    w = w.reshape(Q_LORA, MLA_HEADS, MLA_NOPE + MLA_ROPE)
    rope = w[..., MLA_NOPE:]
    w = jnp.concatenate([w[..., :MLA_NOPE], rope, _rot_cols(rope)], axis=-1)
    return w.reshape(Q_LORA, MLA_HEADS * MLA_QK_PAD).astype(BF16)


def _kv_up_weight(w):
    w = w.reshape(KV_LORA, MLA_HEADS, MLA_NOPE + MLA_VDIM)
    k = w[..., :MLA_NOPE].reshape(KV_LORA, -1)
    v = w[..., MLA_NOPE:].reshape(KV_LORA, -1)
    return jnp.concatenate([k, v], axis=1).astype(BF16)


def kernel(x, p, positions, mix_norm_g, even_w_in, mla_q_norm_g, mla_w_q_up, mla_kv_norm_g, mla_w_kv_up,
           sgu_ln_g, sgu_ln_b, sgu_w_s, sgu_b_s, even_w_out, ret_w_in, ret_gn_g, ret_gn_b, ret_w_out,
           ffn_norm_g, ffn_w_up, ffn_conv_w, ffn_conv_b, ffn_w_down, ple_norm_g, ple_w_gate, ple_w_up,
           final_norm_g):
    B, S, D = x.shape
    depth = p.shape[0]
    T = B * S
    assert D == D_MODEL and S % SGU_BLOCK == 0
    cos_t, sin_t, m_tab = _rope_tables(positions)
    h = x.reshape(T, D)
    p_all = p.reshape(depth, T, PLE_DIM)
    even_in_b = _even_in_weight(even_w_in)
    even_out_b = even_w_out.astype(BF16)
    ret_in_b = ret_w_in.astype(BF16)
    ret_out_b = ret_w_out.astype(BF16)
    ffn_down_b = ffn_w_down.astype(BF16)
    ple_gate_b = ple_w_gate.astype(BF16)
    ple_up_b = ple_w_up.astype(BF16)
    for i in range(depth):
        j = i // 2
        if i % 2 == 0:
            if i == 0:
                z = _norm_matmul(h, mix_norm_g[0], even_in_b, j)
            else:
                z = _matmul(hn, even_in_b, j)
            q, k, v = _mla_prep(z, m_tab, mla_q_norm_g[j], mla_kv_norm_g[j],
                                _q_up_weight(mla_w_q_up[j]), _kv_up_weight(mla_w_kv_up[j]))
            a = _mla_attn(q, k, v, B, S)
            h, hn = _even_out(a, z, h, sgu_ln_g[j], sgu_ln_b[j], sgu_w_s[j], sgu_b_s[j],
                              even_out_b, j, ffn_norm_g[i])
        else:
            z = _ret_in_proj(hn, ret_in_b, j, cos_t, sin_t)
            r = _retention(z, ret_gn_g[j], ret_gn_b[j], B, S)
            h, hn = _res_matmul(r, ret_out_b, j, h, ffn_norm_g[i])
        act = _ffn_up(hn, ffn_w_up, ffn_conv_w, ffn_conv_b, i, S)
        h, hn = _res_matmul(act, ffn_down_b, i, h, ple_norm_g[i])
        if i < depth - 1:
            h, hn = _ple(hn, ple_gate_b, p_all, i, ple_up_b, h, mix_norm_g[i + 1], False)
        else:
            out, = _ple(hn, ple_gate_b, p_all, i, ple_up_b, h, final_norm_g, True)
    return out.reshape(B, S, D)
```
